```python
import math
import jax, jax.numpy as jnp
from jax import lax
import numpy as np

D_MODEL = 2048
BATCH = 2
SEQ = 4096
DEPTH = 2

N_META = 16
CHUNK = 64
N_PAD = CHUNK - N_META
NEG = -1e30
EPS = 1e-6

ML_HEADS = 4
ML_DQK = 128
ML_DV = 256
ML_GATE_CAP = 15.0
ML_COLS = ML_HEADS * (2 * ML_DQK + 2 * ML_DV + 2)

RW_HEADS = 16
RW_N = 64
RW_DIM = RW_HEADS * RW_N
RW_DECAY_LORA = 64
RW_A_LORA = 64
RW_GATE_LORA = 160
RW_LN_EPS = 64e-5
RW_COLS = 3 * RW_DIM + RW_DECAY_LORA + RW_A_LORA + RW_GATE_LORA

E_IN_COLS = ML_COLS + RW_COLS
MIX0_OUT = ML_HEADS * ML_DV + RW_DIM

SSD_DI = 2 * D_MODEL
SSD_P = 64
SSD_H = SSD_DI // SSD_P
SSD_N = 128
SSD_G = 8
SSD_R = SSD_H // SSD_G
SSD_K = 4
SSD_CONV_CH = SSD_DI + 2 * SSD_G * SSD_N
SSD_IN_COLS = SSD_DI + SSD_CONV_CH + SSD_H

D_FF = ((8 * D_MODEL // 3 + 255) // 256) * 256
N_EXPERTS = 8
TOP_K = 2
D_FF_EXPERT = D_FF

N_EVEN = (DEPTH + 1) // 2
N_ODD = DEPTH // 2

kernel_name = "hybrid_mlstm_rwkv7_ssd_moe"


def rmsnorm(x, w):
    xf = x.astype(jnp.float32)
    y = xf * lax.rsqrt(jnp.mean(xf * xf, -1, keepdims=True) + EPS)
    return (y * w.astype(jnp.float32)).astype(x.dtype)


def swiglu(x, wg, wu, wd):
    return (jax.nn.silu(x @ wg) * (x @ wu)) @ wd


def token_shift(p):
    return jnp.pad(p, ((0, 0), (1, 0), (0, 0)))[:, :-1]


def soft_cap(x, cap):
    return cap * jnp.tanh(x / cap)


def mlstm_chunkwise(q, k, v, i_pre, logf):
    B_, H, T, DQK = q.shape
    DV = v.shape[-1]
    nc = T // CHUNK
    q = q.reshape(B_, H, nc, CHUNK, DQK)
    k = k.reshape(B_, H, nc, CHUNK, DQK)
    v = v.reshape(B_, H, nc, CHUNK, DV)
    i_pre = i_pre.reshape(B_, H, nc, CHUNK)
    b = jnp.cumsum(logf.reshape(B_, H, nc, CHUNK), -1)
    g = b[..., -1]
    a = g[..., None] - b + i_pre

    def step(carry, inp):
        C, n, m = carry
        k_c, v_c, a_c, g_c = inp
        m_new = jnp.maximum(g_c + m, jnp.max(a_c, -1))
        dec = jnp.exp(g_c + m - m_new)
        w = jnp.exp(a_c - m_new[..., None])
        C_new = dec[..., None, None] * C + jnp.einsum('bhl,bhld,bhle->bhde', w, k_c, v_c)
        n_new = dec[..., None] * n + jnp.einsum('bhl,bhld->bhd', w, k_c)
        return (C_new, n_new, m_new), (C, n, m)

    init = (jnp.zeros((B_, H, DQK, DV), jnp.float32), jnp.zeros((B_, H, DQK), jnp.float32),
            jnp.zeros((B_, H), jnp.float32))
    xs = (jnp.moveaxis(k, 2, 0), jnp.moveaxis(v, 2, 0), jnp.moveaxis(a, 2, 0), jnp.moveaxis(g, 2, 0))
    _, (C_prev, n_prev, m_prev) = lax.scan(step, init, xs)
    C_prev = jnp.moveaxis(C_prev, 0, 2)
    n_prev = jnp.moveaxis(n_prev, 0, 2)
    m_prev = jnp.moveaxis(m_prev, 0, 2)

    causal = jnp.tril(jnp.ones((CHUNK, CHUNK), bool))
    Dm = jnp.where(causal, b[..., :, None] - b[..., None, :] + i_pre[..., None, :], NEG)
    m_inter = b + m_prev[..., None]
    m_t = jnp.maximum(m_inter, jnp.max(Dm, -1))
    S = jnp.einsum('bhcld,bhcsd->bhcls', q, k) * jnp.exp(Dm - m_t[..., None])
    w_inter = jnp.exp(m_inter - m_t)
    num = (jnp.einsum('bhcls,bhcse->bhcle', S, v)
           + w_inter[..., None] * jnp.einsum('bhcld,bhcde->bhcle', q, C_prev))
    den = jnp.sum(S, -1) + w_inter * jnp.einsum('bhcld,bhcd->bhcl', q, n_prev)
    h = num / jnp.maximum(jnp.abs(den), jnp.exp(-m_t))[..., None]
    return h.reshape(B_, H, T, DV)


def mlstm_group(pm, valid, i_bias, f_bias, out_norm):
    pm = pm.astype(jnp.float32)
    B_, T, _ = pm.shape
    s = np.cumsum([ML_HEADS * ML_DQK, ML_HEADS * ML_DQK, ML_HEADS * ML_DV, ML_HEADS * ML_DV, ML_HEADS])
    q, k, v, o, ig, fg = jnp.split(pm, [int(c) for c in s], axis=-1)
    heads = lambda t, d: t.reshape(B_, T, ML_HEADS, d).transpose(0, 2, 1, 3)
    q = heads(q, ML_DQK) * (ML_DQK ** -0.5)
    k = heads(k, ML_DQK)
    v = heads(v, ML_DV)
    ig = soft_cap(ig + i_bias, ML_GATE_CAP)
    logf = jax.nn.log_sigmoid(soft_cap(fg + f_bias, ML_GATE_CAP))
    ig = jnp.where(valid[:, None], ig, NEG)
    logf = jnp.where(valid[:, None], logf, 0.0)
    h = mlstm_chunkwise(q, k, v, ig.transpose(0, 2, 1), logf.transpose(0, 2, 1))
    h = h.transpose(0, 2, 1, 3)
    h = h * lax.rsqrt(jnp.mean(h * h, -1, keepdims=True) + EPS)
    h = h.reshape(B_, T, ML_HEADS * ML_DV) * out_norm
    return h * jax.nn.sigmoid(o)


def rwkv7_group(pr, valid, mu, w0, w_up, a0, a_up, g_up, k_k, k_a, r_k, ln_w, ln_b):
    pr = pr.astype(jnp.float32)
    B_, T, _ = pr.shape
    pr = pr + mu * (token_shift(pr) - pr)
    s = [RW_DIM, 2 * RW_DIM, 3 * RW_DIM, 3 * RW_DIM + RW_DECAY_LORA, 3 * RW_DIM + RW_DECAY_LORA + RW_A_LORA]
    r, k, v, wd, ad, gd = jnp.split(pr, s, axis=-1)
    w_log = -jax.nn.softplus(-(w0 + jnp.tanh(wd) @ w_up)) - 0.5
    decay = jnp.exp(-jnp.exp(w_log))
    a = jax.nn.sigmoid(a0 + ad @ a_up)
    g = jax.nn.sigmoid(gd) @ g_up
    heads = lambda t: t.reshape(B_, T, RW_HEADS, RW_N)
    kk = heads(k * k_k)
    kk = kk * lax.rsqrt(jnp.maximum(jnp.sum(kk * kk, -1, keepdims=True), 1e-24))
    k = k * (1.0 + (a - 1.0) * k_a)
    vm = valid[:, None]
    decay = jnp.where(vm, decay, 1.0)
    k = jnp.where(vm, k, 0.0)
    kk = jnp.where(valid[:, None, None], kk, 0.0)
    r_h, k_h, v_h, w_h, a_h = heads(r), heads(k), heads(v), heads(decay), heads(a)
    vec_a = -kk
    vec_b = kk * a_h

    def step(S, inp):
        r_t, w_t, k_t, v_t, a_t, b_t = inp
        sa = jnp.einsum('bhij,bhj->bhi', S, a_t)
        S = S * w_t[:, :, None, :] + sa[..., None] * b_t[:, :, None, :] + v_t[..., None] * k_t[:, :, None, :]
        return S, jnp.einsum('bhij,bhj->bhi', S, r_t)

    tm = lambda t: t.transpose(1, 0, 2, 3)
    S0 = jnp.zeros((B_, RW_HEADS, RW_N, RW_N), jnp.float32)
    _, y = lax.scan(step, S0, (tm(r_h), tm(w_h), tm(k_h), tm(v_h), tm(vec_a), tm(vec_b)))
    y = y.transpose(1, 0, 2, 3)
    mean = jnp.mean(y, -1, keepdims=True)
    var = jnp.mean(jnp.square(y - mean), -1, keepdims=True)
    y = ((y - mean) * lax.rsqrt(var + RW_LN_EPS)).reshape(B_, T, RW_DIM) * ln_w + ln_b
    bonus = jnp.sum(r_h * k_h * r_k, -1, keepdims=True) * v_h
    y = y + bonus.reshape(B_, T, RW_DIM)
    return y * g


def even_mixer(hn, valid, w_in, ml_i_bias, ml_f_bias, ml_out_norm, rw_mu, rw_w0, rw_w_up, rw_a0,
               rw_a_up, rw_g_up, rw_k_k, rw_k_a, rw_r_k, rw_ln_w, rw_ln_b, w_out):
    p = hn @ w_in
    y_ml = mlstm_group(p[..., :ML_COLS], valid, ml_i_bias, ml_f_bias, ml_out_norm)
    y_rw = rwkv7_group(p[..., ML_COLS:], valid, rw_mu, rw_w0, rw_w_up, rw_a0, rw_a_up, rw_g_up,
                       rw_k_k, rw_k_a, rw_r_k, rw_ln_w, rw_ln_b)
    y = jnp.concatenate([y_ml, y_rw], axis=-1).astype(hn.dtype)
    return y @ w_out


def ssd_mixer(hn, valid, w_in, conv_w, conv_b, dt_bias, A_log, D_skip, norm_w, w_out):
    B_, T, _ = hn.shape
    nc = T // CHUNK
    p = hn @ w_in
    z, xbc, dt = jnp.split(p, [SSD_DI, SSD_DI + SSD_CONV_CH], axis=-1)
    xbc = lax.conv_general_dilated(xbc, conv_w[:, None, :], window_strides=(1,), padding=[(SSD_K - 1, 0)],
                                   dimension_numbers=('NWC', 'WIO', 'NWC'),
                                   feature_group_count=SSD_CONV_CH) + conv_b
    xbc = jax.nn.silu(xbc.astype(jnp.float32))
    xs, Bm, Cm = jnp.split(xbc, [SSD_DI, SSD_DI + SSD_G * SSD_N], axis=-1)
    dt = jax.nn.softplus(dt.astype(jnp.float32) + dt_bias)
    dt = jnp.where(valid[:, None], dt, 0.0)
    A = -jnp.exp(A_log.astype(jnp.float32))
    x_h = xs.reshape(B_, nc, CHUNK, SSD_G, SSD_R, SSD_P)
    Bm = Bm.reshape(B_, nc, CHUNK, SSD_G, SSD_N)
    Cm = Cm.reshape(B_, nc, CHUNK, SSD_G, SSD_N)
    dt = dt.reshape(B_, nc, CHUNK, SSD_G, SSD_R)
    a_cum = jnp.cumsum(dt * A.reshape(SSD_G, SSD_R), axis=2)
    xdt = x_h * dt[..., None]
    causal = jnp.tril(jnp.ones((CHUNK, CHUNK), bool))[:, :, None, None]
    seg = a_cum[:, :, :, None] - a_cum[:, :, None, :]
    Ldec = jnp.exp(jnp.where(causal, seg, NEG))
    CB = jnp.einsum('bclgn,bcsgn->bclsg', Cm, Bm)
    y_intra = jnp.einsum('bclsgr,bcsgrp->bclgrp', CB[..., None] * Ldec, xdt)
    decay_end = jnp.exp(a_cum[:, :, -1:] - a_cum)
    states = jnp.einsum('bcsgn,bcsgr,bcsgrp->bcgrpn', Bm, decay_end, xdt)
    chunk_decay = jnp.exp(a_cum[:, :, -1])

    def step(S, inp):
        st, dec = inp
        return dec[..., None, None] * S + st, S

    S0 = jnp.zeros((B_, SSD_G, SSD_R, SSD_P, SSD_N), jnp.float32)
    _, S_prev = lax.scan(step, S0, (jnp.moveaxis(states, 1, 0), jnp.moveaxis(chunk_decay, 1, 0)))
    S_prev = jnp.moveaxis(S_prev, 0, 1)
    y_inter = jnp.einsum('bclgn,bcgrpn,bclgr->bclgrp', Cm, S_prev, jnp.exp(a_cum))
    y = y_intra + y_inter + D_skip.reshape(SSD_G, SSD_R)[:, :, None] * x_h
    y = y.reshape(B_, T, SSD_DI) * jax.nn.silu(z.astype(jnp.float32))
    yg = y.reshape(B_, T, SSD_G, SSD_DI // SSD_G)
    yg = yg * lax.rsqrt(jnp.mean(yg * yg, -1, keepdims=True) + EPS)
    y = yg.reshape(B_, T, SSD_DI) * norm_w
    return y.astype(hn.dtype) @ w_out


def moe_swiglu(x, router, w_gate, w_up, w_down):
    logits = (x @ router).astype(jnp.float32)
    top_val, top_idx = lax.top_k(logits, TOP_K)
    top_w = jax.nn.softmax(top_val, axis=-1)
    gates = jnp.sum(jax.nn.one_hot(top_idx, N_EXPERTS, dtype=jnp.float32) * top_w[..., None], axis=-2)
    out = jnp.zeros(x.shape, jnp.float32)
    for e in range(N_EXPERTS):
        out = out + gates[..., e:e + 1] * swiglu(x, w_gate[e], w_up[e], w_down[e])
    return out.astype(x.dtype)


def setup_inputs(seed: int = 0) -> dict:
    key = jax.random.key(seed)
    ks = iter(jax.random.split(key, 64))
    f32 = jnp.float32

    def nrm(shape, scale):
        return scale * jax.random.normal(next(ks), shape, f32)

    NE, NO = N_EVEN, N_ODD
    x = nrm((BATCH, SEQ, D_MODEL), 1.0)
    meta_tokens = nrm((N_META, D_MODEL), 1.0)
    e_norm_mix = 1.0 + nrm((NE, D_MODEL), 0.02)
    e_w_in = nrm((NE, D_MODEL, E_IN_COLS), D_MODEL ** -0.5)
    ml_i_bias = nrm((NE, ML_HEADS), 0.1)
    ml_f_bias = jnp.linspace(3.0, 6.0, ML_HEADS, dtype=f32)[None] + nrm((NE, ML_HEADS), 0.1)
    ml_out_norm = 1.0 + nrm((NE, ML_HEADS * ML_DV), 0.02)
    rw_mu = jax.random.uniform(next(ks), (NE, RW_COLS), f32)
    rw_w0 = jnp.linspace(-6.0, -1.0, RW_DIM, dtype=f32)[None] + nrm((NE, RW_DIM), 0.1)
    rw_w_up = nrm((NE, RW_DECAY_LORA, RW_DIM), 0.1)
    rw_a0 = nrm((NE, RW_DIM), 0.1)
    rw_a_up = nrm((NE, RW_A_LORA, RW_DIM), RW_A_LORA ** -0.5)
    rw_g_up = nrm((NE, RW_GATE_LORA, RW_DIM), RW_GATE_LORA ** -0.5)
    rw_k_k = 0.85 + nrm((NE, RW_DIM), 0.02)
    rw_k_a = 1.0 + nrm((NE, RW_DIM), 0.02)
    rw_r_k = nrm((NE, RW_HEADS, RW_N), 0.1)
    rw_ln_w = 1.0 + nrm((NE, RW_DIM), 0.02)
    rw_ln_b = nrm((NE, RW_DIM), 0.02)
    e_w_out = nrm((NE, MIX0_OUT, D_MODEL), MIX0_OUT ** -0.5)
    e_norm_ffn = 1.0 + nrm((NE, D_MODEL), 0.02)
    ffn_w_gate = nrm((NE, D_MODEL, D_FF), D_MODEL ** -0.5)
    ffn_w_up = nrm((NE, D_MODEL, D_FF), D_MODEL ** -0.5)
    ffn_w_down = nrm((NE, D_FF, D_MODEL), D_FF ** -0.5)
    o_norm_mix = 1.0 + nrm((NO, D_MODEL), 0.02)
    ssd_w_in = nrm((NO, D_MODEL, SSD_IN_COLS), D_MODEL ** -0.5)
    ssd_conv_w = nrm((NO, SSD_K, SSD_CONV_CH), SSD_K ** -0.5)
    ssd_conv_b = nrm((NO, SSD_CONV_CH), 0.02)
    dt0 = jnp.exp(jax.random.uniform(next(ks), (NO, SSD_H), f32, math.log(1e-3), math.log(1e-1)))
    ssd_dt_bias = dt0 + jnp.log(-jnp.expm1(-dt0))
    ssd_A_log = jnp.log(jax.random.uniform(next(ks), (NO, SSD_H), f32, 1.0, 16.0))
    ssd_D = 1.0 + nrm((NO, SSD_H), 0.1)
    ssd_norm = 1.0 + nrm((NO, SSD_DI), 0.02)
    ssd_w_out = nrm((NO, SSD_DI, D_MODEL), SSD_DI ** -0.5)
    o_norm_ffn = 1.0 + nrm((NO, D_MODEL), 0.02)
    moe_router = nrm((NO, D_MODEL, N_EXPERTS), D_MODEL ** -0.5)
    moe_w_gate = nrm((NO, N_EXPERTS, D_MODEL, D_FF_EXPERT), D_MODEL ** -0.5)
    moe_w_up = nrm((NO, N_EXPERTS, D_MODEL, D_FF_EXPERT), D_MODEL ** -0.5)
    moe_w_down = nrm((NO, N_EXPERTS, D_FF_EXPERT, D_MODEL), D_FF_EXPERT ** -0.5)
    final_norm = 1.0 + nrm((D_MODEL,), 0.02)
    return {"x": x, "meta_tokens": meta_tokens, "e_norm_mix": e_norm_mix, "e_w_in": e_w_in,
            "ml_i_bias": ml_i_bias, "ml_f_bias": ml_f_bias, "ml_out_norm": ml_out_norm,
            "rw_mu": rw_mu, "rw_w0": rw_w0, "rw_w_up": rw_w_up, "rw_a0": rw_a0, "rw_a_up": rw_a_up,
            "rw_g_up": rw_g_up, "rw_k_k": rw_k_k, "rw_k_a": rw_k_a, "rw_r_k": rw_r_k,
            "rw_ln_w": rw_ln_w, "rw_ln_b": rw_ln_b, "e_w_out": e_w_out, "e_norm_ffn": e_norm_ffn,
            "ffn_w_gate": ffn_w_gate, "ffn_w_up": ffn_w_up, "ffn_w_down": ffn_w_down,
            "o_norm_mix": o_norm_mix, "ssd_w_in": ssd_w_in, "ssd_conv_w": ssd_conv_w,
            "ssd_conv_b": ssd_conv_b, "ssd_dt_bias": ssd_dt_bias, "ssd_A_log": ssd_A_log,
            "ssd_D": ssd_D, "ssd_norm": ssd_norm, "ssd_w_out": ssd_w_out, "o_norm_ffn": o_norm_ffn,
            "moe_router": moe_router, "moe_w_gate": moe_w_gate, "moe_w_up": moe_w_up,
            "moe_w_down": moe_w_down, "final_norm": final_norm}


def reference(x, meta_tokens, e_norm_mix, e_w_in, ml_i_bias, ml_f_bias, ml_out_norm, rw_mu, rw_w0,
              rw_w_up, rw_a0, rw_a_up, rw_g_up, rw_k_k, rw_k_a, rw_r_k, rw_ln_w, rw_ln_b, e_w_out,
              e_norm_ffn, ffn_w_gate, ffn_w_up, ffn_w_down, o_norm_mix, ssd_w_in, ssd_conv_w, ssd_conv_b,
              ssd_dt_bias, ssd_A_log, ssd_D, ssd_norm, ssd_w_out, o_norm_ffn, moe_router, moe_w_gate,
              moe_w_up, moe_w_down, final_norm):
    B_, S_, D_ = x.shape
    T = N_PAD + N_META + S_
    h = jnp.concatenate([jnp.zeros((B_, N_PAD, D_), x.dtype),
                         jnp.broadcast_to(meta_tokens.astype(x.dtype), (B_, N_META, D_)), x], axis=1)
    valid = jnp.arange(T) >= N_PAD
    valid_f = valid[None, :, None].astype(x.dtype)
    for layer in range(DEPTH):
        j = layer // 2
        if layer % 2 == 0:
            hn = rmsnorm(h, e_norm_mix[j])
            out = even_mixer(hn, valid, e_w_in[j], ml_i_bias[j], ml_f_bias[j], ml_out_norm[j], rw_mu[j],
                             rw_w0[j], rw_w_up[j], rw_a0[j], rw_a_up[j], rw_g_up[j], rw_k_k[j], rw_k_a[j],
                             rw_r_k[j], rw_ln_w[j], rw_ln_b[j], e_w_out[j])
            h = h + (out * valid_f).astype(h.dtype)
            hn = rmsnorm(h, e_norm_ffn[j])
            out = swiglu(hn, ffn_w_gate[j], ffn_w_up[j], ffn_w_down[j])
            h = h + (out * valid_f).astype(h.dtype)
        else:
            hn = rmsnorm(h, o_norm_mix[j])
            out = ssd_mixer(hn, valid, ssd_w_in[j], ssd_conv_w[j], ssd_conv_b[j], ssd_dt_bias[j],
                            ssd_A_log[j], ssd_D[j], ssd_norm[j], ssd_w_out[j])
            h = h + (out * valid_f).astype(h.dtype)
            hn = rmsnorm(h, o_norm_ffn[j])
            out = moe_swiglu(hn, moe_router[j], moe_w_gate[j], moe_w_up[j], moe_w_down[j])
            h = h + (out * valid_f).astype(h.dtype)
    y = rmsnorm(h, final_norm)[:, N_PAD + N_META:]
    return y
```

```python
import functools

import jax
import jax.numpy as jnp
from jax import lax
from jax.experimental import pallas as pl
from jax.experimental.pallas import tpu as pltpu

F32 = jnp.float32
BF16 = jnp.bfloat16
HI = lax.Precision.HIGHEST

N_META = 16
CHUNK = 64
N_PAD = CHUNK - N_META
NEG = -1e30
EPS = 1e-6

ML_HEADS = 4
ML_DQK = 128
ML_DV = 256
ML_GATE_CAP = 15.0
ML_QK = ML_HEADS * ML_DQK
ML_V = ML_HEADS * ML_DV

RW_HEADS = 16
RW_N = 64
RW_DIM = RW_HEADS * RW_N
RW_DECAY_LORA = 64
RW_A_LORA = 64
RW_GATE_LORA = 160
RW_LN_EPS = 64e-5
RW_GROUP = 256
RW_HPG = RW_GROUP // RW_N
LORA_ROWS = 384

E_TAIL = 512
E_TAIL_GATES = 384
E_COLS = 2 * ML_QK + 2 * ML_V + 3 * RW_DIM + E_TAIL
E_OFF_R = 2 * ML_QK + 2 * ML_V
E_OFF_TAIL = E_OFF_R + 3 * RW_DIM

SSD_DI = 4096
SSD_P = 64
SSD_H = SSD_DI // SSD_P
SSD_N = 128
SSD_G = 8
SSD_R = SSD_H // SSD_G
SSD_K = 4
SSD_GW = SSD_DI // SSD_G
SSD_BC = 2 * SSD_G * SSD_N
SSD_DT_PAD = 128
SSD_COLS = 2 * SSD_DI + SSD_BC + SSD_DT_PAD

N_EXPERTS = 8
ROUTER_PAD = 128

VMEM_LIMIT = 56 * 1024 * 1024


def _tile(n, prefs):
    for p in prefs:
        if n % p == 0:
            return p
    return n


_ROW_TILES = (1040, 832, 640, 520, 512, 256, 128, 64)


def _params(sem):
    return pltpu.CompilerParams(dimension_semantics=sem, vmem_limit_bytes=VMEM_LIMIT)


def _dot(a, b, prec=None):
    return lax.dot_general(a, b, (((1,), (0,)), ((), ())), precision=prec, preferred_element_type=F32)


def _dot_nt(a, b, prec=None):
    return lax.dot_general(a, b, (((1,), (1,)), ((), ())), precision=prec, preferred_element_type=F32)


def _dot_tn(a, b, prec=None):
    return lax.dot_general(a, b, (((0,), (0,)), ((), ())), precision=prec, preferred_element_type=F32)


def _b(x):
    return x.astype(BF16)


def _sigmoid(x):
    return 1.0 / (1.0 + jnp.exp(-x))


def _softplus(x):
    return jnp.maximum(x, 0.0) + jnp.log1p(jnp.exp(-jnp.abs(x)))


def _silu(x):
    return x * _sigmoid(x)


def _iota(shape, dim):
    return lax.broadcasted_iota(jnp.int32, shape, dim)


def _tri(n):
    return (_iota((n, n), 0) >= _iota((n, n), 1)).astype(F32)


def _rmsnorm_kernel(x_ref, w_ref, o_ref):
    x = x_ref[...]
    ms = jnp.mean(x * x, axis=-1, keepdims=True)
    o_ref[...] = (x * lax.rsqrt(ms + EPS) * w_ref[...]).astype(o_ref.dtype)


def rmsnorm(x, w, out_dtype):
    m, d = x.shape
    tm = _tile(m, _ROW_TILES)
    return pl.pallas_call(
        _rmsnorm_kernel,
        grid=(m // tm,),
        in_specs=[pl.BlockSpec((tm, d), lambda i: (i, 0)), pl.BlockSpec((1, d), lambda i: (0, 0))],
        out_specs=pl.BlockSpec((tm, d), lambda i: (i, 0)),
        out_shape=jax.ShapeDtypeStruct((m, d), out_dtype),
        compiler_params=_params(("arbitrary",)),
        name="rmsnorm",
    )(x, w.reshape(1, d))


def _mm_up_kernel(a_ref, w_ref, o_ref, wb_ref):
    @pl.when(pl.program_id(1) == 0)
    def _():
        wb_ref[...] = _b(w_ref[...])

    o_ref[...] = _dot(a_ref[...], wb_ref[...]).astype(o_ref.dtype)


def mm_up(a, w, out_dtype, tn_prefs=(512, 384, 256, 128)):
    m, k = a.shape
    n = w.shape[1]
    tm = _tile(m, _ROW_TILES)
    tn = _tile(n, tn_prefs)
    return pl.pallas_call(
        _mm_up_kernel,
        grid=(n // tn, m // tm),
        in_specs=[pl.BlockSpec((tm, k), lambda j, i: (i, 0)), pl.BlockSpec((k, tn), lambda j, i: (0, j))],
        out_specs=pl.BlockSpec((tm, tn), lambda j, i: (i, j)),
        out_shape=jax.ShapeDtypeStruct((m, n), out_dtype),
        scratch_shapes=[pltpu.VMEM((k, tn), BF16)],
        compiler_params=_params(("arbitrary", "arbitrary")),
        name="mm_up",
    )(a, w)


def _swiglu_kernel(*refs, gate_lane):
    if gate_lane is None:
        a_ref, wg_ref, wu_ref, o_ref, wgb_ref, wub_ref = refs
    else:
        a_ref, wg_ref, wu_ref, gate_ref, o_ref, wgb_ref, wub_ref = refs

    @pl.when(pl.program_id(1) == 0)
    def _():
        wgb_ref[...] = _b(wg_ref[...])
        wub_ref[...] = _b(wu_ref[...])

    a = a_ref[...]
    g = _dot(a, wgb_ref[...])
    u = _dot(a, wub_ref[...])
    h = _silu(g) * u
    if gate_lane is not None:
        h = h * gate_ref[:, gate_lane:gate_lane + 1]
    o_ref[...] = h.astype(o_ref.dtype)


def swiglu_up(a, wg, wu, expert=None, gates=None):
    m, k = a.shape
    f = wg.shape[-1]
    tm = _tile(m, _ROW_TILES)
    tn = _tile(f, (512, 256, 128))
    if expert is None:
        w_spec = pl.BlockSpec((k, tn), lambda j, i: (0, j))
    else:
        w_spec = pl.BlockSpec((None, k, tn), lambda j, i: (expert, 0, j))
    in_specs = [pl.BlockSpec((tm, k), lambda j, i: (i, 0)), w_spec, w_spec]
    args = [a, wg, wu]
    if gates is not None:
        in_specs.append(pl.BlockSpec((tm, gates.shape[1]), lambda j, i: (i, 0)))
        args.append(gates)
    return pl.pallas_call(
        functools.partial(_swiglu_kernel, gate_lane=None if gates is None else expert),
        grid=(f // tn, m // tm),
        in_specs=in_specs,
        out_specs=pl.BlockSpec((tm, tn), lambda j, i: (i, j)),
        out_shape=jax.ShapeDtypeStruct((m, f), BF16),
        scratch_shapes=[pltpu.VMEM((k, tn), BF16), pltpu.VMEM((k, tn), BF16)],
        compiler_params=_params(("arbitrary", "arbitrary")),
        name="swiglu_up",
    )(*args)


def _mm_down_kernel(a_ref, w_ref, r_ref, o_ref, *, tm, seq_t):
    acc = _dot(a_ref[...], w_ref[...])
    t0 = (pl.program_id(0) * tm) % seq_t
    valid = ((t0 + _iota((tm, 1), 0)) >= N_PAD).astype(F32)
    o_ref[...] = r_ref[...] + acc * valid


def mm_down(a, w, res, seq_t, expert=None):
    m, k = a.shape
    n = w.shape[-1]
    tm = _tile(seq_t, (520, 512, 256, 128, 64))
    tn = _tile(n, (512, 256, 128))
    if expert is None:
        w_spec = pl.BlockSpec((k, tn), lambda i, j: (0, j))
    else:
        w_spec = pl.BlockSpec((None, k, tn), lambda i, j: (expert, 0, j))
    return pl.pallas_call(
        functools.partial(_mm_down_kernel, tm=tm, seq_t=seq_t),
        grid=(m // tm, n // tn),
        in_specs=[pl.BlockSpec((tm, k), lambda i, j: (i, 0)), w_spec,
                  pl.BlockSpec((tm, tn), lambda i, j: (i, j))],
        out_specs=pl.BlockSpec((tm, tn), lambda i, j: (i, j)),
        out_shape=jax.ShapeDtypeStruct((m, n), F32),
        compiler_params=_params(("arbitrary", "arbitrary")),
        name="mm_down",
    )(a, w, res)


def _log_sigmoid(x):
    return -_softplus(-x)


def _mlstm_kernel(q_ref, k_ref, v_ref, o_ref, tail_ref, bias_ref, onorm_ref, out_ref, c_ref, n_ref, m_ref):
    c = pl.program_id(1)
    L = CHUNK

    @pl.when(c == 0)
    def _():
        c_ref[...] = jnp.zeros_like(c_ref)
        n_ref[...] = jnp.zeros_like(n_ref)
        m_ref[...] = jnp.zeros_like(m_ref)

    gates = tail_ref[:, E_TAIL_GATES:E_TAIL] + bias_ref[...]
    sc = ML_GATE_CAP * jnp.tanh(gates / ML_GATE_CAP)
    valid = (c * L + _iota((L, 1), 0)) >= N_PAD
    ig = jnp.where(valid, sc, NEG)
    lf = jnp.where(valid, _log_sigmoid(sc), 0.0)
    bcum = _dot(_tri(L), lf, HI)
    b_t = bcum.T
    i_t = ig.T
    causal = _iota((L, L), 0) >= _iota((L, L), 1)
    scale = ML_DQK ** -0.5

    for h in range(ML_HEADS):
        q = q_ref[:, h * ML_DQK:(h + 1) * ML_DQK] * scale
        k = k_ref[:, h * ML_DQK:(h + 1) * ML_DQK]
        v = v_ref[:, h * ML_DV:(h + 1) * ML_DV]
        og = o_ref[:, h * ML_DV:(h + 1) * ML_DV]
        bcol = bcum[:, ML_HEADS + h:ML_HEADS + h + 1]
        icol = ig[:, h:h + 1]
        brow = b_t[ML_HEADS + h:ML_HEADS + h + 1, :]
        irow = i_t[h:h + 1, :]
        g = bcol[L - 1:L, :]
        m_prev = m_ref[h:h + 1, 0:1]
        c_prev = c_ref[h]
        n_prev = n_ref[h:h + 1, :]

        dm = jnp.where(causal, bcol - brow + irow, NEG)
        m_inter = bcol + m_prev
        m_t = jnp.maximum(m_inter, jnp.max(dm, axis=-1, keepdims=True))
        s = _dot_nt(_b(q), _b(k)) * jnp.exp(dm - m_t)
        w_inter = jnp.exp(m_inter - m_t)
        num = _dot(_b(s), _b(v)) + w_inter * _dot(_b(q), _b(c_prev))
        den = jnp.sum(s, axis=-1, keepdims=True) + w_inter * jnp.sum(q * n_prev, axis=-1, keepdims=True)
        hh = num / jnp.maximum(jnp.abs(den), jnp.exp(-m_t))
        hh = hh * lax.rsqrt(jnp.mean(hh * hh, axis=-1, keepdims=True) + EPS)
        hh = hh * onorm_ref[:, h * ML_DV:(h + 1) * ML_DV] * _sigmoid(og)
        out_ref[:, h * ML_DV:(h + 1) * ML_DV] = hh.astype(out_ref.dtype)

        a_col = g - bcol + icol
        m_new = jnp.maximum(g + m_prev, jnp.max(a_col, axis=0, keepdims=True))
        dec = jnp.exp(g + m_prev - m_new)
        wk = jnp.exp(a_col - m_new) * k
        c_ref[h] = dec * c_prev + _dot_tn(_b(wk), _b(v))
        n_ref[h:h + 1, :] = dec * n_prev + jnp.sum(wk, axis=0, keepdims=True)
        m_ref[h:h + 1, :] = jnp.broadcast_to(m_new, (1, m_ref.shape[1]))


def mlstm_group(p, i_bias, f_bias, out_norm, batch, seq_t):
    m = p.shape[0]
    nc = seq_t // CHUNK
    bias = jnp.zeros((1, E_TAIL - E_TAIL_GATES), F32)
    bias = bias.at[0, 0:ML_HEADS].set(i_bias).at[0, ML_HEADS:2 * ML_HEADS].set(f_bias)
    row = lambda b, c: b * nc + c
    return pl.pallas_call(
        _mlstm_kernel,
        grid=(batch, nc),
        in_specs=[
            pl.BlockSpec((CHUNK, ML_QK), lambda b, c: (row(b, c), 0)),
            pl.BlockSpec((CHUNK, ML_QK), lambda b, c: (row(b, c), 1)),
            pl.BlockSpec((CHUNK, ML_V), lambda b, c: (row(b, c), 1)),
            pl.BlockSpec((CHUNK, ML_V), lambda b, c: (row(b, c), 2)),
            pl.BlockSpec((CHUNK, E_TAIL), lambda b, c: (row(b, c), E_OFF_TAIL // E_TAIL)),
            pl.BlockSpec((1, E_TAIL - E_TAIL_GATES), lambda b, c: (0, 0)),
            pl.BlockSpec((1, ML_V), lambda b, c: (0, 0)),
        ],
        out_specs=pl.BlockSpec((CHUNK, ML_V), lambda b, c: (row(b, c), 0)),
        out_shape=jax.ShapeDtypeStruct((m, ML_V), BF16),
        scratch_shapes=[pltpu.VMEM((ML_HEADS, ML_DQK, ML_DV), F32), pltpu.VMEM((8, ML_DQK), F32),
                        pltpu.VMEM((8, 128), F32)],
        compiler_params=_params(("arbitrary", "arbitrary")),
        name="mlstm",
    )(p, p, p, p, p, bias, out_norm.reshape(1, ML_V))


def _rwkv_kernel(r_ref, k_ref, v_ref, tail_ref, mu3_ref, mut_ref, chan_ref, wl_ref, out_ref,
                 h_ref, cr_ref, ck_ref, cv_ref, ct_ref):
    c = pl.program_id(2)
    L = CHUNK
    G = RW_GROUP

    @pl.when(c == 0)
    def _():
        h_ref[...] = jnp.zeros_like(h_ref)
        cr_ref[...] = jnp.zeros_like(cr_ref)
        ck_ref[...] = jnp.zeros_like(ck_ref)
        cv_ref[...] = jnp.zeros_like(cv_ref)
        ct_ref[...] = jnp.zeros_like(ct_ref)

    row = _iota((L, 1), 0)

    def shift_lerp(x, carry_ref, mu):
        prev = carry_ref[...]
        xs = jnp.where(row == 0, prev, pltpu.roll(x, 1, 0))
        carry_ref[...] = x[L - 1:L, :]
        return x + mu * (xs - x)

    r = shift_lerp(r_ref[...], cr_ref, mu3_ref[0:1, :])
    k = shift_lerp(k_ref[...], ck_ref, mu3_ref[1:2, :])
    v = shift_lerp(v_ref[...], cv_ref, mu3_ref[2:3, :])
    tail = shift_lerp(tail_ref[...], ct_ref, mut_ref[...])

    lora_in = tail[:, 0:LORA_ROWS]
    lane = _iota((L, LORA_ROWS), 1)
    act = jnp.where(lane < RW_DECAY_LORA, jnp.tanh(lora_in),
                    jnp.where(lane < RW_DECAY_LORA + RW_A_LORA, lora_in,
                              jnp.where(lane < RW_DECAY_LORA + RW_A_LORA + RW_GATE_LORA, _sigmoid(lora_in), 0.0)))
    lw = _dot(act, wl_ref[0], HI)
    la = _dot(act, wl_ref[1], HI)
    gate = _dot(act, wl_ref[2], HI)

    w0 = chan_ref[0:1, :]
    a0 = chan_ref[1:2, :]
    k_k = chan_ref[2:3, :]
    k_a = chan_ref[3:4, :]
    r_k = chan_ref[4:5, :]
    ln_w = chan_ref[5:6, :]
    ln_b = chan_ref[6:7, :]

    ri = _iota((G, G), 0)
    ci = _iota((G, G), 1)
    same = (ri >> 6) == (ci >> 6)
    tl = ri & (RW_N - 1)
    sl = ci & (RW_N - 1)
    bd_mask = same.astype(F32)
    eye = (ri == ci).astype(F32)
    lower = same & (tl >= sl)
    strict = same & (tl > sl)
    ones_bd = bd_mask

    valid = (c * L + row) >= N_PAD
    w_log = -_softplus(-(w0 + lw)) - 0.5
    ld = jnp.where(valid, -jnp.exp(w_log), 0.0)
    a = _sigmoid(a0 + la)
    kk = k * k_k
    kk = kk * lax.rsqrt(jnp.maximum(_dot(kk * kk, ones_bd, HI), 1e-24))
    k2 = k * (1.0 + (a - 1.0) * k_a)
    k2 = jnp.where(valid, k2, 0.0)
    kk = jnp.where(valid, kk, 0.0)
    va = -kk
    vb = kk * a

    cum = _dot(_tri(L), ld, HI)
    cum_l = cum[L - 1:L, :]
    w_t = jnp.exp(cum)
    w_inv = jnp.exp(-cum)
    w_prev = jnp.exp(cum - ld)
    w_end = jnp.exp(cum_l - cum)
    w_last = jnp.exp(cum_l)

    def bd(x):
        return jnp.concatenate([x] * RW_HPG, axis=0) * bd_mask

    a_bd = bd(va * w_prev)
    r_bd = bd(r * w_t)
    b_bd = bd(vb * w_inv)
    k_bd = bd(k2 * w_inv)
    bl_bd = bd(vb * w_end)
    kl_bd = bd(k2 * w_end)
    v_bd = bd(v)

    n_ab = jnp.where(strict, _dot_nt(a_bd, b_bd, HI), 0.0)
    a_ak = jnp.where(strict, _dot_nt(_b(a_bd), _b(k_bd)), 0.0)
    m_rb = jnp.where(lower, _dot_nt(_b(r_bd), _b(b_bd)), 0.0)
    m_rk = jnp.where(lower, _dot_nt(_b(r_bd), _b(k_bd)), 0.0)

    t_inv = eye + n_ab
    pw = n_ab
    for _ in range(5):
        pw = _dot(pw, pw, HI)
        t_inv = t_inv + _dot(t_inv, pw, HI)

    av = _dot(_b(a_ak), _b(v_bd))
    a_p = _dot(t_inv, a_bd, HI)
    u0 = _dot(t_inv, av, HI)
    p_mat = eye * w_last + _dot_tn(_b(bl_bd), _b(a_p))
    q_mat = _dot_tn(_b(bl_bd), _b(u0)) + _dot_tn(_b(kl_bd), _b(v_bd))
    r_p = r_bd + _dot(_b(m_rb), _b(a_p))
    y0 = _dot(_b(m_rb), _b(u0)) + _dot(_b(m_rk), _b(v_bd))

    h0 = h_ref[...]
    y_bd = _dot(r_p, h0, HI) + y0
    h_ref[...] = _dot(p_mat, h0, HI) + q_mat
    y = y_bd[0:L] + y_bd[L:2 * L] + y_bd[2 * L:3 * L] + y_bd[3 * L:4 * L]

    inv_n = 1.0 / RW_N
    mean = _dot(y, ones_bd, HI) * inv_n
    d = y - mean
    var = _dot(d * d, ones_bd, HI) * inv_n
    yn = d * lax.rsqrt(var + RW_LN_EPS) * ln_w + ln_b
    bonus = _dot(r * k2 * r_k, ones_bd, HI) * v
    out_ref[...] = ((yn + bonus) * gate).astype(out_ref.dtype)


def rwkv7_group(p, mu, w0, w_up, a0, a_up, g_up, k_k, k_a, r_k, ln_w, ln_b, batch, seq_t):
    m = p.shape[0]
    nc = seq_t // CHUNK
    ng = RW_DIM // RW_GROUP
    mu3 = mu[:3 * RW_DIM].reshape(3, RW_DIM)
    n_lora = RW_DECAY_LORA + RW_A_LORA + RW_GATE_LORA
    mut = jnp.concatenate([mu[3 * RW_DIM:], jnp.zeros((E_TAIL - n_lora,), F32)]).reshape(1, E_TAIL)
    chan = jnp.stack([w0, a0, k_k, k_a, r_k.reshape(RW_DIM), ln_w, ln_b, jnp.zeros((RW_DIM,), F32)])
    wl = jnp.zeros((3, LORA_ROWS, RW_DIM), F32)
    wl = wl.at[0, 0:RW_DECAY_LORA].set(w_up)
    wl = wl.at[1, RW_DECAY_LORA:RW_DECAY_LORA + RW_A_LORA].set(a_up)
    wl = wl.at[2, RW_DECAY_LORA + RW_A_LORA:n_lora].set(g_up)
    row = lambda b, g, c: b * nc + c
    off = E_OFF_R // RW_GROUP
    per = RW_DIM // RW_GROUP
    return pl.pallas_call(
        _rwkv_kernel,
        grid=(batch, ng, nc),
        in_specs=[
            pl.BlockSpec((CHUNK, RW_GROUP), lambda b, g, c: (row(b, g, c), off + g)),
            pl.BlockSpec((CHUNK, RW_GROUP), lambda b, g, c: (row(b, g, c), off + per + g)),
            pl.BlockSpec((CHUNK, RW_GROUP), lambda b, g, c: (row(b, g, c), off + 2 * per + g)),
            pl.BlockSpec((CHUNK, E_TAIL), lambda b, g, c: (row(b, g, c), E_OFF_TAIL // E_TAIL)),
            pl.BlockSpec((3, RW_GROUP), lambda b, g, c: (0, g)),
            pl.BlockSpec((1, E_TAIL), lambda b, g, c: (0, 0)),
            pl.BlockSpec((8, RW_GROUP), lambda b, g, c: (0, g)),
            pl.BlockSpec((3, LORA_ROWS, RW_GROUP), lambda b, g, c: (0, 0, g)),
        ],
        out_specs=pl.BlockSpec((CHUNK, RW_GROUP), lambda b, g, c: (row(b, g, c), g)),
        out_shape=jax.ShapeDtypeStruct((m, RW_DIM), BF16),
        scratch_shapes=[pltpu.VMEM((RW_GROUP, RW_GROUP), F32), pltpu.VMEM((1, RW_GROUP), F32),
                        pltpu.VMEM((1, RW_GROUP), F32), pltpu.VMEM((1, RW_GROUP), F32),
                        pltpu.VMEM((1, E_TAIL), F32)],
        compiler_params=_params(("arbitrary", "arbitrary", "arbitrary")),
        name="rwkv7",
    )(p, p, p, p, mu3, mut, chan, wl)


def _ssd_kernel(z_ref, xs_ref, bc_ref, dt_ref, cwx_ref, cwb_ref, cbx_ref, cbb_ref, hp_ref, dexp_ref, nw_ref,
                out_ref, s_ref, px_ref, pb_ref):
    c = pl.program_id(1)
    L = CHUNK

    @pl.when(c == 0)
    def _():
        s_ref[...] = jnp.zeros_like(s_ref)
        px_ref[...] = jnp.zeros_like(px_ref)
        pb_ref[...] = jnp.zeros_like(pb_ref)

    row8 = _iota((8, 1), 0)

    def conv_silu(x, prev_ref, w_ref, b_ref):
        prev = prev_ref[...]
        acc = x * w_ref[SSD_K - 1:SSD_K, :] + b_ref[...]
        for j in range(1, SSD_K):
            xr = pltpu.roll(x, j, 0)
            pr = pltpu.roll(prev, j, 0)
            top = jnp.where(row8 < j, pr, xr[0:8])
            sh = jnp.concatenate([top, xr[8:]], axis=0)
            acc = acc + sh * w_ref[SSD_K - 1 - j:SSD_K - j, :]
        prev_ref[...] = x[L - 8:L, :]
        return _silu(acc)

    xs = conv_silu(xs_ref[...], px_ref, cwx_ref, cbx_ref)
    bc = conv_silu(bc_ref[...], pb_ref, cwb_ref, cbb_ref)

    valid = (c * L + _iota((L, 1), 0)) >= N_PAD
    dt = jnp.where(valid, _softplus(dt_ref[...] + hp_ref[0:1, :]), 0.0)
    a = dt * (-jnp.exp(hp_ref[1:2, :]))
    acum = _dot(_tri(L), a, HI)
    a_t = acum.T
    a_last = acum[L - 1:L, :]
    dend = jnp.exp(a_last - acum)
    cdec = jnp.exp(a_last)
    eac = jnp.exp(acum)

    lane = _iota((L, 2 * SSD_P), 1)

    def expand(x):
        parts = []
        for j in range(SSD_H // 2):
            lo = jnp.broadcast_to(x[:, 2 * j:2 * j + 1], (L, 2 * SSD_P))
            hi = jnp.broadcast_to(x[:, 2 * j + 1:2 * j + 2], (L, 2 * SSD_P))
            parts.append(jnp.where(lane < SSD_P, lo, hi))
        return jnp.concatenate(parts, axis=1)

    xdt = xs * expand(dt)
    xw = xdt * expand(dend)
    e_ea = expand(eac)
    causal = _iota((L, L), 0) >= _iota((L, L), 1)

    ys = []
    for g in range(SSD_G):
        bg = bc[:, g * SSD_N:(g + 1) * SSD_N]
        cg = bc[:, SSD_G * SSD_N + g * SSD_N:SSD_G * SSD_N + (g + 1) * SSD_N]
        cb = _dot_nt(_b(cg), _b(bg))
        s_prev = s_ref[g]
        y_inter = _dot_nt(_b(cg), _b(s_prev)) * e_ea[:, g * SSD_GW:(g + 1) * SSD_GW]
        parts = []
        cd_rows = []
        for r in range(SSD_R):
            h = g * SSD_R + r
            seg = acum[:, h:h + 1] - a_t[h:h + 1, :]
            ldec = jnp.exp(jnp.where(causal, seg, NEG))
            parts.append(_dot(_b(cb * ldec), _b(xdt[:, h * SSD_P:(h + 1) * SSD_P])))
            cd_rows.append(jnp.broadcast_to(cdec[:, h:h + 1], (SSD_P, SSD_N)))
        ys.append(jnp.concatenate(parts, axis=1) + y_inter)
        st = _dot_tn(_b(xw[:, g * SSD_GW:(g + 1) * SSD_GW]), _b(bg))
        s_ref[g] = jnp.concatenate(cd_rows, axis=0) * s_prev + st

    y = jnp.concatenate(ys, axis=1) + dexp_ref[...] * xs
    y = y * _silu(z_ref[...])
    outs = []
    for g in range(SSD_G):
        yg = y[:, g * SSD_GW:(g + 1) * SSD_GW]
        outs.append(yg * lax.rsqrt(jnp.mean(yg * yg, axis=-1, keepdims=True) + EPS))
    out_ref[...] = (jnp.concatenate(outs, axis=1) * nw_ref[...]).astype(out_ref.dtype)


def ssd_mixer(p, conv_w, conv_b, dt_bias, a_log, d_skip, norm_w, batch, seq_t):
    m = p.shape[0]
    nc = seq_t // CHUNK
    hp = jnp.zeros((8, SSD_DT_PAD), F32).at[0, :SSD_H].set(dt_bias).at[1, :SSD_H].set(a_log)
    dexp = jnp.repeat(d_skip, SSD_P).reshape(1, SSD_DI)
    conv_b = conv_b.reshape(1, -1)
    row = lambda b, c: b * nc + c
    return pl.pallas_call(
        _ssd_kernel,
        grid=(batch, nc),
        in_specs=[
            pl.BlockSpec((CHUNK, SSD_DI), lambda b, c: (row(b, c), 0)),
            pl.BlockSpec((CHUNK, SSD_DI), lambda b, c: (row(b, c), 1)),
            pl.BlockSpec((CHUNK, SSD_BC), lambda b, c: (row(b, c), 2 * SSD_DI // SSD_BC)),
            pl.BlockSpec((CHUNK, SSD_DT_PAD), lambda b, c: (row(b, c), (2 * SSD_DI + SSD_BC) // SSD_DT_PAD)),
            pl.BlockSpec((SSD_K, SSD_DI), lambda b, c: (0, 0)),
            pl.BlockSpec((SSD_K, SSD_BC), lambda b, c: (0, SSD_DI // SSD_BC)),
            pl.BlockSpec((1, SSD_DI), lambda b, c: (0, 0)),
            pl.BlockSpec((1, SSD_BC), lambda b, c: (0, SSD_DI // SSD_BC)),
            pl.BlockSpec((8, SSD_DT_PAD), lambda b, c: (0, 0)),
            pl.BlockSpec((1, SSD_DI), lambda b, c: (0, 0)),
            pl.BlockSpec((1, SSD_DI), lambda b, c: (0, 0)),
        ],
        out_specs=pl.BlockSpec((CHUNK, SSD_DI), lambda b, c: (row(b, c), 0)),
        out_shape=jax.ShapeDtypeStruct((m, SSD_DI), BF16),
        scratch_shapes=[pltpu.VMEM((SSD_G, SSD_GW, SSD_N), F32), pltpu.VMEM((8, SSD_DI), F32),
                        pltpu.VMEM((8, SSD_BC), F32)],
        compiler_params=_params(("arbitrary", "arbitrary")),
        name="ssd",
    )(p, p, p, p, conv_w, conv_w, conv_b, conv_b, hp, dexp, norm_w.reshape(1, SSD_DI))


def _router_kernel(x_ref, w_ref, r_ref, hn_ref, g_ref):
    x = x_ref[...]
    hn = x * lax.rsqrt(jnp.mean(x * x, axis=-1, keepdims=True) + EPS) * w_ref[...]
    hn_ref[...] = hn.astype(hn_ref.dtype)
    logits = _dot(hn, r_ref[...], HI)
    lane = _iota(logits.shape, 1)
    lane_f = lane.astype(F32)
    logits = jnp.where(lane < N_EXPERTS, logits, -jnp.inf)
    m1 = jnp.max(logits, axis=-1, keepdims=True)
    i1 = jnp.min(jnp.where(logits == m1, lane_f, float(ROUTER_PAD)), axis=-1, keepdims=True)
    sel1 = lane_f == i1
    rest = jnp.where(sel1, -jnp.inf, logits)
    m2 = jnp.max(rest, axis=-1, keepdims=True)
    i2 = jnp.min(jnp.where(rest == m2, lane_f, float(ROUTER_PAD)), axis=-1, keepdims=True)
    sel2 = lane_f == i2
    e2 = jnp.exp(m2 - m1)
    w1 = 1.0 / (1.0 + e2)
    w2 = e2 / (1.0 + e2)
    g_ref[...] = jnp.where(sel1, w1, 0.0) + jnp.where(sel2, w2, 0.0)


def norm_router(x, w, router):
    m, d = x.shape
    tm = _tile(m, (520, 512, 256, 128, 64))
    rpad = jnp.zeros((d, ROUTER_PAD), F32).at[:, :N_EXPERTS].set(router)
    return pl.pallas_call(
        _router_kernel,
        grid=(m // tm,),
        in_specs=[pl.BlockSpec((tm, d), lambda i: (i, 0)), pl.BlockSpec((1, d), lambda i: (0, 0)),
                  pl.BlockSpec((d, ROUTER_PAD), lambda i: (0, 0))],
        out_specs=[pl.BlockSpec((tm, d), lambda i: (i, 0)), pl.BlockSpec((tm, ROUTER_PAD), lambda i: (i, 0))],
        out_shape=[jax.ShapeDtypeStruct((m, d), BF16), jax.ShapeDtypeStruct((m, ROUTER_PAD), F32)],
        compiler_params=_params(("arbitrary",)),
        name="norm_router",
    )(x, w.reshape(1, d), rpad)


def kernel(x, meta_tokens, e_norm_mix, e_w_in, ml_i_bias, ml_f_bias, ml_out_norm, rw_mu, rw_w0, rw_w_up, rw_a0, rw_a_up, rw_g_up, rw_k_k, rw_k_a, rw_r_k, rw_ln_w, rw_ln_b, e_w_out, e_norm_ffn, ffn_w_gate, ffn_w_up, ffn_w_down, o_norm_mix, ssd_w_in, ssd_conv_w, ssd_conv_b, ssd_dt_bias, ssd_A_log, ssd_D, ssd_norm, ssd_w_out, o_norm_ffn, moe_router, moe_w_gate, moe_w_up, moe_w_down, final_norm):
    batch, seq, d = x.shape
    seq_t = N_PAD + N_META + seq
    m = batch * seq_t
    h = jnp.concatenate([jnp.zeros((batch, N_PAD, d), x.dtype),
                         jnp.broadcast_to(meta_tokens.astype(x.dtype), (batch, N_META, d)), x], axis=1)
    h = h.reshape(m, d)

    w = e_w_in[0]
    ml_cols = 2 * ML_QK + 2 * ML_V
    rw0 = ml_cols + 2 * ML_HEADS
    n_lora = RW_DECAY_LORA + RW_A_LORA + RW_GATE_LORA
    w_pad = jnp.concatenate([
        w[:, :ml_cols], w[:, rw0:rw0 + 3 * RW_DIM], w[:, rw0 + 3 * RW_DIM:rw0 + 3 * RW_DIM + n_lora],
        jnp.zeros((d, E_TAIL_GATES - n_lora), F32), w[:, ml_cols:rw0],
        jnp.zeros((d, E_TAIL - E_TAIL_GATES - 2 * ML_HEADS), F32)], axis=1)
    hn = rmsnorm(h, e_norm_mix[0], BF16)
    p = mm_up(hn, w_pad, F32)
    y_ml = mlstm_group(p, ml_i_bias[0], ml_f_bias[0], ml_out_norm[0], batch, seq_t)
    y_rw = rwkv7_group(p, rw_mu[0], rw_w0[0], rw_w_up[0], rw_a0[0], rw_a_up[0], rw_g_up[0], rw_k_k[0],
                       rw_k_a[0], rw_r_k[0], rw_ln_w[0], rw_ln_b[0], batch, seq_t)
    y = jnp.concatenate([y_ml, y_rw], axis=1)
    h = mm_down(y, _b(e_w_out[0]), h, seq_t)
    hn = rmsnorm(h, e_norm_ffn[0], BF16)
    f = swiglu_up(hn, ffn_w_gate[0], ffn_w_up[0])
    h = mm_down(f, _b(ffn_w_down[0]), h, seq_t)

    w = ssd_w_in[0]
    n_real = 2 * SSD_DI + SSD_BC + SSD_H
    w_pad = jnp.concatenate([w, jnp.zeros((d, SSD_COLS - n_real), F32)], axis=1)
    hn = rmsnorm(h, o_norm_mix[0], BF16)
    p = mm_up(hn, w_pad, F32, tn_prefs=(1152, 384, 128))
    y = ssd_mixer(p, ssd_conv_w[0], ssd_conv_b[0], ssd_dt_bias[0], ssd_A_log[0], ssd_D[0], ssd_norm[0],
                  batch, seq_t)
    h = mm_down(y, _b(ssd_w_out[0]), h, seq_t)
    hn, gates = norm_router(h, o_norm_ffn[0], moe_router[0])
    wd = _b(moe_w_down[0])
    for e in range(N_EXPERTS):
        f = swiglu_up(hn, moe_w_gate[0], moe_w_up[0], expert=e, gates=gates)
        h = mm_down(f, wd, h, seq_t, expert=e)

    out = rmsnorm(h.reshape(batch, seq_t, d)[:, N_PAD + N_META:].reshape(batch * seq, d), final_norm, x.dtype)
    return out.reshape(batch, seq, d)
```

```python
import functools

import jax
import jax.numpy as jnp
from jax import lax
from jax.experimental import pallas as pl
from jax.experimental.pallas import tpu as pltpu

F32 = jnp.float32
BF16 = jnp.bfloat16
HI = lax.Precision.HIGHEST

N_META = 16
CHUNK = 64
N_PAD = CHUNK - N_META
NEG = -1e30
EPS = 1e-6

ML_HEADS = 4
ML_DQK = 128
ML_DV = 256
ML_GATE_CAP = 15.0
ML_QK = ML_HEADS * ML_DQK
ML_V = ML_HEADS * ML_DV

RW_HEADS = 16
RW_N = 64
RW_DIM = RW_HEADS * RW_N
RW_DECAY_LORA = 64
RW_A_LORA = 64
RW_GATE_LORA = 160
RW_LN_EPS = 64e-5
RW_GROUP = 256
RW_HPG = RW_GROUP // RW_N
LORA_ROWS = 384

E_TAIL = 512
E_TAIL_GATES = 384
E_COLS = 2 * ML_QK + 2 * ML_V + 3 * RW_DIM + E_TAIL
E_OFF_R = 2 * ML_QK + 2 * ML_V
E_OFF_TAIL = E_OFF_R + 3 * RW_DIM

SSD_DI = 4096
SSD_P = 64
SSD_H = SSD_DI // SSD_P
SSD_N = 128
SSD_G = 8
SSD_R = SSD_H // SSD_G
SSD_K = 4
SSD_GW = SSD_DI // SSD_G
SSD_BC = 2 * SSD_G * SSD_N
SSD_DT_PAD = 128
SSD_COLS = 2 * SSD_DI + SSD_BC + SSD_DT_PAD

N_EXPERTS = 8
ROUTER_PAD = 128

VMEM_LIMIT = 56 * 1024 * 1024


def _tile(n, prefs):
    for p in prefs:
        if n % p == 0:
            return p
    return n


_ROW_TILES = (1040, 832, 640, 520, 512, 256, 128, 64)


def _params(sem):
    return pltpu.CompilerParams(dimension_semantics=sem, vmem_limit_bytes=VMEM_LIMIT)


def _dot(a, b, prec=None):
    return lax.dot_general(a, b, (((1,), (0,)), ((), ())), precision=prec, preferred_element_type=F32)


def _dot_nt(a, b, prec=None):
    return lax.dot_general(a, b, (((1,), (1,)), ((), ())), precision=prec, preferred_element_type=F32)


def _dot_tn(a, b, prec=None):
    return lax.dot_general(a, b, (((0,), (0,)), ((), ())), precision=prec, preferred_element_type=F32)


def _b(x):
    return x.astype(BF16)


def _bdot(dot, a, b):
    return dot(_b(a), _b(b))


def _dot_split(a, b):
    hi = _b(a)
    lo = _b(a - hi.astype(F32))
    bb = _b(b)
    return _dot(hi, bb) + _dot(lo, bb)


def _sigmoid(x):
    return 1.0 / (1.0 + jnp.exp(-x))


def _softplus(x):
    return jnp.maximum(x, 0.0) + jnp.log1p(jnp.exp(-jnp.abs(x)))


def _silu(x):
    return x * _sigmoid(x)


def _iota(shape, dim):
    return lax.broadcasted_iota(jnp.int32, shape, dim)


def _tri(n):
    return (_iota((n, n), 0) >= _iota((n, n), 1)).astype(F32)


def _rmsnorm_kernel(x_ref, w_ref, o_ref):
    x = x_ref[...]
    ms = jnp.mean(x * x, axis=-1, keepdims=True)
    o_ref[...] = (x * lax.rsqrt(ms + EPS) * w_ref[...]).astype(o_ref.dtype)


def rmsnorm(x, w, out_dtype):
    m, d = x.shape
    tm = _tile(m, _ROW_TILES)
    return pl.pallas_call(
        _rmsnorm_kernel,
        grid=(m // tm,),
        in_specs=[pl.BlockSpec((tm, d), lambda i: (i, 0)), pl.BlockSpec((1, d), lambda i: (0, 0))],
        out_specs=pl.BlockSpec((tm, d), lambda i: (i, 0)),
        out_shape=jax.ShapeDtypeStruct((m, d), out_dtype),
        compiler_params=_params(("arbitrary",)),
        name="rmsnorm",
    )(x, w.reshape(1, d))


def _mm_up_kernel(a_ref, w_ref, o_ref, wb_ref):
    @pl.when(pl.program_id(1) == 0)
    def _():
        wb_ref[...] = _b(w_ref[...])

    o_ref[...] = _dot(a_ref[...], wb_ref[...]).astype(o_ref.dtype)


def mm_up(a, w, out_dtype, tn_prefs=(512, 384, 256, 128)):
    m, k = a.shape
    n = w.shape[1]
    tm = _tile(m, _ROW_TILES)
    tn = _tile(n, tn_prefs)
    return pl.pallas_call(
        _mm_up_kernel,
        grid=(n // tn, m // tm),
        in_specs=[pl.BlockSpec((tm, k), lambda j, i: (i, 0)), pl.BlockSpec((k, tn), lambda j, i: (0, j))],
        out_specs=pl.BlockSpec((tm, tn), lambda j, i: (i, j)),
        out_shape=jax.ShapeDtypeStruct((m, n), out_dtype),
        scratch_shapes=[pltpu.VMEM((k, tn), BF16)],
        compiler_params=_params(("arbitrary", "arbitrary")),
        name="mm_up",
    )(a, w)


def _swiglu_kernel(a_ref, wg_ref, wu_ref, o_ref, wgb_ref, wub_ref):
    @pl.when(pl.program_id(1) == 0)
    def _():
        wgb_ref[...] = _b(wg_ref[...])
        wub_ref[...] = _b(wu_ref[...])

    a = a_ref[...]
    o_ref[...] = (_silu(_dot(a, wgb_ref[...])) * _dot(a, wub_ref[...])).astype(o_ref.dtype)


def swiglu_up(a, wg, wu):
    m, k = a.shape
    f = wg.shape[-1]
    tm = _tile(m, _ROW_TILES)
    tn = _tile(f, (512, 256, 128))
    w_spec = pl.BlockSpec((k, tn), lambda j, i: (0, j))
    return pl.pallas_call(
        _swiglu_kernel,
        grid=(f // tn, m // tm),
        in_specs=[pl.BlockSpec((tm, k), lambda j, i: (i, 0)), w_spec, w_spec],
        out_specs=pl.BlockSpec((tm, tn), lambda j, i: (i, j)),
        out_shape=jax.ShapeDtypeStruct((m, f), BF16),
        scratch_shapes=[pltpu.VMEM((k, tn), BF16), pltpu.VMEM((k, tn), BF16)],
        compiler_params=_params(("arbitrary", "arbitrary")),
        name="swiglu_up",
    )(a, wg, wu)


def _mm_down_kernel(a_ref, w_ref, r_ref, o_ref, *, tm, seq_t):
    acc = _dot(a_ref[...], w_ref[...])
    t0 = (pl.program_id(0) * tm) % seq_t
    valid = ((t0 + _iota((tm, 1), 0)) >= N_PAD).astype(F32)
    o_ref[...] = r_ref[...] + acc * valid


def mm_down(a, w, res, seq_t):
    m, k = a.shape
    n = w.shape[-1]
    tm = _tile(seq_t, (520, 512, 256, 128, 64))
    tn = _tile(n, (512, 256, 128))
    w_spec = pl.BlockSpec((k, tn), lambda i, j: (0, j))
    return pl.pallas_call(
        functools.partial(_mm_down_kernel, tm=tm, seq_t=seq_t),
        grid=(m // tm, n // tn),
        in_specs=[pl.BlockSpec((tm, k), lambda i, j: (i, 0)), w_spec,
                  pl.BlockSpec((tm, tn), lambda i, j: (i, j))],
        out_specs=pl.BlockSpec((tm, tn), lambda i, j: (i, j)),
        out_shape=jax.ShapeDtypeStruct((m, n), F32),
        compiler_params=_params(("arbitrary", "arbitrary")),
        name="mm_down",
    )(a, w, res)


def _log_sigmoid(x):
    return -_softplus(-x)


def _mlstm_kernel(q_ref, k_ref, v_ref, o_ref, tail_ref, bias_ref, onorm_ref, out_ref, c_ref, n_ref, m_ref):
    c = pl.program_id(1)
    L = CHUNK

    @pl.when(c == 0)
    def _():
        c_ref[...] = jnp.zeros_like(c_ref)
        n_ref[...] = jnp.zeros_like(n_ref)
        m_ref[...] = jnp.zeros_like(m_ref)

    gates = tail_ref[:, E_TAIL_GATES:E_TAIL] + bias_ref[...]
    sc = ML_GATE_CAP * jnp.tanh(gates / ML_GATE_CAP)
    valid = (c * L + _iota((L, 1), 0)) >= N_PAD
    ig = jnp.where(valid, sc, NEG)
    lf = jnp.where(valid, _log_sigmoid(sc), 0.0)
    bcum = _dot(_tri(L), lf, HI)
    b_t = bcum.T
    i_t = ig.T
    causal = _iota((L, L), 0) >= _iota((L, L), 1)
    scale = ML_DQK ** -0.5

    for h in range(ML_HEADS):
        q = q_ref[:, h * ML_DQK:(h + 1) * ML_DQK] * scale
        k = k_ref[:, h * ML_DQK:(h + 1) * ML_DQK]
        v = v_ref[:, h * ML_DV:(h + 1) * ML_DV]
        og = o_ref[:, h * ML_DV:(h + 1) * ML_DV]
        bcol = bcum[:, ML_HEADS + h:ML_HEADS + h + 1]
        icol = ig[:, h:h + 1]
        brow = b_t[ML_HEADS + h:ML_HEADS + h + 1, :]
        irow = i_t[h:h + 1, :]
        g = bcol[L - 1:L, :]
        m_prev = m_ref[h:h + 1, 0:1]
        c_prev = c_ref[h]
        n_prev = n_ref[h:h + 1, :]

        dm = jnp.where(causal, bcol - brow + irow, NEG)
        m_inter = bcol + m_prev
        m_t = jnp.maximum(m_inter, jnp.max(dm, axis=-1, keepdims=True))
        s = _dot_nt(_b(q), _b(k)) * jnp.exp(dm - m_t)
        w_inter = jnp.exp(m_inter - m_t)
        num = _dot(_b(s), _b(v)) + w_inter * _dot(_b(q), _b(c_prev))
        den = jnp.sum(s, axis=-1, keepdims=True) + w_inter * jnp.sum(q * n_prev, axis=-1, keepdims=True)
        hh = num / jnp.maximum(jnp.abs(den), jnp.exp(-m_t))
        hh = hh * lax.rsqrt(jnp.mean(hh * hh, axis=-1, keepdims=True) + EPS)
        hh = hh * onorm_ref[:, h * ML_DV:(h + 1) * ML_DV] * _sigmoid(og)
        out_ref[:, h * ML_DV:(h + 1) * ML_DV] = hh.astype(out_ref.dtype)

        a_col = g - bcol + icol
        m_new = jnp.maximum(g + m_prev, jnp.max(a_col, axis=0, keepdims=True))
        dec = jnp.exp(g + m_prev - m_new)
        wk = jnp.exp(a_col - m_new) * k
        c_ref[h] = dec * c_prev + _dot_tn(_b(wk), _b(v))
        n_ref[h:h + 1, :] = dec * n_prev + jnp.sum(wk, axis=0, keepdims=True)
        m_ref[h:h + 1, :] = jnp.broadcast_to(m_new, (1, m_ref.shape[1]))


def mlstm_group(p, i_bias, f_bias, out_norm, batch, seq_t):
    m = p.shape[0]
    nc = seq_t // CHUNK
    bias = jnp.zeros((1, E_TAIL - E_TAIL_GATES), F32)
    bias = bias.at[0, 0:ML_HEADS].set(i_bias).at[0, ML_HEADS:2 * ML_HEADS].set(f_bias)
    row = lambda b, c: b * nc + c
    return pl.pallas_call(
        _mlstm_kernel,
        grid=(batch, nc),
        in_specs=[
            pl.BlockSpec((CHUNK, ML_QK), lambda b, c: (row(b, c), 0)),
            pl.BlockSpec((CHUNK, ML_QK), lambda b, c: (row(b, c), 1)),
            pl.BlockSpec((CHUNK, ML_V), lambda b, c: (row(b, c), 1)),
            pl.BlockSpec((CHUNK, ML_V), lambda b, c: (row(b, c), 2)),
            pl.BlockSpec((CHUNK, E_TAIL), lambda b, c: (row(b, c), E_OFF_TAIL // E_TAIL)),
            pl.BlockSpec((1, E_TAIL - E_TAIL_GATES), lambda b, c: (0, 0)),
            pl.BlockSpec((1, ML_V), lambda b, c: (0, 0)),
        ],
        out_specs=pl.BlockSpec((CHUNK, ML_V), lambda b, c: (row(b, c), 0)),
        out_shape=jax.ShapeDtypeStruct((m, ML_V), BF16),
        scratch_shapes=[pltpu.VMEM((ML_HEADS, ML_DQK, ML_DV), F32), pltpu.VMEM((8, ML_DQK), F32),
                        pltpu.VMEM((8, 128), F32)],
        compiler_params=_params(("arbitrary", "arbitrary")),
        name="mlstm",
    )(p, p, p, p, p, bias, out_norm.reshape(1, ML_V))


def _rwkv_kernel(r_ref, k_ref, v_ref, tail_ref, mu3_ref, mut_ref, chan_ref, wl_ref, out_ref,
                 h_ref, cr_ref, ck_ref, cv_ref, ct_ref):
    c = pl.program_id(2)
    L = CHUNK
    G = RW_GROUP

    @pl.when(c == 0)
    def _():
        h_ref[...] = jnp.zeros_like(h_ref)
        cr_ref[...] = jnp.zeros_like(cr_ref)
        ck_ref[...] = jnp.zeros_like(ck_ref)
        cv_ref[...] = jnp.zeros_like(cv_ref)
        ct_ref[...] = jnp.zeros_like(ct_ref)

    row = _iota((L, 1), 0)

    def shift_lerp(x, carry_ref, mu):
        prev = carry_ref[...]
        xs = jnp.where(row == 0, prev, pltpu.roll(x, 1, 0))
        carry_ref[...] = x[L - 1:L, :]
        return x + mu * (xs - x)

    r = shift_lerp(r_ref[...], cr_ref, mu3_ref[0:1, :])
    k = shift_lerp(k_ref[...], ck_ref, mu3_ref[1:2, :])
    v = shift_lerp(v_ref[...], cv_ref, mu3_ref[2:3, :])
    tail = shift_lerp(tail_ref[...], ct_ref, mut_ref[...])

    lora_in = tail[:, 0:LORA_ROWS]
    lane = _iota((L, LORA_ROWS), 1)
    act = jnp.where(lane < RW_DECAY_LORA, jnp.tanh(lora_in),
                    jnp.where(lane < RW_DECAY_LORA + RW_A_LORA, lora_in,
                              jnp.where(lane < RW_DECAY_LORA + RW_A_LORA + RW_GATE_LORA, _sigmoid(lora_in), 0.0)))
    lw = _dot(act, wl_ref[0], HI)
    la = _bdot(_dot, act, wl_ref[1])
    gate = _bdot(_dot, act, wl_ref[2])

    w0 = chan_ref[0:1, :]
    a0 = chan_ref[1:2, :]
    k_k = chan_ref[2:3, :]
    k_a = chan_ref[3:4, :]
    r_k = chan_ref[4:5, :]
    ln_w = chan_ref[5:6, :]
    ln_b = chan_ref[6:7, :]

    ri = _iota((G, G), 0)
    ci = _iota((G, G), 1)
    same = (ri >> 6) == (ci >> 6)
    tl = ri & (RW_N - 1)
    sl = ci & (RW_N - 1)
    bd_mask = same.astype(F32)
    eye = (ri == ci).astype(F32)
    lower = same & (tl >= sl)
    strict = same & (tl > sl)
    ones_bd = bd_mask

    valid = (c * L + row) >= N_PAD
    w_log = -_softplus(-(w0 + lw)) - 0.5
    ld = jnp.where(valid, -jnp.exp(w_log), 0.0)
    a = _sigmoid(a0 + la)
    kk = k * k_k
    kk = kk * lax.rsqrt(jnp.maximum(_dot_split(kk * kk, ones_bd), 1e-24))
    k2 = k * (1.0 + (a - 1.0) * k_a)
    k2 = jnp.where(valid, k2, 0.0)
    kk = jnp.where(valid, kk, 0.0)
    va = -kk
    vb = kk * a

    cum = _dot(_tri(L), ld, HI)
    cum_l = cum[L - 1:L, :]
    w_t = jnp.exp(cum)
    w_inv = jnp.exp(-cum)
    w_prev = jnp.exp(cum - ld)
    w_end = jnp.exp(cum_l - cum)
    w_last = jnp.exp(cum_l)

    def bd(x):
        return _b(jnp.concatenate([x] * RW_HPG, axis=0) * bd_mask)

    a_bd = bd(va * w_prev)
    r_bd = bd(r * w_t)
    b_bd = bd(vb * w_inv)
    k_bd = bd(k2 * w_inv)
    bl_bd = bd(vb * w_end)
    kl_bd = bd(k2 * w_end)
    v_bd = bd(v)

    n_ab = jnp.where(strict, _dot_nt(a_bd, b_bd), 0.0)
    a_ak = _b(jnp.where(strict, _dot_nt(a_bd, k_bd), 0.0))
    m_rb = _b(jnp.where(lower, _dot_nt(r_bd, b_bd), 0.0))
    m_rk = _b(jnp.where(lower, _dot_nt(r_bd, k_bd), 0.0))

    t_inv = eye + n_ab
    pw = n_ab
    for _ in range(5):
        pw_b = _b(pw)
        pw = _dot(pw_b, pw_b)
        t_inv = t_inv + _bdot(_dot, t_inv, pw)

    t_b = _b(t_inv)
    av = _dot(a_ak, v_bd)
    a_p = _b(_dot(t_b, a_bd))
    u0 = _b(_dot(t_b, _b(av)))
    p_mat = eye * w_last + _dot_tn(bl_bd, a_p)
    q_mat = _dot_tn(bl_bd, u0) + _dot_tn(kl_bd, v_bd)
    r_p = r_bd.astype(F32) + _dot(m_rb, a_p)
    y0 = _dot(m_rb, u0) + _dot(m_rk, v_bd)

    h0 = _b(h_ref[...])
    y_bd = _dot(_b(r_p), h0) + y0
    h_ref[...] = _dot(_b(p_mat), h0) + q_mat
    y = y_bd[0:L] + y_bd[L:2 * L] + y_bd[2 * L:3 * L] + y_bd[3 * L:4 * L]

    inv_n = 1.0 / RW_N
    mean = _dot_split(y, ones_bd) * inv_n
    d = y - mean
    var = _dot_split(d * d, ones_bd) * inv_n
    yn = d * lax.rsqrt(var + RW_LN_EPS) * ln_w + ln_b
    bonus = _dot_split(r * k2 * r_k, ones_bd) * v
    out_ref[...] = ((yn + bonus) * gate).astype(out_ref.dtype)


def rwkv7_group(p, mu, w0, w_up, a0, a_up, g_up, k_k, k_a, r_k, ln_w, ln_b, batch, seq_t):
    m = p.shape[0]
    nc = seq_t // CHUNK
    ng = RW_DIM // RW_GROUP
    mu3 = mu[:3 * RW_DIM].reshape(3, RW_DIM)
    n_lora = RW_DECAY_LORA + RW_A_LORA + RW_GATE_LORA
    mut = jnp.concatenate([mu[3 * RW_DIM:], jnp.zeros((E_TAIL - n_lora,), F32)]).reshape(1, E_TAIL)
    chan = jnp.stack([w0, a0, k_k, k_a, r_k.reshape(RW_DIM), ln_w, ln_b, jnp.zeros((RW_DIM,), F32)])
    wl = jnp.zeros((3, LORA_ROWS, RW_DIM), F32)
    wl = wl.at[0, 0:RW_DECAY_LORA].set(w_up)
    wl = wl.at[1, RW_DECAY_LORA:RW_DECAY_LORA + RW_A_LORA].set(a_up)
    wl = wl.at[2, RW_DECAY_LORA + RW_A_LORA:n_lora].set(g_up)
    row = lambda b, g, c: b * nc + c
    off = E_OFF_R // RW_GROUP
    per = RW_DIM // RW_GROUP
    return pl.pallas_call(
        _rwkv_kernel,
        grid=(batch, ng, nc),
        in_specs=[
            pl.BlockSpec((CHUNK, RW_GROUP), lambda b, g, c: (row(b, g, c), off + g)),
            pl.BlockSpec((CHUNK, RW_GROUP), lambda b, g, c: (row(b, g, c), off + per + g)),
            pl.BlockSpec((CHUNK, RW_GROUP), lambda b, g, c: (row(b, g, c), off + 2 * per + g)),
            pl.BlockSpec((CHUNK, E_TAIL), lambda b, g, c: (row(b, g, c), E_OFF_TAIL // E_TAIL)),
            pl.BlockSpec((3, RW_GROUP), lambda b, g, c: (0, g)),
            pl.BlockSpec((1, E_TAIL), lambda b, g, c: (0, 0)),
            pl.BlockSpec((8, RW_GROUP), lambda b, g, c: (0, g)),
            pl.BlockSpec((3, LORA_ROWS, RW_GROUP), lambda b, g, c: (0, 0, g)),
        ],
        out_specs=pl.BlockSpec((CHUNK, RW_GROUP), lambda b, g, c: (row(b, g, c), g)),
        out_shape=jax.ShapeDtypeStruct((m, RW_DIM), BF16),
        scratch_shapes=[pltpu.VMEM((RW_GROUP, RW_GROUP), F32), pltpu.VMEM((1, RW_GROUP), F32),
                        pltpu.VMEM((1, RW_GROUP), F32), pltpu.VMEM((1, RW_GROUP), F32),
                        pltpu.VMEM((1, E_TAIL), F32)],
        compiler_params=_params(("arbitrary", "arbitrary", "arbitrary")),
        name="rwkv7",
    )(p, p, p, p, mu3, mut, chan, wl)


def _ssd_kernel(z_ref, xs_ref, bc_ref, dt_ref, cwx_ref, cwb_ref, cbx_ref, cbb_ref, hp_ref, dexp_ref, nw_ref,
                out_ref, s_ref, px_ref, pb_ref):
    c = pl.program_id(1)
    L = CHUNK

    @pl.when(c == 0)
    def _():
        s_ref[...] = jnp.zeros_like(s_ref)
        px_ref[...] = jnp.zeros_like(px_ref)
        pb_ref[...] = jnp.zeros_like(pb_ref)

    row8 = _iota((8, 1), 0)

    def conv_silu(x, prev_ref, w_ref, b_ref):
        prev = prev_ref[...]
        acc = x * w_ref[SSD_K - 1:SSD_K, :] + b_ref[...]
        for j in range(1, SSD_K):
            xr = pltpu.roll(x, j, 0)
            pr = pltpu.roll(prev, j, 0)
            top = jnp.where(row8 < j, pr, xr[0:8])
            sh = jnp.concatenate([top, xr[8:]], axis=0)
            acc = acc + sh * w_ref[SSD_K - 1 - j:SSD_K - j, :]
        prev_ref[...] = x[L - 8:L, :]
        return _silu(acc)

    xs = conv_silu(xs_ref[...], px_ref, cwx_ref, cbx_ref)
    bc = conv_silu(bc_ref[...], pb_ref, cwb_ref, cbb_ref)

    valid = (c * L + _iota((L, 1), 0)) >= N_PAD
    dt = jnp.where(valid, _softplus(dt_ref[...] + hp_ref[0:1, :]), 0.0)
    a = dt * (-jnp.exp(hp_ref[1:2, :]))
    acum = _dot(_tri(L), a, HI)
    a_t = acum.T
    a_last = acum[L - 1:L, :]
    dend = jnp.exp(a_last - acum)
    cdec = jnp.exp(a_last)
    eac = jnp.exp(acum)

    lane = _iota((L, 2 * SSD_P), 1)

    def expand(x):
        parts = []
        for j in range(SSD_H // 2):
            lo = jnp.broadcast_to(x[:, 2 * j:2 * j + 1], (L, 2 * SSD_P))
            hi = jnp.broadcast_to(x[:, 2 * j + 1:2 * j + 2], (L, 2 * SSD_P))
            parts.append(jnp.where(lane < SSD_P, lo, hi))
        return jnp.concatenate(parts, axis=1)

    xdt = xs * expand(dt)
    xw = xdt * expand(dend)
    e_ea = expand(eac)
    causal = _iota((L, L), 0) >= _iota((L, L), 1)

    ys = []
    for g in range(SSD_G):
        bg = bc[:, g * SSD_N:(g + 1) * SSD_N]
        cg = bc[:, SSD_G * SSD_N + g * SSD_N:SSD_G * SSD_N + (g + 1) * SSD_N]
        cb = _dot_nt(_b(cg), _b(bg))
        s_prev = s_ref[g]
        y_inter = _dot_nt(_b(cg), _b(s_prev)) * e_ea[:, g * SSD_GW:(g + 1) * SSD_GW]
        parts = []
        cd_rows = []
        for r in range(SSD_R):
            h = g * SSD_R + r
            seg = acum[:, h:h + 1] - a_t[h:h + 1, :]
            ldec = jnp.exp(jnp.where(causal, seg, NEG))
            parts.append(_dot(_b(cb * ldec), _b(xdt[:, h * SSD_P:(h + 1) * SSD_P])))
            cd_rows.append(jnp.broadcast_to(cdec[:, h:h + 1], (SSD_P, SSD_N)))
        ys.append(jnp.concatenate(parts, axis=1) + y_inter)
        st = _dot_tn(_b(xw[:, g * SSD_GW:(g + 1) * SSD_GW]), _b(bg))
        s_ref[g] = jnp.concatenate(cd_rows, axis=0) * s_prev + st

    y = jnp.concatenate(ys, axis=1) + dexp_ref[...] * xs
    y = y * _silu(z_ref[...])
    outs = []
    for g in range(SSD_G):
        yg = y[:, g * SSD_GW:(g + 1) * SSD_GW]
        outs.append(yg * lax.rsqrt(jnp.mean(yg * yg, axis=-1, keepdims=True) + EPS))
    out_ref[...] = (jnp.concatenate(outs, axis=1) * nw_ref[...]).astype(out_ref.dtype)


def ssd_mixer(p, conv_w, conv_b, dt_bias, a_log, d_skip, norm_w, batch, seq_t):
    m = p.shape[0]
    nc = seq_t // CHUNK
    hp = jnp.zeros((8, SSD_DT_PAD), F32).at[0, :SSD_H].set(dt_bias).at[1, :SSD_H].set(a_log)
    dexp = jnp.repeat(d_skip, SSD_P).reshape(1, SSD_DI)
    conv_b = conv_b.reshape(1, -1)
    row = lambda b, c: b * nc + c
    return pl.pallas_call(
        _ssd_kernel,
        grid=(batch, nc),
        in_specs=[
            pl.BlockSpec((CHUNK, SSD_DI), lambda b, c: (row(b, c), 0)),
            pl.BlockSpec((CHUNK, SSD_DI), lambda b, c: (row(b, c), 1)),
            pl.BlockSpec((CHUNK, SSD_BC), lambda b, c: (row(b, c), 2 * SSD_DI // SSD_BC)),
            pl.BlockSpec((CHUNK, SSD_DT_PAD), lambda b, c: (row(b, c), (2 * SSD_DI + SSD_BC) // SSD_DT_PAD)),
            pl.BlockSpec((SSD_K, SSD_DI), lambda b, c: (0, 0)),
            pl.BlockSpec((SSD_K, SSD_BC), lambda b, c: (0, SSD_DI // SSD_BC)),
            pl.BlockSpec((1, SSD_DI), lambda b, c: (0, 0)),
            pl.BlockSpec((1, SSD_BC), lambda b, c: (0, SSD_DI // SSD_BC)),
            pl.BlockSpec((8, SSD_DT_PAD), lambda b, c: (0, 0)),
            pl.BlockSpec((1, SSD_DI), lambda b, c: (0, 0)),
            pl.BlockSpec((1, SSD_DI), lambda b, c: (0, 0)),
        ],
        out_specs=pl.BlockSpec((CHUNK, SSD_DI), lambda b, c: (row(b, c), 0)),
        out_shape=jax.ShapeDtypeStruct((m, SSD_DI), BF16),
        scratch_shapes=[pltpu.VMEM((SSD_G, SSD_GW, SSD_N), F32), pltpu.VMEM((8, SSD_DI), F32),
                        pltpu.VMEM((8, SSD_BC), F32)],
        compiler_params=_params(("arbitrary", "arbitrary")),
        name="ssd",
    )(p, p, p, p, conv_w, conv_w, conv_b, conv_b, hp, dexp, norm_w.reshape(1, SSD_DI))


MOE_TILE = 256
ROUTE_W1, ROUTE_W2, ROUTE_I1, ROUTE_I2 = 0, 1, 2, 3


def _router_kernel(x_ref, w_ref, r_ref, o_ref):
    x = x_ref[...]
    hn = x * lax.rsqrt(jnp.mean(x * x, axis=-1, keepdims=True) + EPS) * w_ref[...]
    logits = _dot(hn, r_ref[...], HI)
    lane = _iota(logits.shape, 1)
    lane_f = lane.astype(F32)
    logits = jnp.where(lane < N_EXPERTS, logits, -jnp.inf)
    m1 = jnp.max(logits, axis=-1, keepdims=True)
    i1 = jnp.min(jnp.where(logits == m1, lane_f, float(ROUTER_PAD)), axis=-1, keepdims=True)
    rest = jnp.where(lane_f == i1, -jnp.inf, logits)
    m2 = jnp.max(rest, axis=-1, keepdims=True)
    i2 = jnp.min(jnp.where(rest == m2, lane_f, float(ROUTER_PAD)), axis=-1, keepdims=True)
    e2 = jnp.exp(m2 - m1)
    w1 = 1.0 / (1.0 + e2)
    w2 = e2 / (1.0 + e2)
    o_ref[...] = jnp.where(lane == ROUTE_W1, w1, jnp.where(lane == ROUTE_W2, w2,
                           jnp.where(lane == ROUTE_I1, i1, jnp.where(lane == ROUTE_I2, i2, 0.0))))


def router_top2(x, w, router):
    m, d = x.shape
    tm = _tile(m, (520, 512, 256, 128, 64))
    rpad = jnp.zeros((d, ROUTER_PAD), F32).at[:, :N_EXPERTS].set(router)
    return pl.pallas_call(
        _router_kernel,
        grid=(m // tm,),
        in_specs=[pl.BlockSpec((tm, d), lambda i: (i, 0)), pl.BlockSpec((1, d), lambda i: (0, 0)),
                  pl.BlockSpec((d, ROUTER_PAD), lambda i: (0, 0))],
        out_specs=pl.BlockSpec((tm, ROUTER_PAD), lambda i: (i, 0)),
        out_shape=jax.ShapeDtypeStruct((m, ROUTER_PAD), F32),
        compiler_params=_params(("arbitrary",)),
        name="router_top2",
    )(x, w.reshape(1, d), rpad)


def _row_copy(src_hbm, row, dst, r, sem):
    return pltpu.make_async_copy(src_hbm.at[pl.ds(row, 1)], dst.at[pl.ds(r, 1)], sem)


def _gather_norm_kernel(nused_ref, idx_ref, h_hbm, w_ref, o_ref, buf, sem):
    t = pl.program_id(0)
    tg = buf.shape[0]

    @pl.when(t < nused_ref[0])
    def _():
        def start(r, carry):
            _row_copy(h_hbm, idx_ref[t * tg + r], buf, r, sem).start()
            return carry

        def wait(r, carry):
            _row_copy(h_hbm, 0, buf, r, sem).wait()
            return carry

        lax.fori_loop(0, tg, start, 0, unroll=8)
        lax.fori_loop(0, tg, wait, 0, unroll=8)
        x = buf[...]
        hn = x * lax.rsqrt(jnp.mean(x * x, axis=-1, keepdims=True) + EPS) * w_ref[...]
        o_ref[...] = hn.astype(o_ref.dtype)

    @pl.when(t >= nused_ref[0])
    def _():
        o_ref[...] = jnp.zeros_like(o_ref)


def gather_norm(h, w, src, n_used, n_tiles):
    m, d = h.shape
    tg = MOE_TILE
    return pl.pallas_call(
        _gather_norm_kernel,
        grid_spec=pltpu.PrefetchScalarGridSpec(
            num_scalar_prefetch=2,
            grid=(n_tiles,),
            in_specs=[pl.BlockSpec(memory_space=pl.ANY), pl.BlockSpec((1, d), lambda t, *_: (0, 0))],
            out_specs=pl.BlockSpec((tg, d), lambda t, *_: (t, 0)),
            scratch_shapes=[pltpu.VMEM((tg, d), F32), pltpu.SemaphoreType.DMA(())],
        ),
        out_shape=jax.ShapeDtypeStruct((n_tiles * tg, d), BF16),
        compiler_params=_params(("arbitrary",)),
        name="moe_gather",
    )(n_used, src, h, w.reshape(1, d))


def _new_expert(te_ref, t):
    return jnp.logical_or(t == 0, te_ref[t] != te_ref[jnp.maximum(t - 1, 0)])


def _moe_up_kernel(te_ref, nused_ref, a_ref, wg_ref, wu_ref, o_ref, wgb_ref, wub_ref):
    t = pl.program_id(1)

    @pl.when(_new_expert(te_ref, t))
    def _():
        wgb_ref[...] = _b(wg_ref[...])
        wub_ref[...] = _b(wu_ref[...])

    @pl.when(t < nused_ref[0])
    def _():
        a = a_ref[...]
        o_ref[...] = (_silu(_dot(a, wgb_ref[...])) * _dot(a, wub_ref[...])).astype(o_ref.dtype)

    @pl.when(t >= nused_ref[0])
    def _():
        o_ref[...] = jnp.zeros_like(o_ref)


def moe_up(xs, wg, wu, tile_e, n_used):
    p, k = xs.shape
    f = wg.shape[-1]
    tg = MOE_TILE
    tn = _tile(f, (512, 256, 128))
    w_spec = pl.BlockSpec((None, k, tn), lambda j, t, te, nu: (te[t], 0, j))
    return pl.pallas_call(
        _moe_up_kernel,
        grid_spec=pltpu.PrefetchScalarGridSpec(
            num_scalar_prefetch=2,
            grid=(f // tn, p // tg),
            in_specs=[pl.BlockSpec((tg, k), lambda j, t, te, nu: (t, 0)), w_spec, w_spec],
            out_specs=pl.BlockSpec((tg, tn), lambda j, t, te, nu: (t, j)),
            scratch_shapes=[pltpu.VMEM((k, tn), BF16), pltpu.VMEM((k, tn), BF16)],
        ),
        out_shape=jax.ShapeDtypeStruct((p, f), BF16),
        compiler_params=_params(("arbitrary", "arbitrary")),
        name="moe_up",
    )(tile_e, n_used, xs, wg, wu)


def _moe_down_kernel(te_ref, nused_ref, a_ref, w_ref, o_ref, wb_ref):
    t = pl.program_id(1)

    @pl.when(_new_expert(te_ref, t))
    def _():
        wb_ref[...] = _b(w_ref[...])

    @pl.when(t < nused_ref[0])
    def _():
        o_ref[...] = _dot(a_ref[...], wb_ref[...])

    @pl.when(t >= nused_ref[0])
    def _():
        o_ref[...] = jnp.zeros_like(o_ref)


def moe_down(fs, wd, tile_e, n_used):
    p, f = fs.shape
    d = wd.shape[-1]
    tg = MOE_TILE
    tn = _tile(d, (512, 256, 128))
    return pl.pallas_call(
        _moe_down_kernel,
        grid_spec=pltpu.PrefetchScalarGridSpec(
            num_scalar_prefetch=2,
            grid=(d // tn, p // tg),
            in_specs=[pl.BlockSpec((tg, f), lambda j, t, te, nu: (t, 0)),
                      pl.BlockSpec((None, f, tn), lambda j, t, te, nu: (te[t], 0, j))],
            out_specs=pl.BlockSpec((tg, tn), lambda j, t, te, nu: (t, j)),
            scratch_shapes=[pltpu.VMEM((f, tn), BF16)],
        ),
        out_shape=jax.ShapeDtypeStruct((p, d), F32),
        compiler_params=_params(("arbitrary", "arbitrary")),
        name="moe_down",
    )(tile_e, n_used, fs, wd)


def _combine_norm_kernel(pos_ref, h_ref, route_ref, y_hbm, w_ref, o_ref, buf0, buf1, sem, *, seq_t, m):
    b = pl.program_id(0)
    i = pl.program_id(1)
    tm = buf0.shape[0]
    base = b * seq_t + (N_PAD + N_META) + i * tm

    def start(r, carry):
        _row_copy(y_hbm, pos_ref[base + r], buf0, r, sem).start()
        _row_copy(y_hbm, pos_ref[m + base + r], buf1, r, sem).start()
        return carry

    def wait(r, carry):
        _row_copy(y_hbm, 0, buf0, r, sem).wait()
        _row_copy(y_hbm, 0, buf1, r, sem).wait()
        return carry

    lax.fori_loop(0, tm, start, 0, unroll=8)
    lax.fori_loop(0, tm, wait, 0, unroll=8)
    w1 = route_ref[:, ROUTE_W1:ROUTE_W1 + 1]
    w2 = route_ref[:, ROUTE_W2:ROUTE_W2 + 1]
    x = h_ref[...] + (w1 * buf0[...] + w2 * buf1[...])
    o_ref[...] = (x * lax.rsqrt(jnp.mean(x * x, axis=-1, keepdims=True) + EPS) * w_ref[...]).astype(o_ref.dtype)


def combine_norm(h, route, ys, pos, w, batch, seq_t):
    m, d = h.shape
    seq = seq_t - N_PAD - N_META
    tm = CHUNK
    lead = (N_PAD + N_META) // tm
    nct = seq_t // tm
    row = lambda b, i, *_: (b * nct + lead + i, 0)
    return pl.pallas_call(
        functools.partial(_combine_norm_kernel, seq_t=seq_t, m=m),
        grid_spec=pltpu.PrefetchScalarGridSpec(
            num_scalar_prefetch=1,
            grid=(batch, seq // tm),
            in_specs=[pl.BlockSpec((tm, d), row), pl.BlockSpec((tm, ROUTER_PAD), row),
                      pl.BlockSpec(memory_space=pl.ANY), pl.BlockSpec((1, d), lambda b, i, *_: (0, 0))],
            out_specs=pl.BlockSpec((tm, d), lambda b, i, *_: (b * (seq // tm) + i, 0)),
            scratch_shapes=[pltpu.VMEM((tm, d), F32), pltpu.VMEM((tm, d), F32), pltpu.SemaphoreType.DMA(())],
        ),
        out_shape=jax.ShapeDtypeStruct((batch * seq, d), F32),
        compiler_params=_params(("arbitrary", "arbitrary")),
        name="moe_combine",
    )(pos, h, route, ys, w.reshape(1, d))


def moe_routing(route, m):
    tg = MOE_TILE
    n_tiles = (2 * m + tg - 1) // tg + N_EXPERTS
    e_flat = jnp.concatenate([route[:, ROUTE_I1], route[:, ROUTE_I2]]).astype(jnp.int32)
    onehot = (e_flat[:, None] == jnp.arange(N_EXPERTS, dtype=jnp.int32)[None, :]).astype(jnp.int32)
    csum = jnp.cumsum(onehot, axis=0)
    rank = jnp.sum(csum * onehot, axis=1) - 1
    counts = csum[-1]
    padded = ((counts + tg - 1) // tg) * tg
    ends = jnp.cumsum(padded)
    starts = ends - padded
    pos = jnp.sum(starts[None, :] * onehot, axis=1) + rank
    tok = jnp.concatenate([jnp.arange(m, dtype=jnp.int32)] * 2)
    src = jnp.zeros((n_tiles * tg,), jnp.int32).at[pos].set(tok)
    n_used = (ends[-1] // tg).astype(jnp.int32)
    tile_start = jnp.arange(n_tiles, dtype=jnp.int32) * tg
    tile_e = jnp.sum((tile_start[:, None] >= ends[None, :]).astype(jnp.int32), axis=1)
    last_e = jnp.sum((((n_used - 1) * tg) >= ends).astype(jnp.int32))
    tile_e = jnp.where(jnp.arange(n_tiles) < n_used, tile_e, last_e)
    return src, pos, tile_e, n_used.reshape(1), n_tiles


def kernel(x, meta_tokens, e_norm_mix, e_w_in, ml_i_bias, ml_f_bias, ml_out_norm, rw_mu, rw_w0, rw_w_up, rw_a0, rw_a_up, rw_g_up, rw_k_k, rw_k_a, rw_r_k, rw_ln_w, rw_ln_b, e_w_out, e_norm_ffn, ffn_w_gate, ffn_w_up, ffn_w_down, o_norm_mix, ssd_w_in, ssd_conv_w, ssd_conv_b, ssd_dt_bias, ssd_A_log, ssd_D, ssd_norm, ssd_w_out, o_norm_ffn, moe_router, moe_w_gate, moe_w_up, moe_w_down, final_norm):
    batch, seq, d = x.shape
    seq_t = N_PAD + N_META + seq
    m = batch * seq_t
    h = jnp.concatenate([jnp.zeros((batch, N_PAD, d), x.dtype),
                         jnp.broadcast_to(meta_tokens.astype(x.dtype), (batch, N_META, d)), x], axis=1)
    h = h.reshape(m, d)

    w = e_w_in[0]
    ml_cols = 2 * ML_QK + 2 * ML_V
    rw0 = ml_cols + 2 * ML_HEADS
    n_lora = RW_DECAY_LORA + RW_A_LORA + RW_GATE_LORA
    w_pad = jnp.concatenate([
        w[:, :ml_cols], w[:, rw0:rw0 + 3 * RW_DIM], w[:, rw0 + 3 * RW_DIM:rw0 + 3 * RW_DIM + n_lora],
        jnp.zeros((d, E_TAIL_GATES - n_lora), F32), w[:, ml_cols:rw0],
        jnp.zeros((d, E_TAIL - E_TAIL_GATES - 2 * ML_HEADS), F32)], axis=1)
    hn = rmsnorm(h, e_norm_mix[0], BF16)
    p = mm_up(hn, w_pad, F32)
    y_ml = mlstm_group(p, ml_i_bias[0], ml_f_bias[0], ml_out_norm[0], batch, seq_t)
    y_rw = rwkv7_group(p, rw_mu[0], rw_w0[0], rw_w_up[0], rw_a0[0], rw_a_up[0], rw_g_up[0], rw_k_k[0],
                       rw_k_a[0], rw_r_k[0], rw_ln_w[0], rw_ln_b[0], batch, seq_t)
    y = jnp.concatenate([y_ml, y_rw], axis=1)
    h = mm_down(y, _b(e_w_out[0]), h, seq_t)
    hn = rmsnorm(h, e_norm_ffn[0], BF16)
    f = swiglu_up(hn, ffn_w_gate[0], ffn_w_up[0])
    h = mm_down(f, _b(ffn_w_down[0]), h, seq_t)

    w = ssd_w_in[0]
    n_real = 2 * SSD_DI + SSD_BC + SSD_H
    w_pad = jnp.concatenate([w, jnp.zeros((d, SSD_COLS - n_real), F32)], axis=1)
    hn = rmsnorm(h, o_norm_mix[0], BF16)
    p = mm_up(hn, w_pad, F32, tn_prefs=(1152, 384, 128))
    y = ssd_mixer(p, ssd_conv_w[0], ssd_conv_b[0], ssd_dt_bias[0], ssd_A_log[0], ssd_D[0], ssd_norm[0],
                  batch, seq_t)
    h = mm_down(y, _b(ssd_w_out[0]), h, seq_t)
    route = router_top2(h, o_norm_ffn[0], moe_router[0])
    src, pos, tile_e, n_used, n_tiles = moe_routing(route, m)
    xs = gather_norm(h, o_norm_ffn[0], src, n_used, n_tiles)
    fs = moe_up(xs, moe_w_gate[0], moe_w_up[0], tile_e, n_used)
    ys = moe_down(fs, moe_w_down[0], tile_e, n_used)
    out = combine_norm(h, route, ys, pos, final_norm, batch, seq_t)
    return out.reshape(batch, seq, d).astype(x.dtype)
```

```python
import functools

import jax
import jax.numpy as jnp
from jax import lax
from jax.experimental import pallas as pl
from jax.experimental.pallas import tpu as pltpu

F32 = jnp.float32
BF16 = jnp.bfloat16
HI = lax.Precision.HIGHEST

N_META = 16
CHUNK = 64
N_PAD = CHUNK - N_META
NEG = -1e30
EPS = 1e-6

ML_HEADS = 4
ML_DQK = 128
ML_DV = 256
ML_GATE_CAP = 15.0
ML_QK = ML_HEADS * ML_DQK
ML_V = ML_HEADS * ML_DV

RW_HEADS = 16
RW_N = 64
RW_DIM = RW_HEADS * RW_N
RW_DECAY_LORA = 64
RW_A_LORA = 64
RW_GATE_LORA = 160
RW_LN_EPS = 64e-5
RW_GROUP = 256
RW_HPG = RW_GROUP // RW_N
RW_GPS = 2
LORA_ROWS = 384

E_OFF_GATES = 2 * ML_QK + 2 * ML_V
E_OFF_RW = E_OFF_GATES + 2 * ML_HEADS
RW_LANE_OFF = E_OFF_RW % 128

SSD_DI = 4096
SSD_P = 64
SSD_H = SSD_DI // SSD_P
SSD_N = 128
SSD_G = 8
SSD_R = SSD_H // SSD_G
SSD_K = 4
SSD_GW = SSD_DI // SSD_G
SSD_BC = 2 * SSD_G * SSD_N
SSD_DT_PAD = 128

N_EXPERTS = 8
ROUTER_PAD = 128

VMEM_LIMIT = 56 * 1024 * 1024


def _tile(n, prefs):
    for p in prefs:
        if n % p == 0:
            return p
    return n


_ROW_TILES = (1040, 832, 640, 520, 512, 256, 128, 64)


def _params(sem):
    return pltpu.CompilerParams(dimension_semantics=sem, vmem_limit_bytes=VMEM_LIMIT)


def _dot(a, b, prec=None):
    return lax.dot_general(a, b, (((1,), (0,)), ((), ())), precision=prec, preferred_element_type=F32)


def _dot_nt(a, b, prec=None):
    return lax.dot_general(a, b, (((1,), (1,)), ((), ())), precision=prec, preferred_element_type=F32)


def _dot_tn(a, b, prec=None):
    return lax.dot_general(a, b, (((0,), (0,)), ((), ())), precision=prec, preferred_element_type=F32)


def _b(x):
    return x.astype(BF16)


def _bdot(dot, a, b):
    return dot(_b(a), _b(b))


def _dot_split(a, b):
    hi = _b(a)
    lo = _b(a - hi.astype(F32))
    bb = _b(b)
    return _dot(hi, bb) + _dot(lo, bb)


def _sigmoid(x):
    return 1.0 / (1.0 + jnp.exp(-x))


def _softplus(x):
    return jnp.maximum(x, 0.0) + jnp.log1p(jnp.exp(-jnp.abs(x)))


def _silu(x):
    return x * _sigmoid(x)


def _iota(shape, dim):
    return lax.broadcasted_iota(jnp.int32, shape, dim)


def _tri(n):
    return (_iota((n, n), 0) >= _iota((n, n), 1)).astype(F32)


def _interleave(chains):
    chains = list(chains)
    while chains:
        for ch in list(chains):
            try:
                next(ch)
            except StopIteration:
                chains.remove(ch)


def _rmsnorm_kernel(x_ref, w_ref, o_ref):
    x = x_ref[...]
    ms = jnp.mean(x * x, axis=-1, keepdims=True)
    o_ref[...] = (x * lax.rsqrt(ms + EPS) * w_ref[...]).astype(o_ref.dtype)


def rmsnorm(x, w, out_dtype):
    m, d = x.shape
    tm = _tile(m, _ROW_TILES)
    return pl.pallas_call(
        _rmsnorm_kernel,
        grid=(m // tm,),
        in_specs=[pl.BlockSpec((tm, d), lambda i: (i, 0)), pl.BlockSpec((1, d), lambda i: (0, 0))],
        out_specs=pl.BlockSpec((tm, d), lambda i: (i, 0)),
        out_shape=jax.ShapeDtypeStruct((m, d), out_dtype),
        compiler_params=_params(("arbitrary",)),
        name="rmsnorm",
    )(x, w.reshape(1, d))


def _mm_up_kernel(a_ref, w_ref, o_ref, wb_ref, *, n_valid):
    @pl.when(pl.program_id(1) == 0)
    def _():
        tn = w_ref.shape[1]
        col = pl.program_id(0) * tn + _iota((1, tn), 1)
        wb_ref[...] = _b(jnp.where(col < n_valid, w_ref[...], 0.0))

    o_ref[...] = _dot(a_ref[...], wb_ref[...]).astype(o_ref.dtype)


def mm_up(a, w, out_dtype, tn):
    m, k = a.shape
    n = w.shape[1]
    tm = _tile(m, _ROW_TILES)
    nt = pl.cdiv(n, tn)
    return pl.pallas_call(
        functools.partial(_mm_up_kernel, n_valid=n),
        grid=(nt, m // tm),
        in_specs=[pl.BlockSpec((tm, k), lambda j, i: (i, 0)), pl.BlockSpec((k, tn), lambda j, i: (0, j))],
        out_specs=pl.BlockSpec((tm, tn), lambda j, i: (i, j)),
        out_shape=jax.ShapeDtypeStruct((m, nt * tn), out_dtype),
        scratch_shapes=[pltpu.VMEM((k, tn), BF16)],
        compiler_params=_params(("arbitrary", "arbitrary")),
        name="mm_up",
    )(a, w)


def _swiglu_kernel(a_ref, wg_ref, wu_ref, o_ref, wgb_ref, wub_ref):
    @pl.when(pl.program_id(1) == 0)
    def _():
        wgb_ref[...] = _b(wg_ref[...])
        wub_ref[...] = _b(wu_ref[...])

    a = a_ref[...]
    o_ref[...] = (_silu(_dot(a, wgb_ref[...])) * _dot(a, wub_ref[...])).astype(o_ref.dtype)


def swiglu_up(a, wg, wu):
    m, k = a.shape
    f = wg.shape[-1]
    tm = _tile(m, _ROW_TILES)
    tn = _tile(f, (512, 256, 128))
    w_spec = pl.BlockSpec((k, tn), lambda j, i: (0, j))
    return pl.pallas_call(
        _swiglu_kernel,
        grid=(f // tn, m // tm),
        in_specs=[pl.BlockSpec((tm, k), lambda j, i: (i, 0)), w_spec, w_spec],
        out_specs=pl.BlockSpec((tm, tn), lambda j, i: (i, j)),
        out_shape=jax.ShapeDtypeStruct((m, f), BF16),
        scratch_shapes=[pltpu.VMEM((k, tn), BF16), pltpu.VMEM((k, tn), BF16)],
        compiler_params=_params(("arbitrary", "arbitrary")),
        name="swiglu_up",
    )(a, wg, wu)


def _mm_down_kernel(*refs, tm, seq_t):
    *a_refs, w_ref, r_ref, o_ref = refs
    acc = None
    off = 0
    for a_ref in a_refs:
        k = a_ref.shape[1]
        part = _dot(a_ref[...], w_ref[off:off + k, :])
        acc = part if acc is None else acc + part
        off += k
    t0 = (pl.program_id(0) * tm) % seq_t
    valid = ((t0 + _iota((tm, 1), 0)) >= N_PAD).astype(F32)
    o_ref[...] = r_ref[...] + acc * valid


def mm_down(a_parts, w, res, seq_t):
    m = a_parts[0].shape[0]
    k, n = w.shape
    assert sum(a.shape[1] for a in a_parts) == k
    tm = _tile(seq_t, (520, 512, 256, 128, 64))
    tn = _tile(n, (512, 256, 128))
    a_specs = [pl.BlockSpec((tm, a.shape[1]), lambda i, j: (i, 0)) for a in a_parts]
    return pl.pallas_call(
        functools.partial(_mm_down_kernel, tm=tm, seq_t=seq_t),
        grid=(m // tm, n // tn),
        in_specs=a_specs + [pl.BlockSpec((k, tn), lambda i, j: (0, j)), pl.BlockSpec((tm, tn), lambda i, j: (i, j))],
        out_specs=pl.BlockSpec((tm, tn), lambda i, j: (i, j)),
        out_shape=jax.ShapeDtypeStruct((m, n), F32),
        compiler_params=_params(("arbitrary", "arbitrary")),
        name="mm_down",
    )(*a_parts, w, res)


def _log_sigmoid(x):
    return -_softplus(-x)


def _mlstm_kernel(q_ref, k_ref, v_ref, o_ref, g_ref, bias_ref, onorm_ref, out_ref, c_ref, n_ref, m_ref):
    c = pl.program_id(0)
    L = CHUNK

    @pl.when(c == 0)
    def _():
        c_ref[...] = jnp.zeros_like(c_ref)
        n_ref[...] = jnp.zeros_like(n_ref)
        m_ref[...] = jnp.zeros_like(m_ref)

    valid = (c * L + _iota((L, 1), 0)) >= N_PAD
    causal = _iota((L, L), 0) >= _iota((L, L), 1)
    tri = _tri(L)
    scale = ML_DQK ** -0.5

    def chain(b, h, bcum, ig, b_t, i_t):
        q = q_ref[b, :, h * ML_DQK:(h + 1) * ML_DQK] * scale
        k = k_ref[b, :, h * ML_DQK:(h + 1) * ML_DQK]
        v = v_ref[b, :, h * ML_DV:(h + 1) * ML_DV]
        og = o_ref[b, :, h * ML_DV:(h + 1) * ML_DV]
        bcol = bcum[:, ML_HEADS + h:ML_HEADS + h + 1]
        icol = ig[:, h:h + 1]
        brow = b_t[ML_HEADS + h:ML_HEADS + h + 1, :]
        irow = i_t[h:h + 1, :]
        g = bcol[L - 1:L, :]
        m_prev = m_ref[b, h:h + 1, 0:1]
        c_prev = c_ref[b, h]
        n_prev = n_ref[b, h:h + 1, :]

        dm = jnp.where(causal, bcol - brow + irow, NEG)
        m_inter = bcol + m_prev
        m_t = jnp.maximum(m_inter, jnp.max(dm, axis=-1, keepdims=True))
        yield
        s = _dot_nt(_b(q), _b(k)) * jnp.exp(dm - m_t)
        w_inter = jnp.exp(m_inter - m_t)
        yield
        num = _dot(_b(s), _b(v)) + w_inter * _dot(_b(q), _b(c_prev))
        den = jnp.sum(s, axis=-1, keepdims=True) + w_inter * jnp.sum(q * n_prev, axis=-1, keepdims=True)
        yield
        hh = num / jnp.maximum(jnp.abs(den), jnp.exp(-m_t))
        hh = hh * lax.rsqrt(jnp.mean(hh * hh, axis=-1, keepdims=True) + EPS)
        hh = hh * onorm_ref[:, h * ML_DV:(h + 1) * ML_DV] * _sigmoid(og)
        out_ref[b, :, h * ML_DV:(h + 1) * ML_DV] = hh.astype(out_ref.dtype)
        yield

        a_col = g - bcol + icol
        m_new = jnp.maximum(g + m_prev, jnp.max(a_col, axis=0, keepdims=True))
        dec = jnp.exp(g + m_prev - m_new)
        wk = jnp.exp(a_col - m_new) * k
        yield
        c_ref[b, h] = dec * c_prev + _dot_tn(_b(wk), _b(v))
        n_ref[b, h:h + 1, :] = dec * n_prev + jnp.sum(wk, axis=0, keepdims=True)
        m_ref[b, h:h + 1, :] = jnp.broadcast_to(m_new, (1, m_ref.shape[2]))

    chains = []
    for b in range(q_ref.shape[0]):
        gates = g_ref[b] + bias_ref[...]
        sc = ML_GATE_CAP * jnp.tanh(gates / ML_GATE_CAP)
        ig = jnp.where(valid, sc, NEG)
        lf = jnp.where(valid, _log_sigmoid(sc), 0.0)
        bcum = _dot(tri, lf, HI)
        chains += [chain(b, h, bcum, ig, bcum.T, ig.T) for h in range(ML_HEADS)]
    _interleave(chains)


def mlstm_group(p, i_bias, f_bias, out_norm):
    batch, seq_t, _ = p.shape
    nc = seq_t // CHUNK
    bias = jnp.zeros((1, 128), F32).at[0, 0:ML_HEADS].set(i_bias).at[0, ML_HEADS:2 * ML_HEADS].set(f_bias)
    return pl.pallas_call(
        _mlstm_kernel,
        grid=(nc,),
        in_specs=[
            pl.BlockSpec((batch, CHUNK, ML_QK), lambda c: (0, c, 0)),
            pl.BlockSpec((batch, CHUNK, ML_QK), lambda c: (0, c, 1)),
            pl.BlockSpec((batch, CHUNK, ML_V), lambda c: (0, c, 1)),
            pl.BlockSpec((batch, CHUNK, ML_V), lambda c: (0, c, 2)),
            pl.BlockSpec((batch, CHUNK, 128), lambda c: (0, c, E_OFF_GATES // 128)),
            pl.BlockSpec((1, 128), lambda c: (0, 0)),
            pl.BlockSpec((1, ML_V), lambda c: (0, 0)),
        ],
        out_specs=pl.BlockSpec((batch, CHUNK, ML_V), lambda c: (0, c, 0)),
        out_shape=jax.ShapeDtypeStruct((batch, seq_t, ML_V), BF16),
        scratch_shapes=[pltpu.VMEM((batch, ML_HEADS, ML_DQK, ML_DV), F32), pltpu.VMEM((batch, 8, ML_DQK), F32),
                        pltpu.VMEM((batch, 8, 128), F32)],
        compiler_params=_params(("arbitrary",)),
        name="mlstm",
    )(p, p, p, p, p, bias, out_norm.reshape(1, ML_V))


def _rwkv_kernel(*refs):
    nb = 2 * RW_GPS + 1
    r_blk, k_blk, v_blk = refs[0:nb], refs[nb:2 * nb], refs[2 * nb:3 * nb]
    tail_ref, mu3_ref, mut_ref, chan_ref, wl_ref, out_ref, h_ref, cr_ref, ck_ref, cv_ref, ct_ref = refs[3 * nb:]
    c = pl.program_id(1)
    L = CHUNK
    G = RW_GROUP

    @pl.when(c == 0)
    def _():
        h_ref[...] = jnp.zeros_like(h_ref)
        cr_ref[...] = jnp.zeros_like(cr_ref)
        ck_ref[...] = jnp.zeros_like(ck_ref)
        cv_ref[...] = jnp.zeros_like(cv_ref)
        ct_ref[...] = jnp.zeros_like(ct_ref)

    row = _iota((L, 1), 0)
    valid = (c * L + row) >= N_PAD

    ri = _iota((G, G), 0)
    ci = _iota((G, G), 1)
    same = (ri >> 6) == (ci >> 6)
    tl = ri & (RW_N - 1)
    sl = ci & (RW_N - 1)
    bd_mask = same.astype(F32)
    eye = (ri == ci).astype(F32)
    lower = same & (tl >= sl)
    strict = same & (tl > sl)
    ones_bd = _b(bd_mask)
    tri = _tri(L)
    lane = _iota((L, LORA_ROWS), 1)

    def head_sum(x):
        hi = _b(x)
        return _dot(hi, ones_bd) + _dot(_b(x - hi.astype(F32)), ones_bd)

    def bd(x):
        return _b(jnp.concatenate([x] * RW_HPG, axis=0) * bd_mask)

    def shift_lerp(x, carry_ref, idx, mu):
        prev = carry_ref[idx]
        xs = jnp.where(row == 0, prev, pltpu.roll(x, 1, 0))
        carry_ref[idx] = x[L - 1:L, :]
        return x + mu * (xs - x)

    def unshift(blocks, b, j):
        x = jnp.concatenate([blk[b] for blk in blocks[2 * j:2 * j + 3]], axis=1)
        return x[:, RW_LANE_OFF:RW_LANE_OFF + G]

    def chain(b, j):
        gl = slice(j * G, (j + 1) * G)
        w0 = chan_ref[0:1, gl]
        a0 = chan_ref[1:2, gl]
        k_k = chan_ref[2:3, gl]
        k_a = chan_ref[3:4, gl]
        r_k = chan_ref[4:5, gl]
        ln_w = chan_ref[5:6, gl]
        ln_b = chan_ref[6:7, gl]
        r = shift_lerp(unshift(r_blk, b, j), cr_ref, (b, j), mu3_ref[0:1, gl])
        k = shift_lerp(unshift(k_blk, b, j), ck_ref, (b, j), mu3_ref[1:2, gl])
        v = shift_lerp(unshift(v_blk, b, j), cv_ref, (b, j), mu3_ref[2:3, gl])
        tail = shift_lerp(tail_ref[b], ct_ref, (b, j), mut_ref[...])

        lo1 = RW_LANE_OFF + RW_DECAY_LORA
        lo2 = lo1 + RW_A_LORA
        lo3 = lo2 + RW_GATE_LORA
        act = jnp.where(lane < RW_LANE_OFF, 0.0,
                        jnp.where(lane < lo1, jnp.tanh(tail),
                                  jnp.where(lane < lo2, tail, jnp.where(lane < lo3, _sigmoid(tail), 0.0))))
        lw = _dot(act[:, 0:128], wl_ref[0, 0:128, gl], HI)
        la = _bdot(_dot, act[:, 0:256], wl_ref[1, 0:256, gl])
        gate = _bdot(_dot, act[:, 128:384], wl_ref[2, 128:384, gl])
        yield

        w_log = -_softplus(-(w0 + lw)) - 0.5
        ld = jnp.where(valid, -jnp.exp(w_log), 0.0)
        a = _sigmoid(a0 + la)
        kk = k * k_k
        kk = kk * lax.rsqrt(jnp.maximum(head_sum(kk * kk), 1e-24))
        k2 = k * (1.0 + (a - 1.0) * k_a)
        k2 = jnp.where(valid, k2, 0.0)
        kk = jnp.where(valid, kk, 0.0)
        va = -kk
        vb = kk * a

        yield
        cum = _dot(tri, ld, HI)
        cum_l = cum[L - 1:L, :]
        w_t = jnp.exp(cum)
        w_inv = jnp.exp(-cum)
        w_prev = jnp.exp(cum - ld)
        w_end = jnp.exp(cum_l - cum)
        w_last = jnp.exp(cum_l)

        a_bd = bd(va * w_prev)
        r_bd = bd(r * w_t)
        b_bd = bd(vb * w_inv)
        k_bd = bd(k2 * w_inv)
        bl_bd = bd(vb * w_end)
        kl_bd = bd(k2 * w_end)
        v_bd = bd(v)
        yield

        n_ab = jnp.where(strict, _dot_nt(a_bd, b_bd), 0.0)
        a_ak = _b(jnp.where(strict, _dot_nt(a_bd, k_bd), 0.0))
        m_rb = _b(jnp.where(lower, _dot_nt(r_bd, b_bd), 0.0))
        m_rk = _b(jnp.where(lower, _dot_nt(r_bd, k_bd), 0.0))
        yield

        t_inv = eye + n_ab
        pw = n_ab
        for _ in range(5):
            pw_b = _b(pw)
            pw = _dot(pw_b, pw_b)
            yield
            t_inv = t_inv + _bdot(_dot, t_inv, pw)
            yield

        t_b = _b(t_inv)
        av = _dot(a_ak, v_bd)
        yield
        a_p = _b(_dot(t_b, a_bd))
        u0 = _b(_dot(t_b, _b(av)))
        yield
        p_mat = eye * w_last + _dot_tn(bl_bd, a_p)
        q_mat = _dot_tn(bl_bd, u0) + _dot_tn(kl_bd, v_bd)
        r_p = r_bd.astype(F32) + _dot(m_rb, a_p)
        y0 = _dot(m_rb, u0) + _dot(m_rk, v_bd)
        yield

        h0 = _b(h_ref[b, j])
        y_bd = _dot(_b(r_p), h0) + y0
        h_ref[b, j] = _dot(_b(p_mat), h0) + q_mat
        y = y_bd[0:L] + y_bd[L:2 * L] + y_bd[2 * L:3 * L] + y_bd[3 * L:4 * L]
        yield

        inv_n = 1.0 / RW_N
        mean = head_sum(y) * inv_n
        d = y - mean
        var = head_sum(d * d) * inv_n
        yn = d * lax.rsqrt(var + RW_LN_EPS) * ln_w + ln_b
        bonus = head_sum(r * k2 * r_k) * v
        out_ref[b, :, gl] = ((yn + bonus) * gate).astype(out_ref.dtype)

    _interleave(chain(b, j) for b in range(out_ref.shape[0]) for j in range(RW_GPS))


def rwkv7_group(p, mu, w0, w_up, a0, a_up, g_up, k_k, k_a, r_k, ln_w, ln_b):
    batch, seq_t, _ = p.shape
    nc = seq_t // CHUNK
    ng = RW_DIM // RW_GROUP
    n_lora = RW_DECAY_LORA + RW_A_LORA + RW_GATE_LORA
    o = RW_LANE_OFF
    mu3 = mu[:3 * RW_DIM].reshape(3, RW_DIM)
    mut = jnp.zeros((1, LORA_ROWS), F32).at[0, o:o + n_lora].set(mu[3 * RW_DIM:])
    chan = jnp.stack([w0, a0, k_k, k_a, r_k.reshape(RW_DIM), ln_w, ln_b, jnp.zeros((RW_DIM,), F32)])
    wl = jnp.zeros((3, LORA_ROWS, RW_DIM), F32)
    wl = wl.at[0, o:o + RW_DECAY_LORA].set(w_up)
    wl = wl.at[1, o + RW_DECAY_LORA:o + RW_DECAY_LORA + RW_A_LORA].set(a_up)
    wl = wl.at[2, o + RW_DECAY_LORA + RW_A_LORA:o + n_lora].set(g_up)

    gw = RW_GPS * RW_GROUP

    def col(seg, part):
        first = (E_OFF_RW + seg * RW_DIM - o) // 128
        return pl.BlockSpec((batch, CHUNK, 128), lambda g, c: (0, c, first + 2 * RW_GPS * g + part))

    nb = 2 * RW_GPS + 1
    cols = [col(seg, part) for seg in range(3) for part in range(nb)]
    carry = lambda width: pltpu.VMEM((batch, RW_GPS, 1, width), F32)
    return pl.pallas_call(
        _rwkv_kernel,
        grid=(ng // RW_GPS, nc),
        in_specs=cols + [
            pl.BlockSpec((batch, CHUNK, LORA_ROWS), lambda g, c: (0, c, (E_OFF_RW + 3 * RW_DIM - o) // LORA_ROWS)),
            pl.BlockSpec((3, gw), lambda g, c: (0, g)),
            pl.BlockSpec((1, LORA_ROWS), lambda g, c: (0, 0)),
            pl.BlockSpec((8, gw), lambda g, c: (0, g)),
            pl.BlockSpec((3, LORA_ROWS, gw), lambda g, c: (0, 0, g)),
        ],
        out_specs=pl.BlockSpec((batch, CHUNK, gw), lambda g, c: (0, c, g)),
        out_shape=jax.ShapeDtypeStruct((batch, seq_t, RW_DIM), BF16),
        scratch_shapes=[pltpu.VMEM((batch, RW_GPS, RW_GROUP, RW_GROUP), F32), carry(RW_GROUP), carry(RW_GROUP),
                        carry(RW_GROUP), carry(LORA_ROWS)],
        compiler_params=_params(("arbitrary", "arbitrary")),
        name="rwkv7",
    )(*([p] * (3 * nb + 1)), mu3, mut, chan, wl)


def _ssd_kernel(z_ref, xs_ref, bc_ref, dt_ref, cwx_ref, cwb_ref, cbx_ref, cbb_ref, hp_ref, dexp_ref, nw_ref,
                out_ref, s_ref, px_ref, pb_ref):
    c = pl.program_id(0)
    L = CHUNK

    @pl.when(c == 0)
    def _():
        s_ref[...] = jnp.zeros_like(s_ref)
        px_ref[...] = jnp.zeros_like(px_ref)
        pb_ref[...] = jnp.zeros_like(pb_ref)

    row8 = _iota((8, 1), 0)
    valid = (c * L + _iota((L, 1), 0)) >= N_PAD
    lane = _iota((L, 2 * SSD_P), 1)
    causal = _iota((L, L), 0) >= _iota((L, L), 1)
    tri = _tri(L)

    def conv_silu(x, prev_ref, b, w_ref, b_ref):
        prev = prev_ref[b]
        acc = x * w_ref[SSD_K - 1:SSD_K, :] + b_ref[...]
        for j in range(1, SSD_K):
            xr = pltpu.roll(x, j, 0)
            pr = pltpu.roll(prev, j, 0)
            top = jnp.where(row8 < j, pr, xr[0:8])
            sh = jnp.concatenate([top, xr[8:]], axis=0)
            acc = acc + sh * w_ref[SSD_K - 1 - j:SSD_K - j, :]
        prev_ref[b] = x[L - 8:L, :]
        return _silu(acc)

    def expand(x):
        parts = []
        for j in range(SSD_H // 2):
            lo = jnp.broadcast_to(x[:, 2 * j:2 * j + 1], (L, 2 * SSD_P))
            hi = jnp.broadcast_to(x[:, 2 * j + 1:2 * j + 2], (L, 2 * SSD_P))
            parts.append(jnp.where(lane < SSD_P, lo, hi))
        return jnp.concatenate(parts, axis=1)

    def chain(b):
        xs = conv_silu(xs_ref[b], px_ref, b, cwx_ref, cbx_ref)
        yield
        bc = conv_silu(bc_ref[b], pb_ref, b, cwb_ref, cbb_ref)
        yield

        dt = jnp.where(valid, _softplus(dt_ref[b] + hp_ref[0:1, :]), 0.0)
        a = dt * (-jnp.exp(hp_ref[1:2, :]))
        acum = _dot(tri, a, HI)
        a_t = acum.T
        dt_t = dt.T
        a_last = acum[L - 1:L, :]
        cdec = jnp.exp(a_last)
        yield
        xw = xs * expand(dt * jnp.exp(a_last - acum))
        yield
        e_ea = expand(jnp.exp(acum))
        yield

        ys = []
        for g in range(SSD_G):
            bg = bc[:, g * SSD_N:(g + 1) * SSD_N]
            cg = bc[:, SSD_G * SSD_N + g * SSD_N:SSD_G * SSD_N + (g + 1) * SSD_N]
            cb = _dot_nt(_b(cg), _b(bg))
            s_prev = s_ref[b, g]
            y_inter = _dot_nt(_b(cg), _b(s_prev)) * e_ea[:, g * SSD_GW:(g + 1) * SSD_GW]
            yield
            parts = []
            cd_rows = []
            for r in range(SSD_R):
                h = g * SSD_R + r
                seg = acum[:, h:h + 1] - a_t[h:h + 1, :]
                ldec = jnp.exp(jnp.where(causal, seg, NEG))
                parts.append(_dot(_b(cb * ldec * dt_t[h:h + 1, :]), _b(xs[:, h * SSD_P:(h + 1) * SSD_P])))
                cd_rows.append(jnp.broadcast_to(cdec[:, h:h + 1], (SSD_P, SSD_N)))
                if r % 2:
                    yield
            ys.append(jnp.concatenate(parts, axis=1) + y_inter)
            st = _dot_tn(_b(xw[:, g * SSD_GW:(g + 1) * SSD_GW]), _b(bg))
            s_ref[b, g] = jnp.concatenate(cd_rows, axis=0) * s_prev + st
            yield

        y = jnp.concatenate(ys, axis=1) + dexp_ref[...] * xs
        y = y * _silu(z_ref[b])
        yield
        outs = []
        for g in range(SSD_G):
            yg = y[:, g * SSD_GW:(g + 1) * SSD_GW]
            outs.append(yg * lax.rsqrt(jnp.mean(yg * yg, axis=-1, keepdims=True) + EPS))
        out_ref[b] = (jnp.concatenate(outs, axis=1) * nw_ref[...]).astype(out_ref.dtype)

    _interleave(chain(b) for b in range(out_ref.shape[0]))


def ssd_mixer(p, conv_w, conv_b, dt_bias, a_log, d_skip, norm_w):
    batch, seq_t, _ = p.shape
    nc = seq_t // CHUNK
    hp = jnp.zeros((8, SSD_DT_PAD), F32).at[0, :SSD_H].set(dt_bias).at[1, :SSD_H].set(a_log)
    dexp = jnp.repeat(d_skip, SSD_P).reshape(1, SSD_DI)
    conv_b = conv_b.reshape(1, -1)
    return pl.pallas_call(
        _ssd_kernel,
        grid=(nc,),
        in_specs=[
            pl.BlockSpec((batch, CHUNK, SSD_DI), lambda c: (0, c, 0)),
            pl.BlockSpec((batch, CHUNK, SSD_DI), lambda c: (0, c, 1)),
            pl.BlockSpec((batch, CHUNK, SSD_BC), lambda c: (0, c, 2 * SSD_DI // SSD_BC)),
            pl.BlockSpec((batch, CHUNK, SSD_DT_PAD), lambda c: (0, c, (2 * SSD_DI + SSD_BC) // SSD_DT_PAD)),
            pl.BlockSpec((SSD_K, SSD_DI), lambda c: (0, 0)),
            pl.BlockSpec((SSD_K, SSD_BC), lambda c: (0, SSD_DI // SSD_BC)),
            pl.BlockSpec((1, SSD_DI), lambda c: (0, 0)),
            pl.BlockSpec((1, SSD_BC), lambda c: (0, SSD_DI // SSD_BC)),
            pl.BlockSpec((8, SSD_DT_PAD), lambda c: (0, 0)),
            pl.BlockSpec((1, SSD_DI), lambda c: (0, 0)),
            pl.BlockSpec((1, SSD_DI), lambda c: (0, 0)),
        ],
        out_specs=pl.BlockSpec((batch, CHUNK, SSD_DI), lambda c: (0, c, 0)),
        out_shape=jax.ShapeDtypeStruct((batch, seq_t, SSD_DI), BF16),
        scratch_shapes=[pltpu.VMEM((batch, SSD_G, SSD_GW, SSD_N), F32), pltpu.VMEM((batch, 8, SSD_DI), F32),
                        pltpu.VMEM((batch, 8, SSD_BC), F32)],
        compiler_params=_params(("arbitrary",)),
        name="ssd",
    )(p, p, p, p, conv_w, conv_w, conv_b, conv_b, hp, dexp, norm_w.reshape(1, SSD_DI))


MOE_TILE = 512
ROUTE_W1, ROUTE_W2, ROUTE_I1, ROUTE_I2 = 0, 1, 2, 3


def _router_kernel(x_ref, w_ref, r_ref, o_ref):
    x = x_ref[...]
    hn = x * lax.rsqrt(jnp.mean(x * x, axis=-1, keepdims=True) + EPS) * w_ref[...]
    logits = _dot(hn, r_ref[...], HI)
    lane = _iota(logits.shape, 1)
    lane_f = lane.astype(F32)
    logits = jnp.where(lane < N_EXPERTS, logits, -jnp.inf)
    m1 = jnp.max(logits, axis=-1, keepdims=True)
    i1 = jnp.min(jnp.where(logits == m1, lane_f, float(ROUTER_PAD)), axis=-1, keepdims=True)
    rest = jnp.where(lane_f == i1, -jnp.inf, logits)
    m2 = jnp.max(rest, axis=-1, keepdims=True)
    i2 = jnp.min(jnp.where(rest == m2, lane_f, float(ROUTER_PAD)), axis=-1, keepdims=True)
    e2 = jnp.exp(m2 - m1)
    w1 = 1.0 / (1.0 + e2)
    w2 = e2 / (1.0 + e2)
    o_ref[...] = jnp.where(lane == ROUTE_W1, w1, jnp.where(lane == ROUTE_W2, w2,
                           jnp.where(lane == ROUTE_I1, i1, jnp.where(lane == ROUTE_I2, i2, 0.0))))


def router_top2(x, w, router):
    m, d = x.shape
    tm = _tile(m, (520, 512, 256, 128, 64))
    rpad = jnp.zeros((d, ROUTER_PAD), F32).at[:, :N_EXPERTS].set(router)
    return pl.pallas_call(
        _router_kernel,
        grid=(m // tm,),
        in_specs=[pl.BlockSpec((tm, d), lambda i: (i, 0)), pl.BlockSpec((1, d), lambda i: (0, 0)),
                  pl.BlockSpec((d, ROUTER_PAD), lambda i: (0, 0))],
        out_specs=pl.BlockSpec((tm, ROUTER_PAD), lambda i: (i, 0)),
        out_shape=jax.ShapeDtypeStruct((m, ROUTER_PAD), F32),
        compiler_params=_params(("arbitrary",)),
        name="router_top2",
    )(x, w.reshape(1, d), rpad)


def _row_copy(src_hbm, row, dst, r, sem):
    return pltpu.make_async_copy(src_hbm.at[pl.ds(row, 1)], dst.at[pl.ds(r, 1)], sem)


def _gather_norm_kernel(nused_ref, idx_ref, h_hbm, w_ref, o_ref, buf, sem):
    t = pl.program_id(0)
    tg = buf.shape[0]

    @pl.when(t < nused_ref[0])
    def _():
        def start(r, carry):
            _row_copy(h_hbm, idx_ref[t * tg + r], buf, r, sem).start()
            return carry

        def wait(r, carry):
            _row_copy(h_hbm, 0, buf, r, sem).wait()
            return carry

        lax.fori_loop(0, tg, start, 0, unroll=8)
        lax.fori_loop(0, tg, wait, 0, unroll=8)
        x = buf[...]
        hn = x * lax.rsqrt(jnp.mean(x * x, axis=-1, keepdims=True) + EPS) * w_ref[...]
        o_ref[...] = hn.astype(o_ref.dtype)

    @pl.when(t >= nused_ref[0])
    def _():
        o_ref[...] = jnp.zeros_like(o_ref)


def gather_norm(h, w, src, n_used, n_tiles):
    m, d = h.shape
    tg = MOE_TILE
    return pl.pallas_call(
        _gather_norm_kernel,
        grid_spec=pltpu.PrefetchScalarGridSpec(
            num_scalar_prefetch=2,
            grid=(n_tiles,),
            in_specs=[pl.BlockSpec(memory_space=pl.ANY), pl.BlockSpec((1, d), lambda t, *_: (0, 0))],
            out_specs=pl.BlockSpec((tg, d), lambda t, *_: (t, 0)),
            scratch_shapes=[pltpu.VMEM((tg, d), F32), pltpu.SemaphoreType.DMA(())],
        ),
        out_shape=jax.ShapeDtypeStruct((n_tiles * tg, d), BF16),
        compiler_params=_params(("arbitrary",)),
        name="moe_gather",
    )(n_used, src, h, w.reshape(1, d))


def _new_expert(te_ref, t):
    return jnp.logical_or(t == 0, te_ref[t] != te_ref[jnp.maximum(t - 1, 0)])


def _moe_up_kernel(te_ref, nused_ref, a_ref, wg_ref, wu_ref, o_ref, wgb_ref, wub_ref):
    t = pl.program_id(1)

    @pl.when(_new_expert(te_ref, t))
    def _():
        wgb_ref[...] = _b(wg_ref[...])
        wub_ref[...] = _b(wu_ref[...])

    @pl.when(t < nused_ref[0])
    def _():
        a = a_ref[...]
        o_ref[...] = (_silu(_dot(a, wgb_ref[...])) * _dot(a, wub_ref[...])).astype(o_ref.dtype)

    @pl.when(t >= nused_ref[0])
    def _():
        o_ref[...] = jnp.zeros_like(o_ref)


def moe_up(xs, wg, wu, tile_e, n_used):
    p, k = xs.shape
    f = wg.shape[-1]
    tg = MOE_TILE
    tn = _tile(f, (512, 256, 128))
    w_spec = pl.BlockSpec((None, k, tn), lambda j, t, te, nu: (te[t], 0, j))
    return pl.pallas_call(
        _moe_up_kernel,
        grid_spec=pltpu.PrefetchScalarGridSpec(
            num_scalar_prefetch=2,
            grid=(f // tn, p // tg),
            in_specs=[pl.BlockSpec((tg, k), lambda j, t, te, nu: (t, 0)), w_spec, w_spec],
            out_specs=pl.BlockSpec((tg, tn), lambda j, t, te, nu: (t, j)),
            scratch_shapes=[pltpu.VMEM((k, tn), BF16), pltpu.VMEM((k, tn), BF16)],
        ),
        out_shape=jax.ShapeDtypeStruct((p, f), BF16),
        compiler_params=_params(("arbitrary", "arbitrary")),
        name="moe_up",
    )(tile_e, n_used, xs, wg, wu)


def _moe_down_kernel(te_ref, nused_ref, a_ref, w_ref, o_ref, wb_ref):
    t = pl.program_id(1)

    @pl.when(_new_expert(te_ref, t))
    def _():
        wb_ref[...] = _b(w_ref[...])

    @pl.when(t < nused_ref[0])
    def _():
        o_ref[...] = _dot(a_ref[...], wb_ref[...])

    @pl.when(t >= nused_ref[0])
    def _():
        o_ref[...] = jnp.zeros_like(o_ref)


def moe_down(fs, wd, tile_e, n_used):
    p, f = fs.shape
    d = wd.shape[-1]
    tg = MOE_TILE
    tn = _tile(d, (512, 256, 128))
    return pl.pallas_call(
        _moe_down_kernel,
        grid_spec=pltpu.PrefetchScalarGridSpec(
            num_scalar_prefetch=2,
            grid=(d // tn, p // tg),
            in_specs=[pl.BlockSpec((tg, f), lambda j, t, te, nu: (t, 0)),
                      pl.BlockSpec((None, f, tn), lambda j, t, te, nu: (te[t], 0, j))],
            out_specs=pl.BlockSpec((tg, tn), lambda j, t, te, nu: (t, j)),
            scratch_shapes=[pltpu.VMEM((f, tn), BF16)],
        ),
        out_shape=jax.ShapeDtypeStruct((p, d), F32),
        compiler_params=_params(("arbitrary", "arbitrary")),
        name="moe_down",
    )(tile_e, n_used, fs, wd)


def _combine_norm_kernel(pos_ref, h_ref, route_ref, y_hbm, w_ref, o_ref, buf0, buf1, sem, *, seq_t, m):
    b = pl.program_id(0)
    i = pl.program_id(1)
    tm = buf0.shape[0]
    base = b * seq_t + (N_PAD + N_META) + i * tm

    def start(r, carry):
        _row_copy(y_hbm, pos_ref[base + r], buf0, r, sem).start()
        _row_copy(y_hbm, pos_ref[m + base + r], buf1, r, sem).start()
        return carry

    def wait(r, carry):
        _row_copy(y_hbm, 0, buf0, r, sem).wait()
        _row_copy(y_hbm, 0, buf1, r, sem).wait()
        return carry

    lax.fori_loop(0, tm, start, 0, unroll=8)
    lax.fori_loop(0, tm, wait, 0, unroll=8)
    w1 = route_ref[:, ROUTE_W1:ROUTE_W1 + 1]
    w2 = route_ref[:, ROUTE_W2:ROUTE_W2 + 1]
    x = h_ref[...] + (w1 * buf0[...] + w2 * buf1[...])
    o_ref[...] = (x * lax.rsqrt(jnp.mean(x * x, axis=-1, keepdims=True) + EPS) * w_ref[...]).astype(o_ref.dtype)


def combine_norm(h, route, ys, pos, w, batch, seq_t):
    m, d = h.shape
    seq = seq_t - N_PAD - N_META
    tm = CHUNK
    lead = (N_PAD + N_META) // tm
    nct = seq_t // tm
    row = lambda b, i, *_: (b * nct + lead + i, 0)
    return pl.pallas_call(
        functools.partial(_combine_norm_kernel, seq_t=seq_t, m=m),
        grid_spec=pltpu.PrefetchScalarGridSpec(
            num_scalar_prefetch=1,
            grid=(batch, seq // tm),
            in_specs=[pl.BlockSpec((tm, d), row), pl.BlockSpec((tm, ROUTER_PAD), row),
                      pl.BlockSpec(memory_space=pl.ANY), pl.BlockSpec((1, d), lambda b, i, *_: (0, 0))],
            out_specs=pl.BlockSpec((tm, d), lambda b, i, *_: (b * (seq // tm) + i, 0)),
            scratch_shapes=[pltpu.VMEM((tm, d), F32), pltpu.VMEM((tm, d), F32), pltpu.SemaphoreType.DMA(())],
        ),
        out_shape=jax.ShapeDtypeStruct((batch * seq, d), F32),
        compiler_params=_params(("arbitrary", "arbitrary")),
        name="moe_combine",
    )(pos, h, route, ys, w.reshape(1, d))


def moe_routing(route, m):
    tg = MOE_TILE
    n_tiles = (2 * m + tg - 1) // tg + N_EXPERTS
    e_flat = jnp.concatenate([route[:, ROUTE_I1], route[:, ROUTE_I2]]).astype(jnp.int32)
    onehot = (e_flat[:, None] == jnp.arange(N_EXPERTS, dtype=jnp.int32)[None, :]).astype(jnp.int32)
    csum = jnp.cumsum(onehot, axis=0)
    rank = jnp.sum(csum * onehot, axis=1) - 1
    counts = csum[-1]
    padded = ((counts + tg - 1) // tg) * tg
    ends = jnp.cumsum(padded)
    starts = ends - padded
    pos = jnp.sum(starts[None, :] * onehot, axis=1) + rank
    tok = jnp.concatenate([jnp.arange(m, dtype=jnp.int32)] * 2)
    src = jnp.zeros((n_tiles * tg,), jnp.int32).at[pos].set(tok)
    n_used = (ends[-1] // tg).astype(jnp.int32)
    tile_start = jnp.arange(n_tiles, dtype=jnp.int32) * tg
    tile_e = jnp.sum((tile_start[:, None] >= ends[None, :]).astype(jnp.int32), axis=1)
    last_e = jnp.sum((((n_used - 1) * tg) >= ends).astype(jnp.int32))
    tile_e = jnp.where(jnp.arange(n_tiles) < n_used, tile_e, last_e)
    return src, pos, tile_e, n_used.reshape(1), n_tiles


def kernel(x, meta_tokens, e_norm_mix, e_w_in, ml_i_bias, ml_f_bias, ml_out_norm, rw_mu, rw_w0, rw_w_up, rw_a0, rw_a_up, rw_g_up, rw_k_k, rw_k_a, rw_r_k, rw_ln_w, rw_ln_b, e_w_out, e_norm_ffn, ffn_w_gate, ffn_w_up, ffn_w_down, o_norm_mix, ssd_w_in, ssd_conv_w, ssd_conv_b, ssd_dt_bias, ssd_A_log, ssd_D, ssd_norm, ssd_w_out, o_norm_ffn, moe_router, moe_w_gate, moe_w_up, moe_w_down, final_norm):
    batch, seq, d = x.shape
    seq_t = N_PAD + N_META + seq
    m = batch * seq_t
    h = jnp.concatenate([jnp.zeros((batch, N_PAD, d), x.dtype),
                         jnp.broadcast_to(meta_tokens.astype(x.dtype), (batch, N_META, d)), x], axis=1)
    h = h.reshape(m, d)

    hn = rmsnorm(h, e_norm_mix[0], BF16)
    p = mm_up(hn, e_w_in[0], F32, 512).reshape(batch, seq_t, -1)
    y_ml = mlstm_group(p, ml_i_bias[0], ml_f_bias[0], ml_out_norm[0]).reshape(m, ML_V)
    y_rw = rwkv7_group(p, rw_mu[0], rw_w0[0], rw_w_up[0], rw_a0[0], rw_a_up[0], rw_g_up[0], rw_k_k[0],
                       rw_k_a[0], rw_r_k[0], rw_ln_w[0], rw_ln_b[0]).reshape(m, RW_DIM)
    h = mm_down([y_ml, y_rw], _b(e_w_out[0]), h, seq_t)
    hn = rmsnorm(h, e_norm_ffn[0], BF16)
    f = swiglu_up(hn, ffn_w_gate[0], ffn_w_up[0])
    h = mm_down([f], _b(ffn_w_down[0]), h, seq_t)

    hn = rmsnorm(h, o_norm_mix[0], BF16)
    p = mm_up(hn, ssd_w_in[0], F32, 1152).reshape(batch, seq_t, -1)
    y = ssd_mixer(p, ssd_conv_w[0], ssd_conv_b[0], ssd_dt_bias[0], ssd_A_log[0], ssd_D[0], ssd_norm[0])
    h = mm_down([y.reshape(m, SSD_DI)], _b(ssd_w_out[0]), h, seq_t)
    route = router_top2(h, o_norm_ffn[0], moe_router[0])
    src, pos, tile_e, n_used, n_tiles = moe_routing(route, m)
    xs = gather_norm(h, o_norm_ffn[0], src, n_used, n_tiles)
    fs = moe_up(xs, moe_w_gate[0], moe_w_up[0], tile_e, n_used)
    ys = moe_down(fs, moe_w_down[0], tile_e, n_used)
    out = combine_norm(h, route, ys, pos, final_norm, batch, seq_t)
    return out.reshape(batch, seq, d).astype(x.dtype)
```

```python
import functools

import jax
import jax.numpy as jnp
from jax import lax
from jax.experimental import pallas as pl
from jax.experimental.pallas import tpu as pltpu

F32 = jnp.float32
BF16 = jnp.bfloat16
HI = lax.Precision.HIGHEST

N_META = 16
CHUNK = 64
N_PAD = CHUNK - N_META
NEG = -1e30
EPS = 1e-6

ML_HEADS = 4
ML_DQK = 128
ML_DV = 256
ML_GATE_CAP = 15.0
ML_QK = ML_HEADS * ML_DQK
ML_V = ML_HEADS * ML_DV

RW_HEADS = 16
RW_N = 64
RW_DIM = RW_HEADS * RW_N
RW_DECAY_LORA = 64
RW_A_LORA = 64
RW_GATE_LORA = 160
RW_LN_EPS = 64e-5
RW_GROUP = 256
RW_HPG = RW_GROUP // RW_N
RW_GPS = 4
LORA_ROWS = 384

E_OFF_GATES = 2 * ML_QK + 2 * ML_V
E_OFF_RW = E_OFF_GATES + 2 * ML_HEADS
RW_LANE_OFF = E_OFF_RW % 128

SSD_DI = 4096
SSD_P = 64
SSD_H = SSD_DI // SSD_P
SSD_N = 128
SSD_G = 8
SSD_R = SSD_H // SSD_G
SSD_K = 4
SSD_GW = SSD_DI // SSD_G
SSD_BC = 2 * SSD_G * SSD_N
SSD_DT_PAD = 128

N_EXPERTS = 8
ROUTER_PAD = 128

VMEM_LIMIT = 56 * 1024 * 1024


def _tile(n, prefs):
    for p in prefs:
        if n % p == 0:
            return p
    return n


_ROW_TILES = (1040, 832, 640, 520, 512, 256, 128, 64)


def _params(sem):
    return pltpu.CompilerParams(dimension_semantics=sem, vmem_limit_bytes=VMEM_LIMIT)


def _dot(a, b, prec=None):
    return lax.dot_general(a, b, (((1,), (0,)), ((), ())), precision=prec, preferred_element_type=F32)


def _dot_nt(a, b, prec=None):
    return lax.dot_general(a, b, (((1,), (1,)), ((), ())), precision=prec, preferred_element_type=F32)


def _dot_tn(a, b, prec=None):
    return lax.dot_general(a, b, (((0,), (0,)), ((), ())), precision=prec, preferred_element_type=F32)


def _b(x):
    return x.astype(BF16)


def _bdot(dot, a, b):
    return dot(_b(a), _b(b))


def _dot_split(a, b):
    hi = _b(a)
    lo = _b(a - hi.astype(F32))
    bb = _b(b)
    return _dot(hi, bb) + _dot(lo, bb)


def _sigmoid(x):
    return 1.0 / (1.0 + jnp.exp(-x))


def _softplus(x):
    return jnp.maximum(x, 0.0) + jnp.log1p(jnp.exp(-jnp.abs(x)))


def _silu(x):
    return x * _sigmoid(x)


def _iota(shape, dim):
    return lax.broadcasted_iota(jnp.int32, shape, dim)


def _tri(n):
    return (_iota((n, n), 0) >= _iota((n, n), 1)).astype(F32)


def _interleave(chains):
    chains = list(chains)
    while chains:
        for ch in list(chains):
            try:
                next(ch)
            except StopIteration:
                chains.remove(ch)


def _rmsnorm_kernel(x_ref, w_ref, o_ref):
    x = x_ref[...]
    ms = jnp.mean(x * x, axis=-1, keepdims=True)
    o_ref[...] = (x * lax.rsqrt(ms + EPS) * w_ref[...]).astype(o_ref.dtype)


def rmsnorm(x, w, out_dtype):
    m, d = x.shape
    tm = _tile(m, _ROW_TILES)
    return pl.pallas_call(
        _rmsnorm_kernel,
        grid=(m // tm,),
        in_specs=[pl.BlockSpec((tm, d), lambda i: (i, 0)), pl.BlockSpec((1, d), lambda i: (0, 0))],
        out_specs=pl.BlockSpec((tm, d), lambda i: (i, 0)),
        out_shape=jax.ShapeDtypeStruct((m, d), out_dtype),
        compiler_params=_params(("arbitrary",)),
        name="rmsnorm",
    )(x, w.reshape(1, d))


def _mm_up_kernel(a_ref, wt_ref, o_ref, wb_ref, *, n_valid):
    @pl.when(pl.program_id(1) == 0)
    def _():
        tn = wt_ref.shape[0]
        row = pl.program_id(0) * tn + _iota((tn, 1), 0)
        wt = jnp.where(row < n_valid, wt_ref[...], 0.0)
        wb_ref[...] = _b(wt.T)

    o_ref[...] = _dot(a_ref[...], wb_ref[...]).astype(o_ref.dtype)


def mm_up(a, wt, out_dtype, tn):
    m, k = a.shape
    n = wt.shape[0]
    tm = _tile(m, _ROW_TILES)
    nt = pl.cdiv(n, tn)
    return pl.pallas_call(
        functools.partial(_mm_up_kernel, n_valid=n),
        grid=(nt, m // tm),
        in_specs=[pl.BlockSpec((tm, k), lambda j, i: (i, 0)), pl.BlockSpec((tn, k), lambda j, i: (j, 0))],
        out_specs=pl.BlockSpec((tm, tn), lambda j, i: (i, j)),
        out_shape=jax.ShapeDtypeStruct((m, nt * tn), out_dtype),
        scratch_shapes=[pltpu.VMEM((k, tn), BF16)],
        compiler_params=_params(("arbitrary", "arbitrary")),
        name="mm_up",
    )(a, wt)


def _swiglu_kernel(a_ref, wg_ref, wu_ref, o_ref, wgb_ref, wub_ref):
    @pl.when(pl.program_id(1) == 0)
    def _():
        wgb_ref[...] = _b(wg_ref[...])
        wub_ref[...] = _b(wu_ref[...])

    a = a_ref[...]
    o_ref[...] = (_silu(_dot(a, wgb_ref[...])) * _dot(a, wub_ref[...])).astype(o_ref.dtype)


def swiglu_up(a, wg, wu):
    m, k = a.shape
    f = wg.shape[-1]
    tm = _tile(m, _ROW_TILES)
    tn = _tile(f, (512, 256, 128))
    w_spec = pl.BlockSpec((k, tn), lambda j, i: (0, j))
    return pl.pallas_call(
        _swiglu_kernel,
        grid=(f // tn, m // tm),
        in_specs=[pl.BlockSpec((tm, k), lambda j, i: (i, 0)), w_spec, w_spec],
        out_specs=pl.BlockSpec((tm, tn), lambda j, i: (i, j)),
        out_shape=jax.ShapeDtypeStruct((m, f), BF16),
        scratch_shapes=[pltpu.VMEM((k, tn), BF16), pltpu.VMEM((k, tn), BF16)],
        compiler_params=_params(("arbitrary", "arbitrary")),
        name="swiglu_up",
    )(a, wg, wu)


def _mm_down_kernel(*refs, tm, seq_t):
    *a_refs, w_ref, r_ref, o_ref = refs
    acc = None
    off = 0
    for a_ref in a_refs:
        k = a_ref.shape[1]
        part = _dot(a_ref[...], w_ref[off:off + k, :])
        acc = part if acc is None else acc + part
        off += k
    t0 = (pl.program_id(0) * tm) % seq_t
    valid = ((t0 + _iota((tm, 1), 0)) >= N_PAD).astype(F32)
    o_ref[...] = r_ref[...] + acc * valid


def mm_down(a_parts, w, res, seq_t):
    m = a_parts[0].shape[0]
    k, n = w.shape
    assert sum(a.shape[1] for a in a_parts) == k
    tm = _tile(seq_t, (520, 512, 256, 128, 64))
    tn = _tile(n, (512, 256, 128))
    a_specs = [pl.BlockSpec((tm, a.shape[1]), lambda i, j: (i, 0)) for a in a_parts]
    return pl.pallas_call(
        functools.partial(_mm_down_kernel, tm=tm, seq_t=seq_t),
        grid=(m // tm, n // tn),
        in_specs=a_specs + [pl.BlockSpec((k, tn), lambda i, j: (0, j)), pl.BlockSpec((tm, tn), lambda i, j: (i, j))],
        out_specs=pl.BlockSpec((tm, tn), lambda i, j: (i, j)),
        out_shape=jax.ShapeDtypeStruct((m, n), F32),
        compiler_params=_params(("arbitrary", "arbitrary")),
        name="mm_down",
    )(*a_parts, w, res)


def _log_sigmoid(x):
    return -_softplus(-x)


def _mlstm_kernel(q_ref, k_ref, v_ref, o_ref, g_ref, bias_ref, onorm_ref, out_ref, c_ref, n_ref, m_ref):
    c = pl.program_id(0)
    L = CHUNK

    @pl.when(c == 0)
    def _():
        c_ref[...] = jnp.zeros_like(c_ref)
        n_ref[...] = jnp.zeros_like(n_ref)
        m_ref[...] = jnp.zeros_like(m_ref)

    valid = (c * L + _iota((L, 1), 0)) >= N_PAD
    causal = _iota((L, L), 0) >= _iota((L, L), 1)
    tri = _tri(L)
    scale = ML_DQK ** -0.5

    def chain(b, h, bcum, ig, b_t, i_t):
        q = q_ref[b, :, h * ML_DQK:(h + 1) * ML_DQK] * scale
        k = k_ref[b, :, h * ML_DQK:(h + 1) * ML_DQK]
        v = v_ref[b, :, h * ML_DV:(h + 1) * ML_DV]
        og = o_ref[b, :, h * ML_DV:(h + 1) * ML_DV]
        bcol = bcum[:, ML_HEADS + h:ML_HEADS + h + 1]
        icol = ig[:, h:h + 1]
        brow = b_t[ML_HEADS + h:ML_HEADS + h + 1, :]
        irow = i_t[h:h + 1, :]
        g = bcol[L - 1:L, :]
        m_prev = m_ref[b, h:h + 1, 0:1]
        c_prev = c_ref[b, h]
        n_prev = n_ref[b, h:h + 1, :]

        dm = jnp.where(causal, bcol - brow + irow, NEG)
        m_inter = bcol + m_prev
        m_t = jnp.maximum(m_inter, jnp.max(dm, axis=-1, keepdims=True))
        yield
        s = _dot_nt(_b(q), _b(k)) * jnp.exp(dm - m_t)
        w_inter = jnp.exp(m_inter - m_t)
        yield
        num = _dot(_b(s), _b(v)) + w_inter * _dot(_b(q), _b(c_prev))
        den = jnp.sum(s, axis=-1, keepdims=True) + w_inter * jnp.sum(q * n_prev, axis=-1, keepdims=True)
        yield
        hh = num / jnp.maximum(jnp.abs(den), jnp.exp(-m_t))
        hh = hh * lax.rsqrt(jnp.mean(hh * hh, axis=-1, keepdims=True) + EPS)
        hh = hh * onorm_ref[:, h * ML_DV:(h + 1) * ML_DV] * _sigmoid(og)
        out_ref[b, :, h * ML_DV:(h + 1) * ML_DV] = hh.astype(out_ref.dtype)
        yield

        a_col = g - bcol + icol
        m_new = jnp.maximum(g + m_prev, jnp.max(a_col, axis=0, keepdims=True))
        dec = jnp.exp(g + m_prev - m_new)
        wk = jnp.exp(a_col - m_new) * k
        yield
        c_ref[b, h] = dec * c_prev + _dot_tn(_b(wk), _b(v))
        n_ref[b, h:h + 1, :] = dec * n_prev + jnp.sum(wk, axis=0, keepdims=True)
        m_ref[b, h:h + 1, :] = jnp.broadcast_to(m_new, (1, m_ref.shape[2]))

    chains = []
    for b in range(q_ref.shape[0]):
        gates = g_ref[b] + bias_ref[...]
        sc = ML_GATE_CAP * jnp.tanh(gates / ML_GATE_CAP)
        ig = jnp.where(valid, sc, NEG)
        lf = jnp.where(valid, _log_sigmoid(sc), 0.0)
        bcum = _dot(tri, lf, HI)
        chains += [chain(b, h, bcum, ig, bcum.T, ig.T) for h in range(ML_HEADS)]
    _interleave(chains)


def mlstm_group(p, i_bias, f_bias, out_norm):
    batch, seq_t, _ = p.shape
    nc = seq_t // CHUNK
    bias = jnp.zeros((1, 128), F32).at[0, 0:ML_HEADS].set(i_bias).at[0, ML_HEADS:2 * ML_HEADS].set(f_bias)
    return pl.pallas_call(
        _mlstm_kernel,
        grid=(nc,),
        in_specs=[
            pl.BlockSpec((batch, CHUNK, ML_QK), lambda c: (0, c, 0)),
            pl.BlockSpec((batch, CHUNK, ML_QK), lambda c: (0, c, 1)),
            pl.BlockSpec((batch, CHUNK, ML_V), lambda c: (0, c, 1)),
            pl.BlockSpec((batch, CHUNK, ML_V), lambda c: (0, c, 2)),
            pl.BlockSpec((batch, CHUNK, 128), lambda c: (0, c, E_OFF_GATES // 128)),
            pl.BlockSpec((1, 128), lambda c: (0, 0)),
            pl.BlockSpec((1, ML_V), lambda c: (0, 0)),
        ],
        out_specs=pl.BlockSpec((batch, CHUNK, ML_V), lambda c: (0, c, 0)),
        out_shape=jax.ShapeDtypeStruct((batch, seq_t, ML_V), BF16),
        scratch_shapes=[pltpu.VMEM((batch, ML_HEADS, ML_DQK, ML_DV), F32), pltpu.VMEM((batch, 8, ML_DQK), F32),
                        pltpu.VMEM((batch, 8, 128), F32)],
        compiler_params=_params(("arbitrary",)),
        name="mlstm",
    )(p, p, p, p, p, bias, out_norm.reshape(1, ML_V))


def _rwkv_kernel(*refs):
    nb = 2 * RW_GPS + 1
    r_blk, k_blk, v_blk = refs[0:nb], refs[nb:2 * nb], refs[2 * nb:3 * nb]
    tail_ref, mu3_ref, mut_ref, chan_ref, wl_ref, out_ref, h_ref, cr_ref, ck_ref, cv_ref, ct_ref = refs[3 * nb:]
    c = pl.program_id(1)
    L = CHUNK
    G = RW_GROUP

    @pl.when(c == 0)
    def _():
        h_ref[...] = jnp.zeros_like(h_ref)
        cr_ref[...] = jnp.zeros_like(cr_ref)
        ck_ref[...] = jnp.zeros_like(ck_ref)
        cv_ref[...] = jnp.zeros_like(cv_ref)
        ct_ref[...] = jnp.zeros_like(ct_ref)

    row = _iota((L, 1), 0)
    valid = (c * L + row) >= N_PAD

    ri = _iota((G, G), 0)
    ci = _iota((G, G), 1)
    same = (ri >> 6) == (ci >> 6)
    tl = ri & (RW_N - 1)
    sl = ci & (RW_N - 1)
    bd_mask = same.astype(F32)
    eye = (ri == ci).astype(F32)
    lower = same & (tl >= sl)
    strict = same & (tl > sl)
    ones_bd = _b(bd_mask)
    tri = _tri(L)
    lane = _iota((L, LORA_ROWS), 1)

    def head_sum(x):
        hi = _b(x)
        return _dot(hi, ones_bd) + _dot(_b(x - hi.astype(F32)), ones_bd)

    def bd(x):
        return _b(jnp.concatenate([x] * RW_HPG, axis=0) * bd_mask)

    def shift_lerp(x, carry_ref, idx, mu):
        prev = carry_ref[idx]
        xs = jnp.where(row == 0, prev, pltpu.roll(x, 1, 0))
        carry_ref[idx] = x[L - 1:L, :]
        return x + mu * (xs - x)

    def unshift(blocks, b, j):
        x = jnp.concatenate([blk[b] for blk in blocks[2 * j:2 * j + 3]], axis=1)
        return x[:, RW_LANE_OFF:RW_LANE_OFF + G]

    def chain(b, j):
        gl = slice(j * G, (j + 1) * G)
        w0 = chan_ref[0:1, gl]
        a0 = chan_ref[1:2, gl]
        k_k = chan_ref[2:3, gl]
        k_a = chan_ref[3:4, gl]
        r_k = chan_ref[4:5, gl]
        ln_w = chan_ref[5:6, gl]
        ln_b = chan_ref[6:7, gl]
        r = shift_lerp(unshift(r_blk, b, j), cr_ref, (b, j), mu3_ref[0:1, gl])
        k = shift_lerp(unshift(k_blk, b, j), ck_ref, (b, j), mu3_ref[1:2, gl])
        v = shift_lerp(unshift(v_blk, b, j), cv_ref, (b, j), mu3_ref[2:3, gl])
        tail = shift_lerp(tail_ref[b], ct_ref, (b, j), mut_ref[...])

        lo1 = RW_LANE_OFF + RW_DECAY_LORA
        lo2 = lo1 + RW_A_LORA
        lo3 = lo2 + RW_GATE_LORA
        act = jnp.where(lane < RW_LANE_OFF, 0.0,
                        jnp.where(lane < lo1, jnp.tanh(tail),
                                  jnp.where(lane < lo2, tail, jnp.where(lane < lo3, _sigmoid(tail), 0.0))))
        lw = _dot(act[:, 0:128], wl_ref[0, 0:128, gl], HI)
        la = _bdot(_dot, act[:, 0:256], wl_ref[1, 0:256, gl])
        gate = _bdot(_dot, act[:, 128:384], wl_ref[2, 128:384, gl])
        yield

        w_log = -_softplus(-(w0 + lw)) - 0.5
        ld = jnp.where(valid, -jnp.exp(w_log), 0.0)
        a = _sigmoid(a0 + la)
        kk = k * k_k
        kk = kk * lax.rsqrt(jnp.maximum(head_sum(kk * kk), 1e-24))
        k2 = k * (1.0 + (a - 1.0) * k_a)
        k2 = jnp.where(valid, k2, 0.0)
        kk = jnp.where(valid, kk, 0.0)
        va = -kk
        vb = kk * a

        yield
        cum = _dot(tri, ld, HI)
        cum_l = cum[L - 1:L, :]
        w_t = jnp.exp(cum)
        w_inv = jnp.exp(-cum)
        w_prev = jnp.exp(cum - ld)
        w_end = jnp.exp(cum_l - cum)
        w_last = jnp.exp(cum_l)

        a_bd = bd(va * w_prev)
        r_bd = bd(r * w_t)
        b_bd = bd(vb * w_inv)
        k_bd = bd(k2 * w_inv)
        bl_bd = bd(vb * w_end)
        kl_bd = bd(k2 * w_end)
        v_bd = bd(v)
        yield

        n_ab = jnp.where(strict, _dot_nt(a_bd, b_bd), 0.0)
        a_ak = _b(jnp.where(strict, _dot_nt(a_bd, k_bd), 0.0))
        m_rb = _b(jnp.where(lower, _dot_nt(r_bd, b_bd), 0.0))
        m_rk = _b(jnp.where(lower, _dot_nt(r_bd, k_bd), 0.0))
        yield

        t_inv = eye + n_ab
        pw = n_ab
        for _ in range(5):
            pw_b = _b(pw)
            pw = _dot(pw_b, pw_b)
            yield
            t_inv = t_inv + _bdot(_dot, t_inv, pw)
            yield

        t_b = _b(t_inv)
        av = _dot(a_ak, v_bd)
        yield
        a_p = _b(_dot(t_b, a_bd))
        u0 = _b(_dot(t_b, _b(av)))
        yield
        p_mat = eye * w_last + _dot_tn(bl_bd, a_p)
        q_mat = _dot_tn(bl_bd, u0) + _dot_tn(kl_bd, v_bd)
        r_p = r_bd.astype(F32) + _dot(m_rb, a_p)
        y0 = _dot(m_rb, u0) + _dot(m_rk, v_bd)
        yield

        h0 = _b(h_ref[b, j])
        y_bd = _dot(_b(r_p), h0) + y0
        h_ref[b, j] = _dot(_b(p_mat), h0) + q_mat
        y = y_bd[0:L] + y_bd[L:2 * L] + y_bd[2 * L:3 * L] + y_bd[3 * L:4 * L]
        yield

        inv_n = 1.0 / RW_N
        mean = head_sum(y) * inv_n
        d = y - mean
        var = head_sum(d * d) * inv_n
        yn = d * lax.rsqrt(var + RW_LN_EPS) * ln_w + ln_b
        bonus = head_sum(r * k2 * r_k) * v
        out_ref[b, :, gl] = ((yn + bonus) * gate).astype(out_ref.dtype)

    _interleave(chain(b, j) for b in range(out_ref.shape[0]) for j in range(RW_GPS))


def rwkv7_group(p, mu, w0, w_up, a0, a_up, g_up, k_k, k_a, r_k, ln_w, ln_b):
    batch, seq_t, _ = p.shape
    nc = seq_t // CHUNK
    ng = RW_DIM // RW_GROUP
    n_lora = RW_DECAY_LORA + RW_A_LORA + RW_GATE_LORA
    o = RW_LANE_OFF
    mu3 = mu[:3 * RW_DIM].reshape(3, RW_DIM)
    mut = jnp.zeros((1, LORA_ROWS), F32).at[0, o:o + n_lora].set(mu[3 * RW_DIM:])
    chan = jnp.stack([w0, a0, k_k, k_a, r_k.reshape(RW_DIM), ln_w, ln_b, jnp.zeros((RW_DIM,), F32)])
    wl = jnp.zeros((3, LORA_ROWS, RW_DIM), F32)
    wl = wl.at[0, o:o + RW_DECAY_LORA].set(w_up)
    wl = wl.at[1, o + RW_DECAY_LORA:o + RW_DECAY_LORA + RW_A_LORA].set(a_up)
    wl = wl.at[2, o + RW_DECAY_LORA + RW_A_LORA:o + n_lora].set(g_up)

    gw = RW_GPS * RW_GROUP

    def col(seg, part):
        first = (E_OFF_RW + seg * RW_DIM - o) // 128
        return pl.BlockSpec((batch, CHUNK, 128), lambda g, c: (0, c, first + 2 * RW_GPS * g + part))

    nb = 2 * RW_GPS + 1
    cols = [col(seg, part) for seg in range(3) for part in range(nb)]
    carry = lambda width: pltpu.VMEM((batch, RW_GPS, 1, width), F32)
    return pl.pallas_call(
        _rwkv_kernel,
        grid=(ng // RW_GPS, nc),
        in_specs=cols + [
            pl.BlockSpec((batch, CHUNK, LORA_ROWS), lambda g, c: (0, c, (E_OFF_RW + 3 * RW_DIM - o) // LORA_ROWS)),
            pl.BlockSpec((3, gw), lambda g, c: (0, g)),
            pl.BlockSpec((1, LORA_ROWS), lambda g, c: (0, 0)),
            pl.BlockSpec((8, gw), lambda g, c: (0, g)),
            pl.BlockSpec((3, LORA_ROWS, gw), lambda g, c: (0, 0, g)),
        ],
        out_specs=pl.BlockSpec((batch, CHUNK, gw), lambda g, c: (0, c, g)),
        out_shape=jax.ShapeDtypeStruct((batch, seq_t, RW_DIM), BF16),
        scratch_shapes=[pltpu.VMEM((batch, RW_GPS, RW_GROUP, RW_GROUP), F32), carry(RW_GROUP), carry(RW_GROUP),
                        carry(RW_GROUP), carry(LORA_ROWS)],
        compiler_params=_params(("arbitrary", "arbitrary")),
        name="rwkv7",
    )(*([p] * (3 * nb + 1)), mu3, mut, chan, wl)


def _ssd_kernel(z_ref, xs_ref, bc_ref, dt_ref, cwx_ref, cwb_ref, cbx_ref, cbb_ref, hp_ref, dexp_ref, nw_ref,
                out_ref, s_ref, px_ref, pb_ref):
    c = pl.program_id(0)
    L = CHUNK

    @pl.when(c == 0)
    def _():
        s_ref[...] = jnp.zeros_like(s_ref)
        px_ref[...] = jnp.zeros_like(px_ref)
        pb_ref[...] = jnp.zeros_like(pb_ref)

    row8 = _iota((8, 1), 0)
    valid = (c * L + _iota((L, 1), 0)) >= N_PAD
    lane = _iota((L, 2 * SSD_P), 1)
    causal = _iota((L, L), 0) >= _iota((L, L), 1)
    tri = _tri(L)

    def conv_silu(x, prev_ref, b, w_ref, b_ref):
        prev = prev_ref[b]
        acc = x * w_ref[SSD_K - 1:SSD_K, :] + b_ref[...]
        for j in range(1, SSD_K):
            xr = pltpu.roll(x, j, 0)
            pr = pltpu.roll(prev, j, 0)
            top = jnp.where(row8 < j, pr, xr[0:8])
            sh = jnp.concatenate([top, xr[8:]], axis=0)
            acc = acc + sh * w_ref[SSD_K - 1 - j:SSD_K - j, :]
        prev_ref[b] = x[L - 8:L, :]
        return _silu(acc)

    def expand(x):
        parts = []
        for j in range(SSD_H // 2):
            lo = jnp.broadcast_to(x[:, 2 * j:2 * j + 1], (L, 2 * SSD_P))
            hi = jnp.broadcast_to(x[:, 2 * j + 1:2 * j + 2], (L, 2 * SSD_P))
            parts.append(jnp.where(lane < SSD_P, lo, hi))
        return jnp.concatenate(parts, axis=1)

    def chain(b):
        xs = conv_silu(xs_ref[b], px_ref, b, cwx_ref, cbx_ref)
        yield
        bc = conv_silu(bc_ref[b], pb_ref, b, cwb_ref, cbb_ref)
        yield

        dt = jnp.where(valid, _softplus(dt_ref[b] + hp_ref[0:1, :]), 0.0)
        a = dt * (-jnp.exp(hp_ref[1:2, :]))
        acum = _dot(tri, a, HI)
        a_t = acum.T
        dt_t = dt.T
        a_last = acum[L - 1:L, :]
        cdec = jnp.exp(a_last)
        yield
        xw = xs * expand(dt * jnp.exp(a_last - acum))
        yield
        e_ea = expand(jnp.exp(acum))
        yield

        ys = []
        for g in range(SSD_G):
            bg = bc[:, g * SSD_N:(g + 1) * SSD_N]
            cg = bc[:, SSD_G * SSD_N + g * SSD_N:SSD_G * SSD_N + (g + 1) * SSD_N]
            cb = _dot_nt(_b(cg), _b(bg))
            s_prev = s_ref[b, g]
            y_inter = _dot_nt(_b(cg), _b(s_prev)) * e_ea[:, g * SSD_GW:(g + 1) * SSD_GW]
            yield
            parts = []
            cd_rows = []
            for r in range(SSD_R):
                h = g * SSD_R + r
                seg = acum[:, h:h + 1] - a_t[h:h + 1, :]
                ldec = jnp.exp(jnp.where(causal, seg, NEG))
                parts.append(_dot(_b(cb * ldec * dt_t[h:h + 1, :]), _b(xs[:, h * SSD_P:(h + 1) * SSD_P])))
                cd_rows.append(jnp.broadcast_to(cdec[:, h:h + 1], (SSD_P, SSD_N)))
                if r % 2:
                    yield
            ys.append(jnp.concatenate(parts, axis=1) + y_inter)
            st = _dot_tn(_b(xw[:, g * SSD_GW:(g + 1) * SSD_GW]), _b(bg))
            s_ref[b, g] = jnp.concatenate(cd_rows, axis=0) * s_prev + st
            yield

        y = jnp.concatenate(ys, axis=1) + dexp_ref[...] * xs
        y = y * _silu(z_ref[b])
        yield
        outs = []
        for g in range(SSD_G):
            yg = y[:, g * SSD_GW:(g + 1) * SSD_GW]
            outs.append(yg * lax.rsqrt(jnp.mean(yg * yg, axis=-1, keepdims=True) + EPS))
        out_ref[b] = (jnp.concatenate(outs, axis=1) * nw_ref[...]).astype(out_ref.dtype)

    _interleave(chain(b) for b in range(out_ref.shape[0]))


def ssd_mixer(p, conv_w, conv_b, dt_bias, a_log, d_skip, norm_w):
    batch, seq_t, _ = p.shape
    nc = seq_t // CHUNK
    hp = jnp.zeros((8, SSD_DT_PAD), F32).at[0, :SSD_H].set(dt_bias).at[1, :SSD_H].set(a_log)
    dexp = jnp.repeat(d_skip, SSD_P).reshape(1, SSD_DI)
    conv_b = conv_b.reshape(1, -1)
    return pl.pallas_call(
        _ssd_kernel,
        grid=(nc,),
        in_specs=[
            pl.BlockSpec((batch, CHUNK, SSD_DI), lambda c: (0, c, 0)),
            pl.BlockSpec((batch, CHUNK, SSD_DI), lambda c: (0, c, 1)),
            pl.BlockSpec((batch, CHUNK, SSD_BC), lambda c: (0, c, 2 * SSD_DI // SSD_BC)),
            pl.BlockSpec((batch, CHUNK, SSD_DT_PAD), lambda c: (0, c, (2 * SSD_DI + SSD_BC) // SSD_DT_PAD)),
            pl.BlockSpec((SSD_K, SSD_DI), lambda c: (0, 0)),
            pl.BlockSpec((SSD_K, SSD_BC), lambda c: (0, SSD_DI // SSD_BC)),
            pl.BlockSpec((1, SSD_DI), lambda c: (0, 0)),
            pl.BlockSpec((1, SSD_BC), lambda c: (0, SSD_DI // SSD_BC)),
            pl.BlockSpec((8, SSD_DT_PAD), lambda c: (0, 0)),
            pl.BlockSpec((1, SSD_DI), lambda c: (0, 0)),
            pl.BlockSpec((1, SSD_DI), lambda c: (0, 0)),
        ],
        out_specs=pl.BlockSpec((batch, CHUNK, SSD_DI), lambda c: (0, c, 0)),
        out_shape=jax.ShapeDtypeStruct((batch, seq_t, SSD_DI), BF16),
        scratch_shapes=[pltpu.VMEM((batch, SSD_G, SSD_GW, SSD_N), F32), pltpu.VMEM((batch, 8, SSD_DI), F32),
                        pltpu.VMEM((batch, 8, SSD_BC), F32)],
        compiler_params=_params(("arbitrary",)),
        name="ssd",
    )(p, p, p, p, conv_w, conv_w, conv_b, conv_b, hp, dexp, norm_w.reshape(1, SSD_DI))


MOE_TILE = 512
ROUTE_W1, ROUTE_W2, ROUTE_I1, ROUTE_I2 = 0, 1, 2, 3


def _router_kernel(x_ref, w_ref, r_ref, o_ref):
    x = x_ref[...]
    hn = x * lax.rsqrt(jnp.mean(x * x, axis=-1, keepdims=True) + EPS) * w_ref[...]
    logits = _dot(hn, r_ref[...], HI)
    lane = _iota(logits.shape, 1)
    lane_f = lane.astype(F32)
    logits = jnp.where(lane < N_EXPERTS, logits, -jnp.inf)
    m1 = jnp.max(logits, axis=-1, keepdims=True)
    i1 = jnp.min(jnp.where(logits == m1, lane_f, float(ROUTER_PAD)), axis=-1, keepdims=True)
    rest = jnp.where(lane_f == i1, -jnp.inf, logits)
    m2 = jnp.max(rest, axis=-1, keepdims=True)
    i2 = jnp.min(jnp.where(rest == m2, lane_f, float(ROUTER_PAD)), axis=-1, keepdims=True)
    e2 = jnp.exp(m2 - m1)
    w1 = 1.0 / (1.0 + e2)
    w2 = e2 / (1.0 + e2)
    o_ref[...] = jnp.where(lane == ROUTE_W1, w1, jnp.where(lane == ROUTE_W2, w2,
                           jnp.where(lane == ROUTE_I1, i1, jnp.where(lane == ROUTE_I2, i2, 0.0))))


def router_top2(x, w, router):
    m, d = x.shape
    tm = _tile(m, (520, 512, 256, 128, 64))
    rpad = jnp.zeros((d, ROUTER_PAD), F32).at[:, :N_EXPERTS].set(router)
    return pl.pallas_call(
        _router_kernel,
        grid=(m // tm,),
        in_specs=[pl.BlockSpec((tm, d), lambda i: (i, 0)), pl.BlockSpec((1, d), lambda i: (0, 0)),
                  pl.BlockSpec((d, ROUTER_PAD), lambda i: (0, 0))],
        out_specs=pl.BlockSpec((tm, ROUTER_PAD), lambda i: (i, 0)),
        out_shape=jax.ShapeDtypeStruct((m, ROUTER_PAD), F32),
        compiler_params=_params(("arbitrary",)),
        name="router_top2",
    )(x, w.reshape(1, d), rpad)


def _row_copy(src_hbm, row, dst, r, sem):
    return pltpu.make_async_copy(src_hbm.at[pl.ds(row, 1)], dst.at[pl.ds(r, 1)], sem)


def _gather_norm_kernel(nused_ref, idx_ref, h_hbm, w_ref, o_ref, buf, sem):
    t = pl.program_id(0)
    tg = buf.shape[1]
    n_used = nused_ref[0]

    def issue(tile, slot):
        def body(r2, carry):
            for u in range(2):
                r = 2 * r2 + u
                _row_copy(h_hbm, idx_ref[tile * tg + r], buf.at[slot], r, sem.at[slot]).start(priority=u)
            return carry

        lax.fori_loop(0, tg // 2, body, 0, unroll=4)

    @pl.when(jnp.logical_and(t == 0, n_used > 0))
    def _():
        issue(0, 0)

    @pl.when(t + 1 < n_used)
    def _():
        issue(t + 1, (t + 1) % 2)

    @pl.when(t < n_used)
    def _():
        slot = t % 2
        pltpu.make_async_copy(h_hbm.at[pl.ds(0, tg)], buf.at[slot], sem.at[slot]).wait()
        x = buf[slot]
        hn = x * lax.rsqrt(jnp.mean(x * x, axis=-1, keepdims=True) + EPS) * w_ref[...]
        o_ref[...] = hn.astype(o_ref.dtype)

    @pl.when(t >= n_used)
    def _():
        o_ref[...] = jnp.zeros_like(o_ref)


def gather_norm(h, w, src, n_used, n_tiles):
    m, d = h.shape
    tg = MOE_TILE
    return pl.pallas_call(
        _gather_norm_kernel,
        grid_spec=pltpu.PrefetchScalarGridSpec(
            num_scalar_prefetch=2,
            grid=(n_tiles,),
            in_specs=[pl.BlockSpec(memory_space=pl.ANY), pl.BlockSpec((1, d), lambda t, *_: (0, 0))],
            out_specs=pl.BlockSpec((tg, d), lambda t, *_: (t, 0)),
            scratch_shapes=[pltpu.VMEM((2, tg, d), F32), pltpu.SemaphoreType.DMA((2,))],
        ),
        out_shape=jax.ShapeDtypeStruct((n_tiles * tg, d), BF16),
        compiler_params=_params(("arbitrary",)),
        name="moe_gather",
    )(n_used, src, h, w.reshape(1, d))


def _new_expert(te_ref, t):
    return jnp.logical_or(t == 0, te_ref[t] != te_ref[jnp.maximum(t - 1, 0)])


def _moe_up_kernel(te_ref, nused_ref, a_ref, wg_ref, wu_ref, o_ref, wgb_ref, wub_ref):
    t = pl.program_id(1)

    @pl.when(_new_expert(te_ref, t))
    def _():
        wgb_ref[...] = _b(wg_ref[...])
        wub_ref[...] = _b(wu_ref[...])

    @pl.when(t < nused_ref[0])
    def _():
        a = a_ref[...]
        o_ref[...] = (_silu(_dot(a, wgb_ref[...])) * _dot(a, wub_ref[...])).astype(o_ref.dtype)

    @pl.when(t >= nused_ref[0])
    def _():
        o_ref[...] = jnp.zeros_like(o_ref)


def moe_up(xs, wg, wu, tile_e, n_used):
    p, k = xs.shape
    f = wg.shape[-1]
    tg = MOE_TILE
    tn = _tile(f, (512, 256, 128))
    w_spec = pl.BlockSpec((None, k, tn), lambda j, t, te, nu: (te[t], 0, j))
    return pl.pallas_call(
        _moe_up_kernel,
        grid_spec=pltpu.PrefetchScalarGridSpec(
            num_scalar_prefetch=2,
            grid=(f // tn, p // tg),
            in_specs=[pl.BlockSpec((tg, k), lambda j, t, te, nu: (t, 0)), w_spec, w_spec],
            out_specs=pl.BlockSpec((tg, tn), lambda j, t, te, nu: (t, j)),
            scratch_shapes=[pltpu.VMEM((k, tn), BF16), pltpu.VMEM((k, tn), BF16)],
        ),
        out_shape=jax.ShapeDtypeStruct((p, f), BF16),
        compiler_params=_params(("arbitrary", "arbitrary")),
        name="moe_up",
    )(tile_e, n_used, xs, wg, wu)


def _moe_down_kernel(te_ref, nused_ref, a_ref, w_ref, o_ref, wb_ref):
    t = pl.program_id(1)

    @pl.when(_new_expert(te_ref, t))
    def _():
        wb_ref[...] = _b(w_ref[...])

    @pl.when(t < nused_ref[0])
    def _():
        o_ref[...] = _dot(a_ref[...], wb_ref[...])

    @pl.when(t >= nused_ref[0])
    def _():
        o_ref[...] = jnp.zeros_like(o_ref)


def moe_down(fs, wd, tile_e, n_used):
    p, f = fs.shape
    d = wd.shape[-1]
    tg = MOE_TILE
    tn = _tile(d, (512, 256, 128))
    return pl.pallas_call(
        _moe_down_kernel,
        grid_spec=pltpu.PrefetchScalarGridSpec(
            num_scalar_prefetch=2,
            grid=(d // tn, p // tg),
            in_specs=[pl.BlockSpec((tg, f), lambda j, t, te, nu: (t, 0)),
                      pl.BlockSpec((None, f, tn), lambda j, t, te, nu: (te[t], 0, j))],
            out_specs=pl.BlockSpec((tg, tn), lambda j, t, te, nu: (t, j)),
            scratch_shapes=[pltpu.VMEM((f, tn), BF16)],
        ),
        out_shape=jax.ShapeDtypeStruct((p, d), F32),
        compiler_params=_params(("arbitrary", "arbitrary")),
        name="moe_down",
    )(tile_e, n_used, fs, wd)


def _combine_norm_kernel(pos_ref, h_hbm, route_hbm, y_hbm, w_ref, o_ref, hbuf, rbuf, buf0, buf1, sem,
                         *, seq_t, m, tiles_per_seq):
    s = pl.program_id(0)
    tm = hbuf.shape[1]

    def first_row(step):
        return (step // tiles_per_seq) * seq_t + (N_PAD + N_META) + (step % tiles_per_seq) * tm

    def copies(base, slot):
        return (pltpu.make_async_copy(h_hbm.at[pl.ds(base, tm)], hbuf.at[slot], sem.at[slot, 0]),
                pltpu.make_async_copy(route_hbm.at[pl.ds(base, tm)], rbuf.at[slot], sem.at[slot, 1]))

    def issue(step, slot):
        base = first_row(step)
        for cp in copies(base, slot):
            cp.start()

        def body(r, carry):
            _row_copy(y_hbm, pos_ref[base + r], buf0.at[slot], r, sem.at[slot, 2]).start(priority=0)
            _row_copy(y_hbm, pos_ref[m + base + r], buf1.at[slot], r, sem.at[slot, 3]).start(priority=1)
            return carry

        lax.fori_loop(0, tm, body, 0, unroll=4)

    @pl.when(s == 0)
    def _():
        issue(0, 0)

    @pl.when(s + 1 < pl.num_programs(0))
    def _():
        issue(s + 1, (s + 1) % 2)

    slot = s % 2
    for cp in copies(0, slot):
        cp.wait()
    pltpu.make_async_copy(y_hbm.at[pl.ds(0, tm)], buf0.at[slot], sem.at[slot, 2]).wait()
    pltpu.make_async_copy(y_hbm.at[pl.ds(0, tm)], buf1.at[slot], sem.at[slot, 3]).wait()
    w1 = rbuf[slot, :, ROUTE_W1:ROUTE_W1 + 1]
    w2 = rbuf[slot, :, ROUTE_W2:ROUTE_W2 + 1]
    x = hbuf[slot] + (w1 * buf0[slot] + w2 * buf1[slot])
    o_ref[...] = (x * lax.rsqrt(jnp.mean(x * x, axis=-1, keepdims=True) + EPS) * w_ref[...]).astype(o_ref.dtype)


def combine_norm(h, route, ys, pos, w, batch, seq_t):
    m, d = h.shape
    seq = seq_t - N_PAD - N_META
    tm = _tile(seq, (256, 128, 64))
    any_spec = pl.BlockSpec(memory_space=pl.ANY)
    return pl.pallas_call(
        functools.partial(_combine_norm_kernel, seq_t=seq_t, m=m, tiles_per_seq=seq // tm),
        grid_spec=pltpu.PrefetchScalarGridSpec(
            num_scalar_prefetch=1,
            grid=(batch * seq // tm,),
            in_specs=[any_spec, any_spec, any_spec, pl.BlockSpec((1, d), lambda s, *_: (0, 0))],
            out_specs=pl.BlockSpec((tm, d), lambda s, *_: (s, 0)),
            scratch_shapes=[pltpu.VMEM((2, tm, d), F32), pltpu.VMEM((2, tm, ROUTER_PAD), F32),
                            pltpu.VMEM((2, tm, d), F32), pltpu.VMEM((2, tm, d), F32),
                            pltpu.SemaphoreType.DMA((2, 4))],
        ),
        out_shape=jax.ShapeDtypeStruct((batch * seq, d), F32),
        compiler_params=_params(("arbitrary",)),
        name="moe_combine",
    )(pos, h, route, ys, w.reshape(1, d))


def moe_routing(route, m):
    tg = MOE_TILE
    n_tiles = (2 * m + tg - 1) // tg + N_EXPERTS
    e_flat = jnp.concatenate([route[:, ROUTE_I1], route[:, ROUTE_I2]]).astype(jnp.int32)
    onehot = (e_flat[:, None] == jnp.arange(N_EXPERTS, dtype=jnp.int32)[None, :]).astype(jnp.int32)
    csum = jnp.cumsum(onehot, axis=0)
    rank = jnp.sum(csum * onehot, axis=1) - 1
    counts = csum[-1]
    padded = ((counts + tg - 1) // tg) * tg
    ends = jnp.cumsum(padded)
    starts = ends - padded
    pos = jnp.sum(starts[None, :] * onehot, axis=1) + rank
    tok = jnp.concatenate([jnp.arange(m, dtype=jnp.int32)] * 2)
    src = jnp.zeros((n_tiles * tg,), jnp.int32).at[pos].set(tok)
    n_used = (ends[-1] // tg).astype(jnp.int32)
    tile_start = jnp.arange(n_tiles, dtype=jnp.int32) * tg
    tile_e = jnp.sum((tile_start[:, None] >= ends[None, :]).astype(jnp.int32), axis=1)
    last_e = jnp.sum((((n_used - 1) * tg) >= ends).astype(jnp.int32))
    tile_e = jnp.where(jnp.arange(n_tiles) < n_used, tile_e, last_e)
    return src, pos, tile_e, n_used.reshape(1), n_tiles


def kernel(x, meta_tokens, e_norm_mix, e_w_in, ml_i_bias, ml_f_bias, ml_out_norm, rw_mu, rw_w0, rw_w_up, rw_a0, rw_a_up, rw_g_up, rw_k_k, rw_k_a, rw_r_k, rw_ln_w, rw_ln_b, e_w_out, e_norm_ffn, ffn_w_gate, ffn_w_up, ffn_w_down, o_norm_mix, ssd_w_in, ssd_conv_w, ssd_conv_b, ssd_dt_bias, ssd_A_log, ssd_D, ssd_norm, ssd_w_out, o_norm_ffn, moe_router, moe_w_gate, moe_w_up, moe_w_down, final_norm):
    batch, seq, d = x.shape
    seq_t = N_PAD + N_META + seq
    m = batch * seq_t
    h = jnp.concatenate([jnp.zeros((batch, N_PAD, d), x.dtype),
                         jnp.broadcast_to(meta_tokens.astype(x.dtype), (batch, N_META, d)), x], axis=1)
    h = h.reshape(m, d)

    hn = rmsnorm(h, e_norm_mix[0], BF16)
    p = mm_up(hn, e_w_in[0].T, F32, 512).reshape(batch, seq_t, -1)
    y_ml = mlstm_group(p, ml_i_bias[0], ml_f_bias[0], ml_out_norm[0]).reshape(m, ML_V)
    y_rw = rwkv7_group(p, rw_mu[0], rw_w0[0], rw_w_up[0], rw_a0[0], rw_a_up[0], rw_g_up[0], rw_k_k[0],
                       rw_k_a[0], rw_r_k[0], rw_ln_w[0], rw_ln_b[0]).reshape(m, RW_DIM)
    h = mm_down([y_ml, y_rw], _b(e_w_out[0]), h, seq_t)
    hn = rmsnorm(h, e_norm_ffn[0], BF16)
    f = swiglu_up(hn, ffn_w_gate[0], ffn_w_up[0])
    h = mm_down([f], _b(ffn_w_down[0]), h, seq_t)

    hn = rmsnorm(h, o_norm_mix[0], BF16)
    p = mm_up(hn, ssd_w_in[0].T, F32, 1152).reshape(batch, seq_t, -1)
    y = ssd_mixer(p, ssd_conv_w[0], ssd_conv_b[0], ssd_dt_bias[0], ssd_A_log[0], ssd_D[0], ssd_norm[0])
    h = mm_down([y.reshape(m, SSD_DI)], _b(ssd_w_out[0]), h, seq_t)
    route = router_top2(h, o_norm_ffn[0], moe_router[0])
    src, pos, tile_e, n_used, n_tiles = moe_routing(route, m)
    xs = gather_norm(h, o_norm_ffn[0], src, n_used, n_tiles)
    fs = moe_up(xs, moe_w_gate[0], moe_w_up[0], tile_e, n_used)
    ys = moe_down(fs, moe_w_down[0], tile_e, n_used)
    out = combine_norm(h, route, ys, pos, final_norm, batch, seq_t)
    return out.reshape(batch, seq, d).astype(x.dtype)
```

```python
import functools

import jax
import jax.numpy as jnp
from jax import lax
from jax.experimental import pallas as pl
from jax.experimental.pallas import tpu as pltpu

F32 = jnp.float32
BF16 = jnp.bfloat16
HI = lax.Precision.HIGHEST

N_META = 16
CHUNK = 64
N_PAD = CHUNK - N_META
NEG = -1e30
EPS = 1e-6

ML_HEADS = 4
ML_DQK = 128
ML_DV = 256
ML_GATE_CAP = 15.0
ML_QK = ML_HEADS * ML_DQK
ML_V = ML_HEADS * ML_DV

RW_HEADS = 16
RW_N = 64
RW_DIM = RW_HEADS * RW_N
RW_DECAY_LORA = 64
RW_A_LORA = 64
RW_GATE_LORA = 160
RW_LN_EPS = 64e-5
RW_GROUP = 256
RW_HPG = RW_GROUP // RW_N
RW_GPS = 4
LORA_ROWS = 384

E_OFF_GATES = 2 * ML_QK + 2 * ML_V
E_OFF_RW = E_OFF_GATES + 2 * ML_HEADS
RW_LANE_OFF = E_OFF_RW % 128

SSD_DI = 4096
SSD_P = 64
SSD_H = SSD_DI // SSD_P
SSD_N = 128
SSD_G = 8
SSD_R = SSD_H // SSD_G
SSD_K = 4
SSD_GW = SSD_DI // SSD_G
SSD_BC = 2 * SSD_G * SSD_N
SSD_DT_PAD = 128

N_EXPERTS = 8
ROUTER_PAD = 128

VMEM_LIMIT = 56 * 1024 * 1024


def _tile(n, prefs):
    for p in prefs:
        if n % p == 0:
            return p
    return n


_ROW_TILES = (1040, 832, 640, 520, 512, 256, 128, 64)


def _params(sem):
    return pltpu.CompilerParams(dimension_semantics=sem, vmem_limit_bytes=VMEM_LIMIT)


def _dot(a, b, prec=None):
    return lax.dot_general(a, b, (((1,), (0,)), ((), ())), precision=prec, preferred_element_type=F32)


def _dot_nt(a, b, prec=None):
    return lax.dot_general(a, b, (((1,), (1,)), ((), ())), precision=prec, preferred_element_type=F32)


def _dot_tn(a, b, prec=None):
    return lax.dot_general(a, b, (((0,), (0,)), ((), ())), precision=prec, preferred_element_type=F32)


def _b(x):
    return x.astype(BF16)


def _bdot(dot, a, b):
    return dot(_b(a), _b(b))


def _dot_split(a, b):
    hi = _b(a)
    lo = _b(a - hi.astype(F32))
    bb = _b(b)
    return _dot(hi, bb) + _dot(lo, bb)


def _sigmoid(x):
    return 1.0 / (1.0 + jnp.exp(-x))


def _softplus(x):
    return jnp.maximum(x, 0.0) + jnp.log1p(jnp.exp(-jnp.abs(x)))


def _silu(x):
    return x * _sigmoid(x)


def _iota(shape, dim):
    return lax.broadcasted_iota(jnp.int32, shape, dim)


def _tri(n):
    return (_iota((n, n), 0) >= _iota((n, n), 1)).astype(F32)


def _interleave(chains):
    chains = list(chains)
    while chains:
        for ch in list(chains):
            try:
                next(ch)
            except StopIteration:
                chains.remove(ch)


def _rmsnorm_kernel(x_ref, w_ref, o_ref):
    x = x_ref[...]
    ms = jnp.mean(x * x, axis=-1, keepdims=True)
    o_ref[...] = (x * lax.rsqrt(ms + EPS) * w_ref[...]).astype(o_ref.dtype)


def rmsnorm(x, w, out_dtype):
    m, d = x.shape
    tm = _tile(m, _ROW_TILES)
    return pl.pallas_call(
        _rmsnorm_kernel,
        grid=(m // tm,),
        in_specs=[pl.BlockSpec((tm, d), lambda i: (i, 0)), pl.BlockSpec((1, d), lambda i: (0, 0))],
        out_specs=pl.BlockSpec((tm, d), lambda i: (i, 0)),
        out_shape=jax.ShapeDtypeStruct((m, d), out_dtype),
        compiler_params=_params(("arbitrary",)),
        name="rmsnorm",
    )(x, w.reshape(1, d))


def _mm_up_kernel(a_ref, wt_ref, o_ref, wb_ref, *, n_valid):
    @pl.when(pl.program_id(1) == 0)
    def _():
        tn = wt_ref.shape[0]
        row = pl.program_id(0) * tn + _iota((tn, 1), 0)
        wt = jnp.where(row < n_valid, wt_ref[...], 0.0)
        wb_ref[...] = _b(wt.T)

    o_ref[...] = _dot(a_ref[...], wb_ref[...]).astype(o_ref.dtype)


def mm_up(a, wt, out_dtype, tn):
    m, k = a.shape
    n = wt.shape[0]
    tm = _tile(m, _ROW_TILES)
    nt = pl.cdiv(n, tn)
    return pl.pallas_call(
        functools.partial(_mm_up_kernel, n_valid=n),
        grid=(nt, m // tm),
        in_specs=[pl.BlockSpec((tm, k), lambda j, i: (i, 0)), pl.BlockSpec((tn, k), lambda j, i: (j, 0))],
        out_specs=pl.BlockSpec((tm, tn), lambda j, i: (i, j)),
        out_shape=jax.ShapeDtypeStruct((m, nt * tn), out_dtype),
        scratch_shapes=[pltpu.VMEM((k, tn), BF16)],
        compiler_params=_params(("arbitrary", "arbitrary")),
        name="mm_up",
    )(a, wt)


def _swiglu_kernel(a_ref, wg_ref, wu_ref, o_ref, wgb_ref, wub_ref):
    @pl.when(pl.program_id(1) == 0)
    def _():
        wgb_ref[...] = _b(wg_ref[...])
        wub_ref[...] = _b(wu_ref[...])

    a = a_ref[...]
    o_ref[...] = (_silu(_dot(a, wgb_ref[...])) * _dot(a, wub_ref[...])).astype(o_ref.dtype)


def swiglu_up(a, wg, wu):
    m, k = a.shape
    f = wg.shape[-1]
    tm = _tile(m, _ROW_TILES)
    tn = _tile(f, (512, 256, 128))
    w_spec = pl.BlockSpec((k, tn), lambda j, i: (0, j))
    return pl.pallas_call(
        _swiglu_kernel,
        grid=(f // tn, m // tm),
        in_specs=[pl.BlockSpec((tm, k), lambda j, i: (i, 0)), w_spec, w_spec],
        out_specs=pl.BlockSpec((tm, tn), lambda j, i: (i, j)),
        out_shape=jax.ShapeDtypeStruct((m, f), BF16),
        scratch_shapes=[pltpu.VMEM((k, tn), BF16), pltpu.VMEM((k, tn), BF16)],
        compiler_params=_params(("arbitrary", "arbitrary")),
        name="swiglu_up",
    )(a, wg, wu)


def _mm_down_kernel(*refs, tm, seq_t):
    *a_refs, w_ref, r_ref, o_ref = refs
    acc = None
    off = 0
    for a_ref in a_refs:
        k = a_ref.shape[1]
        part = _dot(a_ref[...], w_ref[off:off + k, :])
        acc = part if acc is None else acc + part
        off += k
    t0 = (pl.program_id(0) * tm) % seq_t
    valid = ((t0 + _iota((tm, 1), 0)) >= N_PAD).astype(F32)
    o_ref[...] = r_ref[...] + acc * valid


def mm_down(a_parts, w, res, seq_t):
    m = a_parts[0].shape[0]
    k, n = w.shape
    assert sum(a.shape[1] for a in a_parts) == k
    tm = _tile(seq_t, (520, 512, 256, 128, 64))
    tn = _tile(n, (512, 256, 128))
    a_specs = [pl.BlockSpec((tm, a.shape[1]), lambda i, j: (i, 0)) for a in a_parts]
    return pl.pallas_call(
        functools.partial(_mm_down_kernel, tm=tm, seq_t=seq_t),
        grid=(m // tm, n // tn),
        in_specs=a_specs + [pl.BlockSpec((k, tn), lambda i, j: (0, j)), pl.BlockSpec((tm, tn), lambda i, j: (i, j))],
        out_specs=pl.BlockSpec((tm, tn), lambda i, j: (i, j)),
        out_shape=jax.ShapeDtypeStruct((m, n), F32),
        compiler_params=_params(("arbitrary", "arbitrary")),
        name="mm_down",
    )(*a_parts, w, res)


def _log_sigmoid(x):
    return -_softplus(-x)


def _mlstm_kernel(q_ref, k_ref, v_ref, o_ref, g_ref, bias_ref, onorm_ref, out_ref, c_ref, n_ref, m_ref):
    c = pl.program_id(0)
    L = CHUNK

    @pl.when(c == 0)
    def _():
        c_ref[...] = jnp.zeros_like(c_ref)
        n_ref[...] = jnp.zeros_like(n_ref)
        m_ref[...] = jnp.zeros_like(m_ref)

    valid = (c * L + _iota((L, 1), 0)) >= N_PAD
    causal = _iota((L, L), 0) >= _iota((L, L), 1)
    tri = _tri(L)
    scale = ML_DQK ** -0.5

    def chain(b, h, bcum, ig, b_t, i_t):
        q = q_ref[b, :, h * ML_DQK:(h + 1) * ML_DQK] * scale
        k = k_ref[b, :, h * ML_DQK:(h + 1) * ML_DQK]
        v = v_ref[b, :, h * ML_DV:(h + 1) * ML_DV]
        og = o_ref[b, :, h * ML_DV:(h + 1) * ML_DV]
        bcol = bcum[:, ML_HEADS + h:ML_HEADS + h + 1]
        icol = ig[:, h:h + 1]
        brow = b_t[ML_HEADS + h:ML_HEADS + h + 1, :]
        irow = i_t[h:h + 1, :]
        g = bcol[L - 1:L, :]
        m_prev = m_ref[b, h:h + 1, 0:1]
        c_prev = c_ref[b, h]
        n_prev = n_ref[b, h:h + 1, :]

        dm = jnp.where(causal, bcol - brow + irow, NEG)
        m_inter = bcol + m_prev
        m_t = jnp.maximum(m_inter, jnp.max(dm, axis=-1, keepdims=True))
        yield
        s = _dot_nt(_b(q), _b(k)) * jnp.exp(dm - m_t)
        w_inter = jnp.exp(m_inter - m_t)
        yield
        num = _dot(_b(s), _b(v)) + w_inter * _dot(_b(q), _b(c_prev))
        den = jnp.sum(s, axis=-1, keepdims=True) + w_inter * jnp.sum(q * n_prev, axis=-1, keepdims=True)
        yield
        hh = num / jnp.maximum(jnp.abs(den), jnp.exp(-m_t))
        hh = hh * lax.rsqrt(jnp.mean(hh * hh, axis=-1, keepdims=True) + EPS)
        hh = hh * onorm_ref[:, h * ML_DV:(h + 1) * ML_DV] * _sigmoid(og)
        out_ref[b, :, h * ML_DV:(h + 1) * ML_DV] = hh.astype(out_ref.dtype)
        yield

        a_col = g - bcol + icol
        m_new = jnp.maximum(g + m_prev, jnp.max(a_col, axis=0, keepdims=True))
        dec = jnp.exp(g + m_prev - m_new)
        wk = jnp.exp(a_col - m_new) * k
        yield
        c_ref[b, h] = dec * c_prev + _dot_tn(_b(wk), _b(v))
        n_ref[b, h:h + 1, :] = dec * n_prev + jnp.sum(wk, axis=0, keepdims=True)
        m_ref[b, h:h + 1, :] = jnp.broadcast_to(m_new, (1, m_ref.shape[2]))

    chains = []
    for b in range(q_ref.shape[0]):
        gates = g_ref[b] + bias_ref[...]
        sc = ML_GATE_CAP * jnp.tanh(gates / ML_GATE_CAP)
        ig = jnp.where(valid, sc, NEG)
        lf = jnp.where(valid, _log_sigmoid(sc), 0.0)
        bcum = _dot(tri, lf, HI)
        chains += [chain(b, h, bcum, ig, bcum.T, ig.T) for h in range(ML_HEADS)]
    _interleave(chains)


def mlstm_group(p, i_bias, f_bias, out_norm):
    batch, seq_t, _ = p.shape
    nc = seq_t // CHUNK
    bias = jnp.zeros((1, 128), F32).at[0, 0:ML_HEADS].set(i_bias).at[0, ML_HEADS:2 * ML_HEADS].set(f_bias)
    return pl.pallas_call(
        _mlstm_kernel,
        grid=(nc,),
        in_specs=[
            pl.BlockSpec((batch, CHUNK, ML_QK), lambda c: (0, c, 0)),
            pl.BlockSpec((batch, CHUNK, ML_QK), lambda c: (0, c, 1)),
            pl.BlockSpec((batch, CHUNK, ML_V), lambda c: (0, c, 1)),
            pl.BlockSpec((batch, CHUNK, ML_V), lambda c: (0, c, 2)),
            pl.BlockSpec((batch, CHUNK, 128), lambda c: (0, c, E_OFF_GATES // 128)),
            pl.BlockSpec((1, 128), lambda c: (0, 0)),
            pl.BlockSpec((1, ML_V), lambda c: (0, 0)),
        ],
        out_specs=pl.BlockSpec((batch, CHUNK, ML_V), lambda c: (0, c, 0)),
        out_shape=jax.ShapeDtypeStruct((batch, seq_t, ML_V), BF16),
        scratch_shapes=[pltpu.VMEM((batch, ML_HEADS, ML_DQK, ML_DV), F32), pltpu.VMEM((batch, 8, ML_DQK), F32),
                        pltpu.VMEM((batch, 8, 128), F32)],
        compiler_params=_params(("arbitrary",)),
        name="mlstm",
    )(p, p, p, p, p, bias, out_norm.reshape(1, ML_V))


def _rwkv_kernel(*refs):
    nb = 2 * RW_GPS + 1
    r_blk, k_blk, v_blk = refs[0:nb], refs[nb:2 * nb], refs[2 * nb:3 * nb]
    tail_ref, mu3_ref, mut_ref, chan_ref, wl_ref, out_ref, h_ref, cr_ref, ck_ref, cv_ref, ct_ref = refs[3 * nb:]
    c = pl.program_id(1)
    L = CHUNK
    G = RW_GROUP

    @pl.when(c == 0)
    def _():
        h_ref[...] = jnp.zeros_like(h_ref)
        cr_ref[...] = jnp.zeros_like(cr_ref)
        ck_ref[...] = jnp.zeros_like(ck_ref)
        cv_ref[...] = jnp.zeros_like(cv_ref)
        ct_ref[...] = jnp.zeros_like(ct_ref)

    row = _iota((L, 1), 0)
    valid = (c * L + row) >= N_PAD

    ri = _iota((G, G), 0)
    ci = _iota((G, G), 1)
    same = (ri >> 6) == (ci >> 6)
    tl = ri & (RW_N - 1)
    sl = ci & (RW_N - 1)
    bd_mask = same.astype(F32)
    eye = (ri == ci).astype(F32)
    lower = same & (tl >= sl)
    strict = same & (tl > sl)
    ones_bd = _b(bd_mask)
    tri = _tri(L)
    lane = _iota((L, LORA_ROWS), 1)

    def head_sum(x):
        hi = _b(x)
        return _dot(hi, ones_bd) + _dot(_b(x - hi.astype(F32)), ones_bd)

    def bd(x):
        return _b(jnp.concatenate([x] * RW_HPG, axis=0) * bd_mask)

    def shift_lerp(x, carry_ref, idx, mu):
        prev = carry_ref[idx]
        xs = jnp.where(row == 0, prev, pltpu.roll(x, 1, 0))
        carry_ref[idx] = x[L - 1:L, :]
        return x + mu * (xs - x)

    def unshift(blocks, b, j):
        x = jnp.concatenate([blk[b] for blk in blocks[2 * j:2 * j + 3]], axis=1)
        return x[:, RW_LANE_OFF:RW_LANE_OFF + G]

    def chain(b, j):
        gl = slice(j * G, (j + 1) * G)
        w0 = chan_ref[0:1, gl]
        a0 = chan_ref[1:2, gl]
        k_k = chan_ref[2:3, gl]
        k_a = chan_ref[3:4, gl]
        r_k = chan_ref[4:5, gl]
        ln_w = chan_ref[5:6, gl]
        ln_b = chan_ref[6:7, gl]
        r = shift_lerp(unshift(r_blk, b, j), cr_ref, (b, j), mu3_ref[0:1, gl])
        k = shift_lerp(unshift(k_blk, b, j), ck_ref, (b, j), mu3_ref[1:2, gl])
        v = shift_lerp(unshift(v_blk, b, j), cv_ref, (b, j), mu3_ref[2:3, gl])
        tail = shift_lerp(tail_ref[b], ct_ref, (b, j), mut_ref[...])

        lo1 = RW_LANE_OFF + RW_DECAY_LORA
        lo2 = lo1 + RW_A_LORA
        lo3 = lo2 + RW_GATE_LORA
        act = jnp.where(lane < RW_LANE_OFF, 0.0,
                        jnp.where(lane < lo1, jnp.tanh(tail),
                                  jnp.where(lane < lo2, tail, jnp.where(lane < lo3, _sigmoid(tail), 0.0))))
        lw = _dot(act[:, 0:128], wl_ref[0, 0:128, gl], HI)
        la = _bdot(_dot, act[:, 0:256], wl_ref[1, 0:256, gl])
        gate = _bdot(_dot, act[:, 128:384], wl_ref[2, 128:384, gl])
        yield

        w_log = -_softplus(-(w0 + lw)) - 0.5
        ld = jnp.where(valid, -jnp.exp(w_log), 0.0)
        a = _sigmoid(a0 + la)
        kk = k * k_k
        kk = kk * lax.rsqrt(jnp.maximum(head_sum(kk * kk), 1e-24))
        k2 = k * (1.0 + (a - 1.0) * k_a)
        k2 = jnp.where(valid, k2, 0.0)
        kk = jnp.where(valid, kk, 0.0)
        va = -kk
        vb = kk * a

        yield
        cum = _dot(tri, ld, HI)
        cum_l = cum[L - 1:L, :]
        w_t = jnp.exp(cum)
        w_inv = jnp.exp(-cum)
        w_prev = jnp.exp(cum - ld)
        w_end = jnp.exp(cum_l - cum)
        w_last = jnp.exp(cum_l)

        a_bd = bd(va * w_prev)
        r_bd = bd(r * w_t)
        b_bd = bd(vb * w_inv)
        k_bd = bd(k2 * w_inv)
        bl_bd = bd(vb * w_end)
        kl_bd = bd(k2 * w_end)
        v_bd = bd(v)
        yield

        n_ab = jnp.where(strict, _dot_nt(a_bd, b_bd), 0.0)
        a_ak = _b(jnp.where(strict, _dot_nt(a_bd, k_bd), 0.0))
        m_rb = _b(jnp.where(lower, _dot_nt(r_bd, b_bd), 0.0))
        m_rk = _b(jnp.where(lower, _dot_nt(r_bd, k_bd), 0.0))
        yield

        t_inv = eye + n_ab
        pw = n_ab
        for _ in range(5):
            pw_b = _b(pw)
            pw = _dot(pw_b, pw_b)
            yield
            t_inv = t_inv + _bdot(_dot, t_inv, pw)
            yield

        t_b = _b(t_inv)
        av = _dot(a_ak, v_bd)
        yield
        a_p = _b(_dot(t_b, a_bd))
        u0 = _b(_dot(t_b, _b(av)))
        yield
        p_mat = eye * w_last + _dot_tn(bl_bd, a_p)
        q_mat = _dot_tn(bl_bd, u0) + _dot_tn(kl_bd, v_bd)
        r_p = r_bd.astype(F32) + _dot(m_rb, a_p)
        y0 = _dot(m_rb, u0) + _dot(m_rk, v_bd)
        yield

        h0 = _b(h_ref[b, j])
        y_bd = _dot(_b(r_p), h0) + y0
        h_ref[b, j] = _dot(_b(p_mat), h0) + q_mat
        y = y_bd[0:L] + y_bd[L:2 * L] + y_bd[2 * L:3 * L] + y_bd[3 * L:4 * L]
        yield

        inv_n = 1.0 / RW_N
        mean = head_sum(y) * inv_n
        d = y - mean
        var = head_sum(d * d) * inv_n
        yn = d * lax.rsqrt(var + RW_LN_EPS) * ln_w + ln_b
        bonus = head_sum(r * k2 * r_k) * v
        out_ref[b, :, gl] = ((yn + bonus) * gate).astype(out_ref.dtype)

    _interleave(chain(b, j) for b in range(out_ref.shape[0]) for j in range(RW_GPS))


def rwkv7_group(p, mu, w0, w_up, a0, a_up, g_up, k_k, k_a, r_k, ln_w, ln_b):
    batch, seq_t, _ = p.shape
    nc = seq_t // CHUNK
    ng = RW_DIM // RW_GROUP
    n_lora = RW_DECAY_LORA + RW_A_LORA + RW_GATE_LORA
    o = RW_LANE_OFF
    mu3 = mu[:3 * RW_DIM].reshape(3, RW_DIM)
    mut = jnp.zeros((1, LORA_ROWS), F32).at[0, o:o + n_lora].set(mu[3 * RW_DIM:])
    chan = jnp.stack([w0, a0, k_k, k_a, r_k.reshape(RW_DIM), ln_w, ln_b, jnp.zeros((RW_DIM,), F32)])
    wl = jnp.zeros((3, LORA_ROWS, RW_DIM), F32)
    wl = wl.at[0, o:o + RW_DECAY_LORA].set(w_up)
    wl = wl.at[1, o + RW_DECAY_LORA:o + RW_DECAY_LORA + RW_A_LORA].set(a_up)
    wl = wl.at[2, o + RW_DECAY_LORA + RW_A_LORA:o + n_lora].set(g_up)

    gw = RW_GPS * RW_GROUP

    def col(seg, part):
        first = (E_OFF_RW + seg * RW_DIM - o) // 128
        return pl.BlockSpec((batch, CHUNK, 128), lambda g, c: (0, c, first + 2 * RW_GPS * g + part))

    nb = 2 * RW_GPS + 1
    cols = [col(seg, part) for seg in range(3) for part in range(nb)]
    carry = lambda width: pltpu.VMEM((batch, RW_GPS, 1, width), F32)
    return pl.pallas_call(
        _rwkv_kernel,
        grid=(ng // RW_GPS, nc),
        in_specs=cols + [
            pl.BlockSpec((batch, CHUNK, LORA_ROWS), lambda g, c: (0, c, (E_OFF_RW + 3 * RW_DIM - o) // LORA_ROWS)),
            pl.BlockSpec((3, gw), lambda g, c: (0, g)),
            pl.BlockSpec((1, LORA_ROWS), lambda g, c: (0, 0)),
            pl.BlockSpec((8, gw), lambda g, c: (0, g)),
            pl.BlockSpec((3, LORA_ROWS, gw), lambda g, c: (0, 0, g)),
        ],
        out_specs=pl.BlockSpec((batch, CHUNK, gw), lambda g, c: (0, c, g)),
        out_shape=jax.ShapeDtypeStruct((batch, seq_t, RW_DIM), BF16),
        scratch_shapes=[pltpu.VMEM((batch, RW_GPS, RW_GROUP, RW_GROUP), F32), carry(RW_GROUP), carry(RW_GROUP),
                        carry(RW_GROUP), carry(LORA_ROWS)],
        compiler_params=_params(("arbitrary", "arbitrary")),
        name="rwkv7",
    )(*([p] * (3 * nb + 1)), mu3, mut, chan, wl)


def _ssd_kernel(z_ref, xs_ref, bc_ref, dt_ref, cwx_ref, cwb_ref, cbx_ref, cbb_ref, hp_ref, dexp_ref, nw_ref,
                out_ref, s_ref, px_ref, pb_ref):
    c = pl.program_id(0)
    L = CHUNK

    @pl.when(c == 0)
    def _():
        s_ref[...] = jnp.zeros_like(s_ref)
        px_ref[...] = jnp.zeros_like(px_ref)
        pb_ref[...] = jnp.zeros_like(pb_ref)

    row8 = _iota((8, 1), 0)
    valid = (c * L + _iota((L, 1), 0)) >= N_PAD
    lane = _iota((L, 2 * SSD_P), 1)
    causal = _iota((L, L), 0) >= _iota((L, L), 1)
    tri = _tri(L)

    def conv_silu(x, prev_ref, b, w_ref, b_ref):
        prev = prev_ref[b]
        acc = x * w_ref[SSD_K - 1:SSD_K, :] + b_ref[...]
        for j in range(1, SSD_K):
            xr = pltpu.roll(x, j, 0)
            pr = pltpu.roll(prev, j, 0)
            top = jnp.where(row8 < j, pr, xr[0:8])
            sh = jnp.concatenate([top, xr[8:]], axis=0)
            acc = acc + sh * w_ref[SSD_K - 1 - j:SSD_K - j, :]
        prev_ref[b] = x[L - 8:L, :]
        return _silu(acc)

    def expand(x):
        parts = []
        for j in range(SSD_H // 2):
            lo = jnp.broadcast_to(x[:, 2 * j:2 * j + 1], (L, 2 * SSD_P))
            hi = jnp.broadcast_to(x[:, 2 * j + 1:2 * j + 2], (L, 2 * SSD_P))
            parts.append(jnp.where(lane < SSD_P, lo, hi))
        return jnp.concatenate(parts, axis=1)

    def chain(b):
        xs = conv_silu(xs_ref[b], px_ref, b, cwx_ref, cbx_ref)
        yield
        bc = conv_silu(bc_ref[b], pb_ref, b, cwb_ref, cbb_ref)
        yield

        dt = jnp.where(valid, _softplus(dt_ref[b] + hp_ref[0:1, :]), 0.0)
        a = dt * (-jnp.exp(hp_ref[1:2, :]))
        acum = _dot(tri, a, HI)
        a_t = acum.T
        dt_t = dt.T
        a_last = acum[L - 1:L, :]
        cdec = jnp.exp(a_last)
        yield
        xw = xs * expand(dt * jnp.exp(a_last - acum))
        yield
        e_ea = expand(jnp.exp(acum))
        yield

        ys = []
        for g in range(SSD_G):
            bg = bc[:, g * SSD_N:(g + 1) * SSD_N]
            cg = bc[:, SSD_G * SSD_N + g * SSD_N:SSD_G * SSD_N + (g + 1) * SSD_N]
            cb = _dot_nt(_b(cg), _b(bg))
            s_prev = s_ref[b, g]
            y_inter = _dot_nt(_b(cg), _b(s_prev)) * e_ea[:, g * SSD_GW:(g + 1) * SSD_GW]
            yield
            parts = []
            cd_rows = []
            for r in range(SSD_R):
                h = g * SSD_R + r
                seg = acum[:, h:h + 1] - a_t[h:h + 1, :]
                ldec = jnp.exp(jnp.where(causal, seg, NEG))
                parts.append(_dot(_b(cb * ldec * dt_t[h:h + 1, :]), _b(xs[:, h * SSD_P:(h + 1) * SSD_P])))
                cd_rows.append(jnp.broadcast_to(cdec[:, h:h + 1], (SSD_P, SSD_N)))
                if r % 2:
                    yield
            ys.append(jnp.concatenate(parts, axis=1) + y_inter)
            st = _dot_tn(_b(xw[:, g * SSD_GW:(g + 1) * SSD_GW]), _b(bg))
            s_ref[b, g] = jnp.concatenate(cd_rows, axis=0) * s_prev + st
            yield

        y = jnp.concatenate(ys, axis=1) + dexp_ref[...] * xs
        y = y * _silu(z_ref[b])
        yield
        outs = []
        for g in range(SSD_G):
            yg = y[:, g * SSD_GW:(g + 1) * SSD_GW]
            outs.append(yg * lax.rsqrt(jnp.mean(yg * yg, axis=-1, keepdims=True) + EPS))
        out_ref[b] = (jnp.concatenate(outs, axis=1) * nw_ref[...]).astype(out_ref.dtype)

    _interleave(chain(b) for b in range(out_ref.shape[0]))


def ssd_mixer(p, conv_w, conv_b, dt_bias, a_log, d_skip, norm_w):
    batch, seq_t, _ = p.shape
    nc = seq_t // CHUNK
    hp = jnp.zeros((8, SSD_DT_PAD), F32).at[0, :SSD_H].set(dt_bias).at[1, :SSD_H].set(a_log)
    dexp = jnp.repeat(d_skip, SSD_P).reshape(1, SSD_DI)
    conv_b = conv_b.reshape(1, -1)
    return pl.pallas_call(
        _ssd_kernel,
        grid=(nc,),
        in_specs=[
            pl.BlockSpec((batch, CHUNK, SSD_DI), lambda c: (0, c, 0)),
            pl.BlockSpec((batch, CHUNK, SSD_DI), lambda c: (0, c, 1)),
            pl.BlockSpec((batch, CHUNK, SSD_BC), lambda c: (0, c, 2 * SSD_DI // SSD_BC)),
            pl.BlockSpec((batch, CHUNK, SSD_DT_PAD), lambda c: (0, c, (2 * SSD_DI + SSD_BC) // SSD_DT_PAD)),
            pl.BlockSpec((SSD_K, SSD_DI), lambda c: (0, 0)),
            pl.BlockSpec((SSD_K, SSD_BC), lambda c: (0, SSD_DI // SSD_BC)),
            pl.BlockSpec((1, SSD_DI), lambda c: (0, 0)),
            pl.BlockSpec((1, SSD_BC), lambda c: (0, SSD_DI // SSD_BC)),
            pl.BlockSpec((8, SSD_DT_PAD), lambda c: (0, 0)),
            pl.BlockSpec((1, SSD_DI), lambda c: (0, 0)),
            pl.BlockSpec((1, SSD_DI), lambda c: (0, 0)),
        ],
        out_specs=pl.BlockSpec((batch, CHUNK, SSD_DI), lambda c: (0, c, 0)),
        out_shape=jax.ShapeDtypeStruct((batch, seq_t, SSD_DI), BF16),
        scratch_shapes=[pltpu.VMEM((batch, SSD_G, SSD_GW, SSD_N), F32), pltpu.VMEM((batch, 8, SSD_DI), F32),
                        pltpu.VMEM((batch, 8, SSD_BC), F32)],
        compiler_params=_params(("arbitrary",)),
        name="ssd",
    )(p, p, p, p, conv_w, conv_w, conv_b, conv_b, hp, dexp, norm_w.reshape(1, SSD_DI))


MOE_TILE = 512
ROUTE_W1, ROUTE_W2, ROUTE_I1, ROUTE_I2 = 0, 1, 2, 3


def _router_kernel(x_ref, w_ref, r_ref, o_ref):
    x = x_ref[...]
    hn = x * lax.rsqrt(jnp.mean(x * x, axis=-1, keepdims=True) + EPS) * w_ref[...]
    logits = _dot(hn, r_ref[...], HI)
    lane = _iota(logits.shape, 1)
    lane_f = lane.astype(F32)
    logits = jnp.where(lane < N_EXPERTS, logits, -jnp.inf)
    m1 = jnp.max(logits, axis=-1, keepdims=True)
    i1 = jnp.min(jnp.where(logits == m1, lane_f, float(ROUTER_PAD)), axis=-1, keepdims=True)
    rest = jnp.where(lane_f == i1, -jnp.inf, logits)
    m2 = jnp.max(rest, axis=-1, keepdims=True)
    i2 = jnp.min(jnp.where(rest == m2, lane_f, float(ROUTER_PAD)), axis=-1, keepdims=True)
    e2 = jnp.exp(m2 - m1)
    w1 = 1.0 / (1.0 + e2)
    w2 = e2 / (1.0 + e2)
    o_ref[...] = jnp.where(lane == ROUTE_W1, w1, jnp.where(lane == ROUTE_W2, w2,
                           jnp.where(lane == ROUTE_I1, i1, jnp.where(lane == ROUTE_I2, i2, 0.0))))


def router_top2(x, w, router):
    m, d = x.shape
    tm = _tile(m, (520, 512, 256, 128, 64))
    rpad = jnp.zeros((d, ROUTER_PAD), F32).at[:, :N_EXPERTS].set(router)
    return pl.pallas_call(
        _router_kernel,
        grid=(m // tm,),
        in_specs=[pl.BlockSpec((tm, d), lambda i: (i, 0)), pl.BlockSpec((1, d), lambda i: (0, 0)),
                  pl.BlockSpec((d, ROUTER_PAD), lambda i: (0, 0))],
        out_specs=pl.BlockSpec((tm, ROUTER_PAD), lambda i: (i, 0)),
        out_shape=jax.ShapeDtypeStruct((m, ROUTER_PAD), F32),
        compiler_params=_params(("arbitrary",)),
        name="router_top2",
    )(x, w.reshape(1, d), rpad)


def _row_copy(src_hbm, row, dst, r, sem):
    return pltpu.make_async_copy(src_hbm.at[pl.ds(row, 1)], dst.at[pl.ds(r, 1)], sem)


def _gather_norm_kernel(nused_ref, idx_ref, h_hbm, w_ref, o_ref, buf, sem):
    t = pl.program_id(0)
    tg = buf.shape[1]
    n_used = nused_ref[0]

    def issue(tile, slot):
        def body(r2, carry):
            for u in range(2):
                r = 2 * r2 + u
                _row_copy(h_hbm, idx_ref[tile * tg + r], buf.at[slot], r, sem.at[slot]).start(priority=u)
            return carry

        lax.fori_loop(0, tg // 2, body, 0, unroll=4)

    @pl.when(jnp.logical_and(t == 0, n_used > 0))
    def _():
        issue(0, 0)

    @pl.when(t + 1 < n_used)
    def _():
        issue(t + 1, (t + 1) % 2)

    @pl.when(t < n_used)
    def _():
        slot = t % 2
        pltpu.make_async_copy(h_hbm.at[pl.ds(0, tg)], buf.at[slot], sem.at[slot]).wait()
        x = buf[slot]
        hn = x * lax.rsqrt(jnp.mean(x * x, axis=-1, keepdims=True) + EPS) * w_ref[...]
        o_ref[...] = hn.astype(o_ref.dtype)

    @pl.when(t >= n_used)
    def _():
        o_ref[...] = jnp.zeros_like(o_ref)


def gather_norm(h, w, src, n_used, n_tiles):
    m, d = h.shape
    tg = MOE_TILE
    return pl.pallas_call(
        _gather_norm_kernel,
        grid_spec=pltpu.PrefetchScalarGridSpec(
            num_scalar_prefetch=2,
            grid=(n_tiles,),
            in_specs=[pl.BlockSpec(memory_space=pl.ANY), pl.BlockSpec((1, d), lambda t, *_: (0, 0))],
            out_specs=pl.BlockSpec((tg, d), lambda t, *_: (t, 0)),
            scratch_shapes=[pltpu.VMEM((2, tg, d), F32), pltpu.SemaphoreType.DMA((2,))],
        ),
        out_shape=jax.ShapeDtypeStruct((n_tiles * tg, d), BF16),
        compiler_params=_params(("arbitrary",)),
        name="moe_gather",
    )(n_used, src, h, w.reshape(1, d))


def _tile_loop(lo, hi, a_copy, o_copy, compute):
    @pl.when(lo < hi)
    def _():
        a_copy(lo, 0).start()

    def body(t, carry):
        slot = (t - lo) % 2
        a_copy(t, slot).wait()

        @pl.when(t + 1 < hi)
        def _():
            a_copy(t + 1, 1 - slot).start()

        @pl.when(t - lo >= 2)
        def _():
            o_copy(t - 2, slot).wait()

        compute(slot)
        o_copy(t, slot).start()
        return carry

    lax.fori_loop(lo, hi, body, 0)

    @pl.when(hi - lo >= 2)
    def _():
        o_copy(hi - 2, (hi - 2 - lo) % 2).wait()

    @pl.when(hi - lo >= 1)
    def _():
        o_copy(hi - 1, (hi - 1 - lo) % 2).wait()


def _zero_idle_tiles(first, last, zbuf, o_copy_from, sem):
    zbuf[...] = jnp.zeros_like(zbuf)

    def body(t, carry):
        cp = o_copy_from(zbuf, t, sem)
        cp.start()
        cp.wait()
        return carry

    lax.fori_loop(first, last, body, 0)


def _moe_up_kernel(lo_ref, hi_ref, xs_hbm, wg_ref, wu_ref, out_hbm, wgb_ref, wub_ref, abuf, obuf, zbuf,
                   asem, osem, zsem):
    j = pl.program_id(0)
    e = pl.program_id(1)
    tg = abuf.shape[1]
    tn = obuf.shape[2]
    col = pl.multiple_of(j * tn, tn)
    wgb_ref[...] = _b(wg_ref[...])
    wub_ref[...] = _b(wu_ref[...])

    def a_copy(t, slot):
        return pltpu.make_async_copy(xs_hbm.at[pl.ds(pl.multiple_of(t * tg, tg), tg)], abuf.at[slot], asem.at[slot])

    def o_copy_from(src, t, sem):
        return pltpu.make_async_copy(src, out_hbm.at[pl.ds(pl.multiple_of(t * tg, tg), tg), pl.ds(col, tn)], sem)

    def o_copy(t, slot):
        return o_copy_from(obuf.at[slot], t, osem.at[slot])

    def compute(slot):
        a = abuf[slot]
        obuf[slot] = (_silu(_dot(a, wgb_ref[...])) * _dot(a, wub_ref[...])).astype(obuf.dtype)

    _tile_loop(lo_ref[e], hi_ref[e], a_copy, o_copy, compute)

    @pl.when(e == pl.num_programs(1) - 1)
    def _():
        _zero_idle_tiles(hi_ref[e], out_hbm.shape[0] // tg, zbuf, o_copy_from, zsem)


def moe_up(xs, wg, wu, tile_lo, tile_hi):
    p, k = xs.shape
    n_e, _, f = wg.shape
    tg = MOE_TILE
    tn = _tile(f, (512, 256, 128))
    w_spec = pl.BlockSpec((None, k, tn), lambda j, e, lo, hi: (e, 0, j))
    any_spec = pl.BlockSpec(memory_space=pl.ANY)
    return pl.pallas_call(
        _moe_up_kernel,
        grid_spec=pltpu.PrefetchScalarGridSpec(
            num_scalar_prefetch=2,
            grid=(f // tn, n_e),
            in_specs=[any_spec, w_spec, w_spec],
            out_specs=any_spec,
            scratch_shapes=[pltpu.VMEM((k, tn), BF16), pltpu.VMEM((k, tn), BF16), pltpu.VMEM((2, tg, k), BF16),
                            pltpu.VMEM((2, tg, tn), BF16), pltpu.VMEM((tg, tn), BF16),
                            pltpu.SemaphoreType.DMA((2,)), pltpu.SemaphoreType.DMA((2,)),
                            pltpu.SemaphoreType.DMA(())],
        ),
        out_shape=jax.ShapeDtypeStruct((p, f), BF16),
        compiler_params=_params(("arbitrary", "arbitrary")),
        name="moe_up",
    )(tile_lo, tile_hi, xs, wg, wu)


def _moe_down_kernel(lo_ref, hi_ref, fs_hbm, w_ref, out_hbm, wb_ref, abuf, obuf, zbuf, asem, osem, zsem):
    j = pl.program_id(0)
    e = pl.program_id(1)
    tg = abuf.shape[1]
    tn = obuf.shape[2]
    col = pl.multiple_of(j * tn, tn)
    wb_ref[...] = _b(w_ref[...])

    def a_copy(t, slot):
        return pltpu.make_async_copy(fs_hbm.at[pl.ds(pl.multiple_of(t * tg, tg), tg)], abuf.at[slot], asem.at[slot])

    def o_copy_from(src, t, sem):
        return pltpu.make_async_copy(src, out_hbm.at[pl.ds(pl.multiple_of(t * tg, tg), tg), pl.ds(col, tn)], sem)

    def o_copy(t, slot):
        return o_copy_from(obuf.at[slot], t, osem.at[slot])

    def compute(slot):
        obuf[slot] = _dot(abuf[slot], wb_ref[...])

    _tile_loop(lo_ref[e], hi_ref[e], a_copy, o_copy, compute)

    @pl.when(e == pl.num_programs(1) - 1)
    def _():
        _zero_idle_tiles(hi_ref[e], out_hbm.shape[0] // tg, zbuf, o_copy_from, zsem)


def moe_down(fs, wd, tile_lo, tile_hi):
    p, f = fs.shape
    n_e, _, d = wd.shape
    tg = MOE_TILE
    tn = _tile(d, (512, 256, 128))
    any_spec = pl.BlockSpec(memory_space=pl.ANY)
    return pl.pallas_call(
        _moe_down_kernel,
        grid_spec=pltpu.PrefetchScalarGridSpec(
            num_scalar_prefetch=2,
            grid=(d // tn, n_e),
            in_specs=[any_spec, pl.BlockSpec((None, f, tn), lambda j, e, lo, hi: (e, 0, j))],
            out_specs=any_spec,
            scratch_shapes=[pltpu.VMEM((f, tn), BF16), pltpu.VMEM((2, tg, f), BF16), pltpu.VMEM((2, tg, tn), F32),
                            pltpu.VMEM((tg, tn), F32), pltpu.SemaphoreType.DMA((2,)),
                            pltpu.SemaphoreType.DMA((2,)), pltpu.SemaphoreType.DMA(())],
        ),
        out_shape=jax.ShapeDtypeStruct((p, d), F32),
        compiler_params=_params(("arbitrary", "arbitrary")),
        name="moe_down",
    )(tile_lo, tile_hi, fs, wd)


def _combine_norm_kernel(pos_ref, h_hbm, route_hbm, y_hbm, w_ref, o_ref, hbuf, rbuf, buf0, buf1, sem,
                         *, seq_t, m, tiles_per_seq):
    s = pl.program_id(0)
    tm = hbuf.shape[1]

    def first_row(step):
        return (step // tiles_per_seq) * seq_t + (N_PAD + N_META) + (step % tiles_per_seq) * tm

    def copies(base, slot):
        return (pltpu.make_async_copy(h_hbm.at[pl.ds(base, tm)], hbuf.at[slot], sem.at[slot, 0]),
                pltpu.make_async_copy(route_hbm.at[pl.ds(base, tm)], rbuf.at[slot], sem.at[slot, 1]))

    def issue(step, slot):
        base = first_row(step)
        for cp in copies(base, slot):
            cp.start()

        def body(r, carry):
            _row_copy(y_hbm, pos_ref[base + r], buf0.at[slot], r, sem.at[slot, 2]).start(priority=0)
            _row_copy(y_hbm, pos_ref[m + base + r], buf1.at[slot], r, sem.at[slot, 3]).start(priority=1)
            return carry

        lax.fori_loop(0, tm, body, 0, unroll=4)

    @pl.when(s == 0)
    def _():
        issue(0, 0)

    @pl.when(s + 1 < pl.num_programs(0))
    def _():
        issue(s + 1, (s + 1) % 2)

    slot = s % 2
    for cp in copies(0, slot):
        cp.wait()
    pltpu.make_async_copy(y_hbm.at[pl.ds(0, tm)], buf0.at[slot], sem.at[slot, 2]).wait()
    pltpu.make_async_copy(y_hbm.at[pl.ds(0, tm)], buf1.at[slot], sem.at[slot, 3]).wait()
    w1 = rbuf[slot, :, ROUTE_W1:ROUTE_W1 + 1]
    w2 = rbuf[slot, :, ROUTE_W2:ROUTE_W2 + 1]
    x = hbuf[slot] + (w1 * buf0[slot] + w2 * buf1[slot])
    o_ref[...] = (x * lax.rsqrt(jnp.mean(x * x, axis=-1, keepdims=True) + EPS) * w_ref[...]).astype(o_ref.dtype)


def combine_norm(h, route, ys, pos, w, batch, seq_t):
    m, d = h.shape
    seq = seq_t - N_PAD - N_META
    tm = _tile(seq, (256, 128, 64))
    any_spec = pl.BlockSpec(memory_space=pl.ANY)
    return pl.pallas_call(
        functools.partial(_combine_norm_kernel, seq_t=seq_t, m=m, tiles_per_seq=seq // tm),
        grid_spec=pltpu.PrefetchScalarGridSpec(
            num_scalar_prefetch=1,
            grid=(batch * seq // tm,),
            in_specs=[any_spec, any_spec, any_spec, pl.BlockSpec((1, d), lambda s, *_: (0, 0))],
            out_specs=pl.BlockSpec((tm, d), lambda s, *_: (s, 0)),
            scratch_shapes=[pltpu.VMEM((2, tm, d), F32), pltpu.VMEM((2, tm, ROUTER_PAD), F32),
                            pltpu.VMEM((2, tm, d), F32), pltpu.VMEM((2, tm, d), F32),
                            pltpu.SemaphoreType.DMA((2, 4))],
        ),
        out_shape=jax.ShapeDtypeStruct((batch * seq, d), F32),
        compiler_params=_params(("arbitrary",)),
        name="moe_combine",
    )(pos, h, route, ys, w.reshape(1, d))


def moe_routing(route, batch, seq_t):
    m = batch * seq_t
    tg = MOE_TILE
    n_tiles = (2 * m + tg - 1) // tg + N_EXPERTS
    e_flat = jnp.concatenate([route[:, ROUTE_I1], route[:, ROUTE_I2]]).astype(jnp.int32)
    tok = jnp.concatenate([jnp.arange(m, dtype=jnp.int32)] * 2)
    real = (tok % seq_t) >= (N_PAD + N_META)
    onehot = ((e_flat[:, None] == jnp.arange(N_EXPERTS, dtype=jnp.int32)[None, :]) & real[:, None]).astype(jnp.int32)
    csum = jnp.cumsum(onehot, axis=0)
    rank = jnp.sum(csum * onehot, axis=1) - 1
    counts = csum[-1]
    padded = ((counts + tg - 1) // tg) * tg
    ends = jnp.cumsum(padded)
    starts = ends - padded
    pos = jnp.sum(starts[None, :] * onehot, axis=1) + rank
    pos = jnp.where(real, pos, n_tiles * tg)
    src = jnp.zeros((n_tiles * tg,), jnp.int32).at[pos].set(tok, mode="drop")
    n_used = (ends[-1] // tg).astype(jnp.int32)
    return src, pos, starts // tg, ends // tg, n_used.reshape(1), n_tiles


def kernel(x, meta_tokens, e_norm_mix, e_w_in, ml_i_bias, ml_f_bias, ml_out_norm, rw_mu, rw_w0, rw_w_up, rw_a0, rw_a_up, rw_g_up, rw_k_k, rw_k_a, rw_r_k, rw_ln_w, rw_ln_b, e_w_out, e_norm_ffn, ffn_w_gate, ffn_w_up, ffn_w_down, o_norm_mix, ssd_w_in, ssd_conv_w, ssd_conv_b, ssd_dt_bias, ssd_A_log, ssd_D, ssd_norm, ssd_w_out, o_norm_ffn, moe_router, moe_w_gate, moe_w_up, moe_w_down, final_norm):
    batch, seq, d = x.shape
    seq_t = N_PAD + N_META + seq
    m = batch * seq_t
    h = jnp.concatenate([jnp.zeros((batch, N_PAD, d), x.dtype),
                         jnp.broadcast_to(meta_tokens.astype(x.dtype), (batch, N_META, d)), x], axis=1)
    h = h.reshape(m, d)

    hn = rmsnorm(h, e_norm_mix[0], BF16)
    p = mm_up(hn, e_w_in[0].T, F32, 512).reshape(batch, seq_t, -1)
    y_ml = mlstm_group(p, ml_i_bias[0], ml_f_bias[0], ml_out_norm[0]).reshape(m, ML_V)
    y_rw = rwkv7_group(p, rw_mu[0], rw_w0[0], rw_w_up[0], rw_a0[0], rw_a_up[0], rw_g_up[0], rw_k_k[0],
                       rw_k_a[0], rw_r_k[0], rw_ln_w[0], rw_ln_b[0]).reshape(m, RW_DIM)
    h = mm_down([y_ml, y_rw], _b(e_w_out[0]), h, seq_t)
    hn = rmsnorm(h, e_norm_ffn[0], BF16)
    f = swiglu_up(hn, ffn_w_gate[0], ffn_w_up[0])
    h = mm_down([f], _b(ffn_w_down[0]), h, seq_t)

    hn = rmsnorm(h, o_norm_mix[0], BF16)
    p = mm_up(hn, ssd_w_in[0].T, F32, 1152).reshape(batch, seq_t, -1)
    y = ssd_mixer(p, ssd_conv_w[0], ssd_conv_b[0], ssd_dt_bias[0], ssd_A_log[0], ssd_D[0], ssd_norm[0])
    h = mm_down([y.reshape(m, SSD_DI)], _b(ssd_w_out[0]), h, seq_t)
    route = router_top2(h, o_norm_ffn[0], moe_router[0])
    src, pos, tile_lo, tile_hi, n_used, n_tiles = moe_routing(route, batch, seq_t)
    xs = gather_norm(h, o_norm_ffn[0], src, n_used, n_tiles)
    fs = moe_up(xs, moe_w_gate[0], moe_w_up[0], tile_lo, tile_hi)
    ys = moe_down(fs, moe_w_down[0], tile_lo, tile_hi)
    out = combine_norm(h, route, ys, pos, final_norm, batch, seq_t)
    return out.reshape(batch, seq, d).astype(x.dtype)
```

```python
import functools

import jax
import jax.numpy as jnp
from jax import lax
from jax.experimental import pallas as pl
from jax.experimental.pallas import tpu as pltpu

F32 = jnp.float32
BF16 = jnp.bfloat16
HI = lax.Precision.HIGHEST

N_META = 16
CHUNK = 64
N_PAD = CHUNK - N_META
NEG = -1e30
EPS = 1e-6

ML_HEADS = 4
ML_DQK = 128
ML_DV = 256
ML_GATE_CAP = 15.0
ML_QK = ML_HEADS * ML_DQK
ML_V = ML_HEADS * ML_DV

RW_HEADS = 16
RW_N = 64
RW_DIM = RW_HEADS * RW_N
RW_DECAY_LORA = 64
RW_A_LORA = 64
RW_GATE_LORA = 160
RW_LN_EPS = 64e-5
RW_GROUP = 256
RW_HPG = RW_GROUP // RW_N
RW_GPS = 4
LORA_ROWS = 384

E_OFF_GATES = 2 * ML_QK + 2 * ML_V
E_OFF_RW = E_OFF_GATES + 2 * ML_HEADS
RW_LANE_OFF = E_OFF_RW % 128

SSD_DI = 4096
SSD_P = 64
SSD_H = SSD_DI // SSD_P
SSD_N = 128
SSD_G = 8
SSD_R = SSD_H // SSD_G
SSD_K = 4
SSD_GW = SSD_DI // SSD_G
SSD_BC = 2 * SSD_G * SSD_N
SSD_DT_PAD = 128

N_EXPERTS = 8
ROUTER_PAD = 128

VMEM_LIMIT = 56 * 1024 * 1024


def _tile(n, prefs):
    for p in prefs:
        if n % p == 0:
            return p
    return n


_ROW_TILES = (1040, 832, 640, 520, 512, 256, 128, 64)


def _params(sem):
    return pltpu.CompilerParams(dimension_semantics=sem, vmem_limit_bytes=VMEM_LIMIT)


def _dot(a, b, prec=None):
    return lax.dot_general(a, b, (((1,), (0,)), ((), ())), precision=prec, preferred_element_type=F32)


def _dot_nt(a, b, prec=None):
    return lax.dot_general(a, b, (((1,), (1,)), ((), ())), precision=prec, preferred_element_type=F32)


def _dot_tn(a, b, prec=None):
    return lax.dot_general(a, b, (((0,), (0,)), ((), ())), precision=prec, preferred_element_type=F32)


def _b(x):
    return x.astype(BF16)


def _bdot(dot, a, b):
    return dot(_b(a), _b(b))


def _dot_split(a, b):
    hi = _b(a)
    lo = _b(a - hi.astype(F32))
    bb = _b(b)
    return _dot(hi, bb) + _dot(lo, bb)


def _sigmoid(x):
    return 1.0 / (1.0 + jnp.exp(-x))


def _softplus(x):
    return jnp.maximum(x, 0.0) + jnp.log1p(jnp.exp(-jnp.abs(x)))


def _silu(x):
    return x * _sigmoid(x)


def _iota(shape, dim):
    return lax.broadcasted_iota(jnp.int32, shape, dim)


def _tri(n):
    return (_iota((n, n), 0) >= _iota((n, n), 1)).astype(F32)


def _interleave(chains):
    chains = list(chains)
    while chains:
        for ch in list(chains):
            try:
                next(ch)
            except StopIteration:
                chains.remove(ch)


def _rmsnorm_kernel(x_ref, w_ref, o_ref):
    x = x_ref[...]
    ms = jnp.mean(x * x, axis=-1, keepdims=True)
    o_ref[...] = (x * lax.rsqrt(ms + EPS) * w_ref[...]).astype(o_ref.dtype)


def rmsnorm(x, w, out_dtype):
    m, d = x.shape
    tm = _tile(m, _ROW_TILES)
    return pl.pallas_call(
        _rmsnorm_kernel,
        grid=(m // tm,),
        in_specs=[pl.BlockSpec((tm, d), lambda i: (i, 0)), pl.BlockSpec((1, d), lambda i: (0, 0))],
        out_specs=pl.BlockSpec((tm, d), lambda i: (i, 0)),
        out_shape=jax.ShapeDtypeStruct((m, d), out_dtype),
        compiler_params=_params(("arbitrary",)),
        name="rmsnorm",
    )(x, w.reshape(1, d))


def _mm_up_kernel(a_ref, wt_ref, o_ref, wb_ref, *, n_valid, last_w):
    j = pl.program_id(0)
    tn = wt_ref.shape[0]

    @pl.when(pl.program_id(1) == 0)
    def _():
        row = j * tn + _iota((tn, 1), 0)
        wt = jnp.where(row < n_valid, wt_ref[...], 0.0)
        wb_ref[...] = _b(wt.T)

    if last_w == tn:
        o_ref[...] = _dot(a_ref[...], wb_ref[...]).astype(o_ref.dtype)
    else:
        @pl.when(j < pl.num_programs(0) - 1)
        def _():
            o_ref[...] = _dot(a_ref[...], wb_ref[...]).astype(o_ref.dtype)

        @pl.when(j == pl.num_programs(0) - 1)
        def _():
            o_ref[:, 0:last_w] = _dot(a_ref[...], wb_ref[:, 0:last_w]).astype(o_ref.dtype)


def mm_up(a, wt, out_dtype, tn):
    m, k = a.shape
    n = wt.shape[0]
    tm = _tile(m, _ROW_TILES)
    nt = pl.cdiv(n, tn)
    last_w = pl.cdiv(n - (nt - 1) * tn, 128) * 128
    return pl.pallas_call(
        functools.partial(_mm_up_kernel, n_valid=n, last_w=last_w),
        grid=(nt, m // tm),
        in_specs=[pl.BlockSpec((tm, k), lambda j, i: (i, 0)), pl.BlockSpec((tn, k), lambda j, i: (j, 0))],
        out_specs=pl.BlockSpec((tm, tn), lambda j, i: (i, j)),
        out_shape=jax.ShapeDtypeStruct((m, (nt - 1) * tn + last_w), out_dtype),
        scratch_shapes=[pltpu.VMEM((k, tn), BF16)],
        compiler_params=_params(("arbitrary", "arbitrary")),
        name="mm_up",
    )(a, wt)


def _swiglu_kernel(a_ref, wg_ref, wu_ref, o_ref, wgb_ref, wub_ref):
    @pl.when(pl.program_id(1) == 0)
    def _():
        wgb_ref[...] = _b(wg_ref[...])
        wub_ref[...] = _b(wu_ref[...])

    a = a_ref[...]
    o_ref[...] = (_silu(_dot(a, wgb_ref[...])) * _dot(a, wub_ref[...])).astype(o_ref.dtype)


def swiglu_up(a, wg, wu):
    m, k = a.shape
    f = wg.shape[-1]
    tm = _tile(m, _ROW_TILES)
    tn = _tile(f, (512, 256, 128))
    w_spec = pl.BlockSpec((k, tn), lambda j, i: (0, j))
    return pl.pallas_call(
        _swiglu_kernel,
        grid=(f // tn, m // tm),
        in_specs=[pl.BlockSpec((tm, k), lambda j, i: (i, 0)), w_spec, w_spec],
        out_specs=pl.BlockSpec((tm, tn), lambda j, i: (i, j)),
        out_shape=jax.ShapeDtypeStruct((m, f), BF16),
        scratch_shapes=[pltpu.VMEM((k, tn), BF16), pltpu.VMEM((k, tn), BF16)],
        compiler_params=_params(("arbitrary", "arbitrary")),
        name="swiglu_up",
    )(a, wg, wu)


def _mm_down_kernel(*refs, tm, seq_t):
    *a_refs, w_ref, r_ref, o_ref = refs
    acc = None
    off = 0
    for a_ref in a_refs:
        k = a_ref.shape[1]
        part = _dot(a_ref[...], w_ref[off:off + k, :])
        acc = part if acc is None else acc + part
        off += k
    t0 = (pl.program_id(0) * tm) % seq_t
    valid = ((t0 + _iota((tm, 1), 0)) >= N_PAD).astype(F32)
    o_ref[...] = r_ref[...] + acc * valid


def mm_down(a_parts, w, res, seq_t):
    m = a_parts[0].shape[0]
    k, n = w.shape
    assert sum(a.shape[1] for a in a_parts) == k
    tm = _tile(seq_t, (520, 512, 256, 128, 64))
    tn = _tile(n, (1024, 512, 256, 128))
    a_specs = [pl.BlockSpec((tm, a.shape[1]), lambda i, j: (i, 0)) for a in a_parts]
    return pl.pallas_call(
        functools.partial(_mm_down_kernel, tm=tm, seq_t=seq_t),
        grid=(m // tm, n // tn),
        in_specs=a_specs + [pl.BlockSpec((k, tn), lambda i, j: (0, j)), pl.BlockSpec((tm, tn), lambda i, j: (i, j))],
        out_specs=pl.BlockSpec((tm, tn), lambda i, j: (i, j)),
        out_shape=jax.ShapeDtypeStruct((m, n), F32),
        compiler_params=_params(("arbitrary", "arbitrary")),
        name="mm_down",
    )(*a_parts, w, res)


def _log_sigmoid(x):
    return -_softplus(-x)


def _mlstm_kernel(q_ref, k_ref, v_ref, o_ref, g_ref, bias_ref, onorm_ref, out_ref, c_ref, n_ref, m_ref):
    c = pl.program_id(0)
    L = CHUNK

    @pl.when(c == 0)
    def _():
        c_ref[...] = jnp.zeros_like(c_ref)
        n_ref[...] = jnp.zeros_like(n_ref)
        m_ref[...] = jnp.zeros_like(m_ref)

    valid = (c * L + _iota((L, 1), 0)) >= N_PAD
    causal = _iota((L, L), 0) >= _iota((L, L), 1)
    tri = _tri(L)
    scale = ML_DQK ** -0.5

    def chain(b, h, bcum, ig, b_t, i_t):
        q = q_ref[b, :, h * ML_DQK:(h + 1) * ML_DQK] * scale
        k = k_ref[b, :, h * ML_DQK:(h + 1) * ML_DQK]
        v = v_ref[b, :, h * ML_DV:(h + 1) * ML_DV]
        og = o_ref[b, :, h * ML_DV:(h + 1) * ML_DV]
        bcol = bcum[:, ML_HEADS + h:ML_HEADS + h + 1]
        icol = ig[:, h:h + 1]
        brow = b_t[ML_HEADS + h:ML_HEADS + h + 1, :]
        irow = i_t[h:h + 1, :]
        g = bcol[L - 1:L, :]
        m_prev = m_ref[b, h:h + 1, 0:1]
        c_prev = c_ref[b, h]
        n_prev = n_ref[b, h:h + 1, :]

        dm = jnp.where(causal, bcol - brow + irow, NEG)
        m_inter = bcol + m_prev
        m_t = jnp.maximum(m_inter, jnp.max(dm, axis=-1, keepdims=True))
        yield
        s = _dot_nt(_b(q), _b(k)) * jnp.exp(dm - m_t)
        w_inter = jnp.exp(m_inter - m_t)
        yield
        num = _dot(_b(s), _b(v)) + w_inter * _dot(_b(q), _b(c_prev))
        den = jnp.sum(s, axis=-1, keepdims=True) + w_inter * jnp.sum(q * n_prev, axis=-1, keepdims=True)
        yield
        hh = num / jnp.maximum(jnp.abs(den), jnp.exp(-m_t))
        hh = hh * lax.rsqrt(jnp.mean(hh * hh, axis=-1, keepdims=True) + EPS)
        hh = hh * onorm_ref[:, h * ML_DV:(h + 1) * ML_DV] * _sigmoid(og)
        out_ref[b, :, h * ML_DV:(h + 1) * ML_DV] = hh.astype(out_ref.dtype)
        yield

        a_col = g - bcol + icol
        m_new = jnp.maximum(g + m_prev, jnp.max(a_col, axis=0, keepdims=True))
        dec = jnp.exp(g + m_prev - m_new)
        wk = jnp.exp(a_col - m_new) * k
        yield
        c_ref[b, h] = dec * c_prev + _dot_tn(_b(wk), _b(v))
        n_ref[b, h:h + 1, :] = dec * n_prev + jnp.sum(wk, axis=0, keepdims=True)
        m_ref[b, h:h + 1, :] = jnp.broadcast_to(m_new, (1, m_ref.shape[2]))

    chains = []
    for b in range(q_ref.shape[0]):
        gates = g_ref[b] + bias_ref[...]
        sc = ML_GATE_CAP * jnp.tanh(gates / ML_GATE_CAP)
        ig = jnp.where(valid, sc, NEG)
        lf = jnp.where(valid, _log_sigmoid(sc), 0.0)
        bcum = _dot(tri, lf, HI)
        chains += [chain(b, h, bcum, ig, bcum.T, ig.T) for h in range(ML_HEADS)]
    _interleave(chains)


def mlstm_group(p, i_bias, f_bias, out_norm):
    batch, seq_t, _ = p.shape
    nc = seq_t // CHUNK
    bias = jnp.zeros((1, 128), F32).at[0, 0:ML_HEADS].set(i_bias).at[0, ML_HEADS:2 * ML_HEADS].set(f_bias)
    return pl.pallas_call(
        _mlstm_kernel,
        grid=(nc,),
        in_specs=[
            pl.BlockSpec((batch, CHUNK, ML_QK), lambda c: (0, c, 0)),
            pl.BlockSpec((batch, CHUNK, ML_QK), lambda c: (0, c, 1)),
            pl.BlockSpec((batch, CHUNK, ML_V), lambda c: (0, c, 1)),
            pl.BlockSpec((batch, CHUNK, ML_V), lambda c: (0, c, 2)),
            pl.BlockSpec((batch, CHUNK, 128), lambda c: (0, c, E_OFF_GATES // 128)),
            pl.BlockSpec((1, 128), lambda c: (0, 0)),
            pl.BlockSpec((1, ML_V), lambda c: (0, 0)),
        ],
        out_specs=pl.BlockSpec((batch, CHUNK, ML_V), lambda c: (0, c, 0)),
        out_shape=jax.ShapeDtypeStruct((batch, seq_t, ML_V), BF16),
        scratch_shapes=[pltpu.VMEM((batch, ML_HEADS, ML_DQK, ML_DV), F32), pltpu.VMEM((batch, 8, ML_DQK), F32),
                        pltpu.VMEM((batch, 8, 128), F32)],
        compiler_params=_params(("arbitrary",)),
        name="mlstm",
    )(p, p, p, p, p, bias, out_norm.reshape(1, ML_V))


def _rwkv_kernel(*refs):
    nb = 2 * RW_GPS + 1
    r_blk, k_blk, v_blk = refs[0:nb], refs[nb:2 * nb], refs[2 * nb:3 * nb]
    tail_ref, mu3_ref, mut_ref, chan_ref, wl_ref, out_ref, h_ref, cr_ref, ck_ref, cv_ref, ct_ref = refs[3 * nb:]
    c = pl.program_id(1)
    L = CHUNK
    G = RW_GROUP

    @pl.when(c == 0)
    def _():
        h_ref[...] = jnp.zeros_like(h_ref)
        cr_ref[...] = jnp.zeros_like(cr_ref)
        ck_ref[...] = jnp.zeros_like(ck_ref)
        cv_ref[...] = jnp.zeros_like(cv_ref)
        ct_ref[...] = jnp.zeros_like(ct_ref)

    row = _iota((L, 1), 0)
    valid = (c * L + row) >= N_PAD

    ri = _iota((G, G), 0)
    ci = _iota((G, G), 1)
    same = (ri >> 6) == (ci >> 6)
    tl = ri & (RW_N - 1)
    sl = ci & (RW_N - 1)
    bd_mask = same.astype(F32)
    eye = (ri == ci).astype(F32)
    lower = same & (tl >= sl)
    strict = same & (tl > sl)
    ones_bd = _b(bd_mask)
    tri = _tri(L)
    lane = _iota((L, LORA_ROWS), 1)

    def head_sum(x):
        hi = _b(x)
        return _dot(hi, ones_bd) + _dot(_b(x - hi.astype(F32)), ones_bd)

    def bd(x):
        return _b(jnp.concatenate([x] * RW_HPG, axis=0) * bd_mask)

    def shift_lerp(x, carry_ref, idx, mu):
        prev = carry_ref[idx]
        xs = jnp.where(row == 0, prev, pltpu.roll(x, 1, 0))
        carry_ref[idx] = x[L - 1:L, :]
        return x + mu * (xs - x)

    def unshift(blocks, b, j):
        x = jnp.concatenate([blk[b] for blk in blocks[2 * j:2 * j + 3]], axis=1)
        return x[:, RW_LANE_OFF:RW_LANE_OFF + G]

    def chain(b, j):
        gl = slice(j * G, (j + 1) * G)
        w0 = chan_ref[0:1, gl]
        a0 = chan_ref[1:2, gl]
        k_k = chan_ref[2:3, gl]
        k_a = chan_ref[3:4, gl]
        r_k = chan_ref[4:5, gl]
        ln_w = chan_ref[5:6, gl]
        ln_b = chan_ref[6:7, gl]
        r = shift_lerp(unshift(r_blk, b, j), cr_ref, (b, j), mu3_ref[0:1, gl])
        k = shift_lerp(unshift(k_blk, b, j), ck_ref, (b, j), mu3_ref[1:2, gl])
        v = shift_lerp(unshift(v_blk, b, j), cv_ref, (b, j), mu3_ref[2:3, gl])
        tail = shift_lerp(tail_ref[b], ct_ref, (b, j), mut_ref[...])

        lo1 = RW_LANE_OFF + RW_DECAY_LORA
        lo2 = lo1 + RW_A_LORA
        lo3 = lo2 + RW_GATE_LORA
        act = jnp.where(lane < RW_LANE_OFF, 0.0,
                        jnp.where(lane < lo1, jnp.tanh(tail),
                                  jnp.where(lane < lo2, tail, jnp.where(lane < lo3, _sigmoid(tail), 0.0))))
        lw = _dot(act[:, 0:128], wl_ref[0, 0:128, gl], HI)
        la = _bdot(_dot, act[:, 0:256], wl_ref[1, 0:256, gl])
        gate = _bdot(_dot, act[:, 128:384], wl_ref[2, 128:384, gl])
        yield

        w_log = -_softplus(-(w0 + lw)) - 0.5
        ld = jnp.where(valid, -jnp.exp(w_log), 0.0)
        a = _sigmoid(a0 + la)
        kk = k * k_k
        kk = kk * lax.rsqrt(jnp.maximum(head_sum(kk * kk), 1e-24))
        k2 = k * (1.0 + (a - 1.0) * k_a)
        k2 = jnp.where(valid, k2, 0.0)
        kk = jnp.where(valid, kk, 0.0)
        va = -kk
        vb = kk * a

        yield
        cum = _dot(tri, ld, HI)
        cum_l = cum[L - 1:L, :]
        w_t = jnp.exp(cum)
        w_inv = jnp.exp(-cum)
        w_prev = jnp.exp(cum - ld)
        w_end = jnp.exp(cum_l - cum)
        w_last = jnp.exp(cum_l)

        a_bd = bd(va * w_prev)
        r_bd = bd(r * w_t)
        b_bd = bd(vb * w_inv)
        k_bd = bd(k2 * w_inv)
        bl_bd = bd(vb * w_end)
        kl_bd = bd(k2 * w_end)
        v_bd = bd(v)
        yield

        n_ab = jnp.where(strict, _dot_nt(a_bd, b_bd), 0.0)
        a_ak = _b(jnp.where(strict, _dot_nt(a_bd, k_bd), 0.0))
        m_rb = _b(jnp.where(lower, _dot_nt(r_bd, b_bd), 0.0))
        m_rk = _b(jnp.where(lower, _dot_nt(r_bd, k_bd), 0.0))
        yield

        t_inv = eye + n_ab
        pw = n_ab
        for _ in range(5):
            pw_b = _b(pw)
            pw = _dot(pw_b, pw_b)
            yield
            t_inv = t_inv + _bdot(_dot, t_inv, pw)
            yield

        t_b = _b(t_inv)
        av = _dot(a_ak, v_bd)
        yield
        a_p = _b(_dot(t_b, a_bd))
        u0 = _b(_dot(t_b, _b(av)))
        yield
        p_mat = eye * w_last + _dot_tn(bl_bd, a_p)
        q_mat = _dot_tn(bl_bd, u0) + _dot_tn(kl_bd, v_bd)
        r_p = r_bd.astype(F32) + _dot(m_rb, a_p)
        y0 = _dot(m_rb, u0) + _dot(m_rk, v_bd)
        yield

        h0 = _b(h_ref[b, j])
        y_bd = _dot(_b(r_p), h0) + y0
        h_ref[b, j] = _dot(_b(p_mat), h0) + q_mat
        y = y_bd[0:L] + y_bd[L:2 * L] + y_bd[2 * L:3 * L] + y_bd[3 * L:4 * L]
        yield

        inv_n = 1.0 / RW_N
        mean = head_sum(y) * inv_n
        d = y - mean
        var = head_sum(d * d) * inv_n
        yn = d * lax.rsqrt(var + RW_LN_EPS) * ln_w + ln_b
        bonus = head_sum(r * k2 * r_k) * v
        out_ref[b, :, gl] = ((yn + bonus) * gate).astype(out_ref.dtype)

    _interleave(chain(b, j) for b in range(out_ref.shape[0]) for j in range(RW_GPS))


def rwkv7_group(p, mu, w0, w_up, a0, a_up, g_up, k_k, k_a, r_k, ln_w, ln_b):
    batch, seq_t, _ = p.shape
    nc = seq_t // CHUNK
    ng = RW_DIM // RW_GROUP
    n_lora = RW_DECAY_LORA + RW_A_LORA + RW_GATE_LORA
    o = RW_LANE_OFF
    mu3 = mu[:3 * RW_DIM].reshape(3, RW_DIM)
    mut = jnp.zeros((1, LORA_ROWS), F32).at[0, o:o + n_lora].set(mu[3 * RW_DIM:])
    chan = jnp.stack([w0, a0, k_k, k_a, r_k.reshape(RW_DIM), ln_w, ln_b, jnp.zeros((RW_DIM,), F32)])
    wl = jnp.zeros((3, LORA_ROWS, RW_DIM), F32)
    wl = wl.at[0, o:o + RW_DECAY_LORA].set(w_up)
    wl = wl.at[1, o + RW_DECAY_LORA:o + RW_DECAY_LORA + RW_A_LORA].set(a_up)
    wl = wl.at[2, o + RW_DECAY_LORA + RW_A_LORA:o + n_lora].set(g_up)

    gw = RW_GPS * RW_GROUP

    def col(seg, part):
        first = (E_OFF_RW + seg * RW_DIM - o) // 128
        return pl.BlockSpec((batch, CHUNK, 128), lambda g, c: (0, c, first + 2 * RW_GPS * g + part))

    nb = 2 * RW_GPS + 1
    cols = [col(seg, part) for seg in range(3) for part in range(nb)]
    carry = lambda width: pltpu.VMEM((batch, RW_GPS, 1, width), F32)
    return pl.pallas_call(
        _rwkv_kernel,
        grid=(ng // RW_GPS, nc),
        in_specs=cols + [
            pl.BlockSpec((batch, CHUNK, LORA_ROWS), lambda g, c: (0, c, (E_OFF_RW + 3 * RW_DIM - o) // LORA_ROWS)),
            pl.BlockSpec((3, gw), lambda g, c: (0, g)),
            pl.BlockSpec((1, LORA_ROWS), lambda g, c: (0, 0)),
            pl.BlockSpec((8, gw), lambda g, c: (0, g)),
            pl.BlockSpec((3, LORA_ROWS, gw), lambda g, c: (0, 0, g)),
        ],
        out_specs=pl.BlockSpec((batch, CHUNK, gw), lambda g, c: (0, c, g)),
        out_shape=jax.ShapeDtypeStruct((batch, seq_t, RW_DIM), BF16),
        scratch_shapes=[pltpu.VMEM((batch, RW_GPS, RW_GROUP, RW_GROUP), F32), carry(RW_GROUP), carry(RW_GROUP),
                        carry(RW_GROUP), carry(LORA_ROWS)],
        compiler_params=_params(("arbitrary", "arbitrary")),
        name="rwkv7",
    )(*([p] * (3 * nb + 1)), mu3, mut, chan, wl)


def _ssd_kernel(z_ref, xs_ref, bc_ref, dt_ref, cwx_ref, cwb_ref, cbx_ref, cbb_ref, hp_ref, dexp_ref, nw_ref,
                out_ref, s_ref, px_ref, pb_ref):
    c = pl.program_id(0)
    L = CHUNK

    @pl.when(c == 0)
    def _():
        s_ref[...] = jnp.zeros_like(s_ref)
        px_ref[...] = jnp.zeros_like(px_ref)
        pb_ref[...] = jnp.zeros_like(pb_ref)

    row8 = _iota((8, 1), 0)
    valid = (c * L + _iota((L, 1), 0)) >= N_PAD
    lane = _iota((L, 2 * SSD_P), 1)
    causal = _iota((L, L), 0) >= _iota((L, L), 1)
    tri = _tri(L)

    def conv_silu(x, prev_ref, b, w_ref, b_ref):
        prev = prev_ref[b]
        acc = x * w_ref[SSD_K - 1:SSD_K, :] + b_ref[...]
        for j in range(1, SSD_K):
            xr = pltpu.roll(x, j, 0)
            pr = pltpu.roll(prev, j, 0)
            top = jnp.where(row8 < j, pr, xr[0:8])
            sh = jnp.concatenate([top, xr[8:]], axis=0)
            acc = acc + sh * w_ref[SSD_K - 1 - j:SSD_K - j, :]
        prev_ref[b] = x[L - 8:L, :]
        return _silu(acc)

    def expand(x):
        parts = []
        for j in range(SSD_H // 2):
            lo = jnp.broadcast_to(x[:, 2 * j:2 * j + 1], (L, 2 * SSD_P))
            hi = jnp.broadcast_to(x[:, 2 * j + 1:2 * j + 2], (L, 2 * SSD_P))
            parts.append(jnp.where(lane < SSD_P, lo, hi))
        return jnp.concatenate(parts, axis=1)

    def chain(b):
        xs = conv_silu(xs_ref[b], px_ref, b, cwx_ref, cbx_ref)
        yield
        bc = conv_silu(bc_ref[b], pb_ref, b, cwb_ref, cbb_ref)
        yield

        dt = jnp.where(valid, _softplus(dt_ref[b] + hp_ref[0:1, :]), 0.0)
        a = dt * (-jnp.exp(hp_ref[1:2, :]))
        acum = _dot(tri, a, HI)
        a_t = acum.T
        dt_t = dt.T
        a_last = acum[L - 1:L, :]
        cdec = jnp.exp(a_last)
        yield
        xw = xs * expand(dt * jnp.exp(a_last - acum))
        yield
        e_ea = expand(jnp.exp(acum))
        yield

        ys = []
        for g in range(SSD_G):
            bg = bc[:, g * SSD_N:(g + 1) * SSD_N]
            cg = bc[:, SSD_G * SSD_N + g * SSD_N:SSD_G * SSD_N + (g + 1) * SSD_N]
            cb = _dot_nt(_b(cg), _b(bg))
            s_prev = s_ref[b, g]
            y_inter = _dot_nt(_b(cg), _b(s_prev)) * e_ea[:, g * SSD_GW:(g + 1) * SSD_GW]
            yield
            parts = []
            cd_rows = []
            for r in range(SSD_R):
                h = g * SSD_R + r
                seg = acum[:, h:h + 1] - a_t[h:h + 1, :]
                ldec = jnp.exp(jnp.where(causal, seg, NEG))
                parts.append(_dot(_b(cb * ldec * dt_t[h:h + 1, :]), _b(xs[:, h * SSD_P:(h + 1) * SSD_P])))
                cd_rows.append(jnp.broadcast_to(cdec[:, h:h + 1], (SSD_P, SSD_N)))
                if r % 2:
                    yield
            ys.append(jnp.concatenate(parts, axis=1) + y_inter)
            st = _dot_tn(_b(xw[:, g * SSD_GW:(g + 1) * SSD_GW]), _b(bg))
            s_ref[b, g] = jnp.concatenate(cd_rows, axis=0) * s_prev + st
            yield

        y = jnp.concatenate(ys, axis=1) + dexp_ref[...] * xs
        y = y * _silu(z_ref[b])
        yield
        outs = []
        for g in range(SSD_G):
            yg = y[:, g * SSD_GW:(g + 1) * SSD_GW]
            outs.append(yg * lax.rsqrt(jnp.mean(yg * yg, axis=-1, keepdims=True) + EPS))
        out_ref[b] = (jnp.concatenate(outs, axis=1) * nw_ref[...]).astype(out_ref.dtype)

    _interleave(chain(b) for b in range(out_ref.shape[0]))


def ssd_mixer(p, conv_w, conv_b, dt_bias, a_log, d_skip, norm_w):
    batch, seq_t, _ = p.shape
    nc = seq_t // CHUNK
    hp = jnp.zeros((8, SSD_DT_PAD), F32).at[0, :SSD_H].set(dt_bias).at[1, :SSD_H].set(a_log)
    dexp = jnp.repeat(d_skip, SSD_P).reshape(1, SSD_DI)
    conv_b = conv_b.reshape(1, -1)
    return pl.pallas_call(
        _ssd_kernel,
        grid=(nc,),
        in_specs=[
            pl.BlockSpec((batch, CHUNK, SSD_DI), lambda c: (0, c, 0)),
            pl.BlockSpec((batch, CHUNK, SSD_DI), lambda c: (0, c, 1)),
            pl.BlockSpec((batch, CHUNK, SSD_BC), lambda c: (0, c, 2 * SSD_DI // SSD_BC)),
            pl.BlockSpec((batch, CHUNK, SSD_DT_PAD), lambda c: (0, c, (2 * SSD_DI + SSD_BC) // SSD_DT_PAD)),
            pl.BlockSpec((SSD_K, SSD_DI), lambda c: (0, 0)),
            pl.BlockSpec((SSD_K, SSD_BC), lambda c: (0, SSD_DI // SSD_BC)),
            pl.BlockSpec((1, SSD_DI), lambda c: (0, 0)),
            pl.BlockSpec((1, SSD_BC), lambda c: (0, SSD_DI // SSD_BC)),
            pl.BlockSpec((8, SSD_DT_PAD), lambda c: (0, 0)),
            pl.BlockSpec((1, SSD_DI), lambda c: (0, 0)),
            pl.BlockSpec((1, SSD_DI), lambda c: (0, 0)),
        ],
        out_specs=pl.BlockSpec((batch, CHUNK, SSD_DI), lambda c: (0, c, 0)),
        out_shape=jax.ShapeDtypeStruct((batch, seq_t, SSD_DI), BF16),
        scratch_shapes=[pltpu.VMEM((batch, SSD_G, SSD_GW, SSD_N), F32), pltpu.VMEM((batch, 8, SSD_DI), F32),
                        pltpu.VMEM((batch, 8, SSD_BC), F32)],
        compiler_params=_params(("arbitrary",)),
        name="ssd",
    )(p, p, p, p, conv_w, conv_w, conv_b, conv_b, hp, dexp, norm_w.reshape(1, SSD_DI))


MOE_TILE = 512
ROUTE_W1, ROUTE_W2, ROUTE_I1, ROUTE_I2 = 0, 1, 2, 3


def _router_kernel(x_ref, w_ref, r_ref, o_ref):
    x = x_ref[...]
    hn = x * lax.rsqrt(jnp.mean(x * x, axis=-1, keepdims=True) + EPS) * w_ref[...]
    logits = _dot(hn, r_ref[...], HI)
    lane = _iota(logits.shape, 1)
    lane_f = lane.astype(F32)
    logits = jnp.where(lane < N_EXPERTS, logits, -jnp.inf)
    m1 = jnp.max(logits, axis=-1, keepdims=True)
    i1 = jnp.min(jnp.where(logits == m1, lane_f, float(ROUTER_PAD)), axis=-1, keepdims=True)
    rest = jnp.where(lane_f == i1, -jnp.inf, logits)
    m2 = jnp.max(rest, axis=-1, keepdims=True)
    i2 = jnp.min(jnp.where(rest == m2, lane_f, float(ROUTER_PAD)), axis=-1, keepdims=True)
    e2 = jnp.exp(m2 - m1)
    w1 = 1.0 / (1.0 + e2)
    w2 = e2 / (1.0 + e2)
    o_ref[...] = jnp.where(lane == ROUTE_W1, w1, jnp.where(lane == ROUTE_W2, w2,
                           jnp.where(lane == ROUTE_I1, i1, jnp.where(lane == ROUTE_I2, i2, 0.0))))


def router_top2(x, w, router):
    m, d = x.shape
    tm = _tile(m, (520, 512, 256, 128, 64))
    rpad = jnp.zeros((d, ROUTER_PAD), F32).at[:, :N_EXPERTS].set(router)
    return pl.pallas_call(
        _router_kernel,
        grid=(m // tm,),
        in_specs=[pl.BlockSpec((tm, d), lambda i: (i, 0)), pl.BlockSpec((1, d), lambda i: (0, 0)),
                  pl.BlockSpec((d, ROUTER_PAD), lambda i: (0, 0))],
        out_specs=pl.BlockSpec((tm, ROUTER_PAD), lambda i: (i, 0)),
        out_shape=jax.ShapeDtypeStruct((m, ROUTER_PAD), F32),
        compiler_params=_params(("arbitrary",)),
        name="router_top2",
    )(x, w.reshape(1, d), rpad)


def _row_copy(src_hbm, row, dst, r, sem):
    return pltpu.make_async_copy(src_hbm.at[pl.ds(row, 1)], dst.at[pl.ds(r, 1)], sem)


def _gather_norm_kernel(nused_ref, idx_ref, h_hbm, w_ref, o_ref, buf, sem):
    t = pl.program_id(0)
    tg = buf.shape[1]
    n_used = nused_ref[0]

    def issue(tile, slot):
        def body(r2, carry):
            for u in range(2):
                r = 2 * r2 + u
                _row_copy(h_hbm, idx_ref[tile * tg + r], buf.at[slot], r, sem.at[slot]).start(priority=u)
            return carry

        lax.fori_loop(0, tg // 2, body, 0, unroll=4)

    @pl.when(jnp.logical_and(t == 0, n_used > 0))
    def _():
        issue(0, 0)

    @pl.when(t + 1 < n_used)
    def _():
        issue(t + 1, (t + 1) % 2)

    @pl.when(t < n_used)
    def _():
        slot = t % 2
        pltpu.make_async_copy(h_hbm.at[pl.ds(0, tg)], buf.at[slot], sem.at[slot]).wait()
        x = buf[slot]
        hn = x * lax.rsqrt(jnp.mean(x * x, axis=-1, keepdims=True) + EPS) * w_ref[...]
        o_ref[...] = hn.astype(o_ref.dtype)

    @pl.when(t >= n_used)
    def _():
        o_ref[...] = jnp.zeros_like(o_ref)


def gather_norm(h, w, src, n_used, n_tiles):
    m, d = h.shape
    tg = MOE_TILE
    return pl.pallas_call(
        _gather_norm_kernel,
        grid_spec=pltpu.PrefetchScalarGridSpec(
            num_scalar_prefetch=2,
            grid=(n_tiles,),
            in_specs=[pl.BlockSpec(memory_space=pl.ANY), pl.BlockSpec((1, d), lambda t, *_: (0, 0))],
            out_specs=pl.BlockSpec((tg, d), lambda t, *_: (t, 0)),
            scratch_shapes=[pltpu.VMEM((2, tg, d), F32), pltpu.SemaphoreType.DMA((2,))],
        ),
        out_shape=jax.ShapeDtypeStruct((n_tiles * tg, d), BF16),
        compiler_params=_params(("arbitrary",)),
        name="moe_gather",
    )(n_used, src, h, w.reshape(1, d))


def _new_expert(te_ref, t):
    return jnp.logical_or(t == 0, te_ref[t] != te_ref[jnp.maximum(t - 1, 0)])


def _used_tile(t, nu):
    return jnp.minimum(t, nu[0] - 1)


def _moe_up_kernel(te_ref, nused_ref, a_ref, wg_ref, wu_ref, o_ref, wgb_ref, wub_ref):
    t = pl.program_id(1)

    @pl.when(_new_expert(te_ref, t))
    def _():
        wgb_ref[...] = _b(wg_ref[...])
        wub_ref[...] = _b(wu_ref[...])

    @pl.when(t < nused_ref[0])
    def _():
        a = a_ref[...]
        o_ref[...] = (_silu(_dot(a, wgb_ref[...])) * _dot(a, wub_ref[...])).astype(o_ref.dtype)

    @pl.when(t >= nused_ref[0])
    def _():
        o_ref[...] = jnp.zeros_like(o_ref)


def moe_up(xs, wg, wu, tile_e, n_used):
    p, k = xs.shape
    f = wg.shape[-1]
    tg = MOE_TILE
    tn = _tile(f, (512, 256, 128))
    w_spec = pl.BlockSpec((None, k, tn), lambda j, t, te, nu: (te[t], 0, j))
    return pl.pallas_call(
        _moe_up_kernel,
        grid_spec=pltpu.PrefetchScalarGridSpec(
            num_scalar_prefetch=2,
            grid=(f // tn, p // tg),
            in_specs=[pl.BlockSpec((tg, k), lambda j, t, te, nu: (_used_tile(t, nu), 0)), w_spec, w_spec],
            out_specs=pl.BlockSpec((tg, tn), lambda j, t, te, nu: (t, j)),
            scratch_shapes=[pltpu.VMEM((k, tn), BF16), pltpu.VMEM((k, tn), BF16)],
        ),
        out_shape=jax.ShapeDtypeStruct((p, f), BF16),
        compiler_params=_params(("arbitrary", "arbitrary")),
        name="moe_up",
    )(tile_e, n_used, xs, wg, wu)


def _moe_down_kernel(te_ref, nused_ref, a_ref, w_ref, o_ref, wb_ref):
    t = pl.program_id(1)

    @pl.when(_new_expert(te_ref, t))
    def _():
        wb_ref[...] = _b(w_ref[...])

    @pl.when(t < nused_ref[0])
    def _():
        o_ref[...] = _dot(a_ref[...], wb_ref[...])

    @pl.when(t >= nused_ref[0])
    def _():
        o_ref[...] = jnp.zeros_like(o_ref)


def moe_down(fs, wd, tile_e, n_used):
    p, f = fs.shape
    d = wd.shape[-1]
    tg = MOE_TILE
    tn = _tile(d, (512, 256, 128))
    return pl.pallas_call(
        _moe_down_kernel,
        grid_spec=pltpu.PrefetchScalarGridSpec(
            num_scalar_prefetch=2,
            grid=(d // tn, p // tg),
            in_specs=[pl.BlockSpec((tg, f), lambda j, t, te, nu: (_used_tile(t, nu), 0)),
                      pl.BlockSpec((None, f, tn), lambda j, t, te, nu: (te[t], 0, j))],
            out_specs=pl.BlockSpec((tg, tn), lambda j, t, te, nu: (t, j)),
            scratch_shapes=[pltpu.VMEM((f, tn), BF16)],
        ),
        out_shape=jax.ShapeDtypeStruct((p, d), F32),
        compiler_params=_params(("arbitrary", "arbitrary")),
        name="moe_down",
    )(tile_e, n_used, fs, wd)


def _combine_norm_kernel(pos_ref, h_hbm, route_hbm, y_hbm, w_ref, o_ref, hbuf, rbuf, buf0, buf1, sem,
                         *, seq_t, m, tiles_per_seq):
    s = pl.program_id(0)
    tm = hbuf.shape[1]

    def first_row(step):
        return (step // tiles_per_seq) * seq_t + (N_PAD + N_META) + (step % tiles_per_seq) * tm

    def copies(base, slot):
        return (pltpu.make_async_copy(h_hbm.at[pl.ds(base, tm)], hbuf.at[slot], sem.at[slot, 0]),
                pltpu.make_async_copy(route_hbm.at[pl.ds(base, tm)], rbuf.at[slot], sem.at[slot, 1]))

    def issue(step, slot):
        base = first_row(step)
        for cp in copies(base, slot):
            cp.start()

        def body(r, carry):
            _row_copy(y_hbm, pos_ref[base + r], buf0.at[slot], r, sem.at[slot, 2]).start(priority=0)
            _row_copy(y_hbm, pos_ref[m + base + r], buf1.at[slot], r, sem.at[slot, 3]).start(priority=1)
            return carry

        lax.fori_loop(0, tm, body, 0, unroll=4)

    @pl.when(s == 0)
    def _():
        issue(0, 0)

    @pl.when(s + 1 < pl.num_programs(0))
    def _():
        issue(s + 1, (s + 1) % 2)

    slot = s % 2
    for cp in copies(0, slot):
        cp.wait()
    pltpu.make_async_copy(y_hbm.at[pl.ds(0, tm)], buf0.at[slot], sem.at[slot, 2]).wait()
    pltpu.make_async_copy(y_hbm.at[pl.ds(0, tm)], buf1.at[slot], sem.at[slot, 3]).wait()
    w1 = rbuf[slot, :, ROUTE_W1:ROUTE_W1 + 1]
    w2 = rbuf[slot, :, ROUTE_W2:ROUTE_W2 + 1]
    x = hbuf[slot] + (w1 * buf0[slot] + w2 * buf1[slot])
    o_ref[...] = (x * lax.rsqrt(jnp.mean(x * x, axis=-1, keepdims=True) + EPS) * w_ref[...]).astype(o_ref.dtype)


def combine_norm(h, route, ys, pos, w, batch, seq_t):
    m, d = h.shape
    seq = seq_t - N_PAD - N_META
    tm = _tile(seq, (256, 128, 64))
    any_spec = pl.BlockSpec(memory_space=pl.ANY)
    return pl.pallas_call(
        functools.partial(_combine_norm_kernel, seq_t=seq_t, m=m, tiles_per_seq=seq // tm),
        grid_spec=pltpu.PrefetchScalarGridSpec(
            num_scalar_prefetch=1,
            grid=(batch * seq // tm,),
            in_specs=[any_spec, any_spec, any_spec, pl.BlockSpec((1, d), lambda s, *_: (0, 0))],
            out_specs=pl.BlockSpec((tm, d), lambda s, *_: (s, 0)),
            scratch_shapes=[pltpu.VMEM((2, tm, d), F32), pltpu.VMEM((2, tm, ROUTER_PAD), F32),
                            pltpu.VMEM((2, tm, d), F32), pltpu.VMEM((2, tm, d), F32),
                            pltpu.SemaphoreType.DMA((2, 4))],
        ),
        out_shape=jax.ShapeDtypeStruct((batch * seq, d), F32),
        compiler_params=_params(("arbitrary",)),
        name="moe_combine",
    )(pos, h, route, ys, w.reshape(1, d))


def moe_routing(route, batch, seq_t):
    m = batch * seq_t
    tg = MOE_TILE
    n_tiles = (2 * m + tg - 1) // tg + N_EXPERTS
    e_flat = jnp.concatenate([route[:, ROUTE_I1], route[:, ROUTE_I2]]).astype(jnp.int32)
    tok = jnp.concatenate([jnp.arange(m, dtype=jnp.int32)] * 2)
    real = (tok % seq_t) >= (N_PAD + N_META)
    onehot = ((e_flat[:, None] == jnp.arange(N_EXPERTS, dtype=jnp.int32)[None, :]) & real[:, None]).astype(jnp.int32)
    csum = jnp.cumsum(onehot, axis=0)
    rank = jnp.sum(csum * onehot, axis=1) - 1
    counts = csum[-1]
    padded = ((counts + tg - 1) // tg) * tg
    ends = jnp.cumsum(padded)
    starts = ends - padded
    pos = jnp.sum(starts[None, :] * onehot, axis=1) + rank
    pos = jnp.where(real, pos, n_tiles * tg)
    src = jnp.zeros((n_tiles * tg,), jnp.int32).at[pos].set(tok, mode="drop")
    n_used = (ends[-1] // tg).astype(jnp.int32)
    tile_start = jnp.arange(n_tiles, dtype=jnp.int32) * tg
    tile_e = jnp.sum((tile_start[:, None] >= ends[None, :]).astype(jnp.int32), axis=1)
    last_e = jnp.sum((((n_used - 1) * tg) >= ends).astype(jnp.int32))
    tile_e = jnp.where(jnp.arange(n_tiles) < n_used, tile_e, last_e)
    return src, pos, tile_e, n_used.reshape(1), n_tiles


def kernel(x, meta_tokens, e_norm_mix, e_w_in, ml_i_bias, ml_f_bias, ml_out_norm, rw_mu, rw_w0, rw_w_up, rw_a0, rw_a_up, rw_g_up, rw_k_k, rw_k_a, rw_r_k, rw_ln_w, rw_ln_b, e_w_out, e_norm_ffn, ffn_w_gate, ffn_w_up, ffn_w_down, o_norm_mix, ssd_w_in, ssd_conv_w, ssd_conv_b, ssd_dt_bias, ssd_A_log, ssd_D, ssd_norm, ssd_w_out, o_norm_ffn, moe_router, moe_w_gate, moe_w_up, moe_w_down, final_norm):
    batch, seq, d = x.shape
    seq_t = N_PAD + N_META + seq
    m = batch * seq_t
    h = jnp.concatenate([jnp.zeros((batch, N_PAD, d), x.dtype),
                         jnp.broadcast_to(meta_tokens.astype(x.dtype), (batch, N_META, d)), x], axis=1)
    h = h.reshape(m, d)

    hn = rmsnorm(h, e_norm_mix[0], BF16)
    p = mm_up(hn, e_w_in[0].T, F32, 512).reshape(batch, seq_t, -1)
    y_ml = mlstm_group(p, ml_i_bias[0], ml_f_bias[0], ml_out_norm[0]).reshape(m, ML_V)
    y_rw = rwkv7_group(p, rw_mu[0], rw_w0[0], rw_w_up[0], rw_a0[0], rw_a_up[0], rw_g_up[0], rw_k_k[0],
                       rw_k_a[0], rw_r_k[0], rw_ln_w[0], rw_ln_b[0]).reshape(m, RW_DIM)
    h = mm_down([y_ml, y_rw], _b(e_w_out[0]), h, seq_t)
    hn = rmsnorm(h, e_norm_ffn[0], BF16)
    f = swiglu_up(hn, ffn_w_gate[0], ffn_w_up[0])
    h = mm_down([f], _b(ffn_w_down[0]), h, seq_t)

    hn = rmsnorm(h, o_norm_mix[0], BF16)
    p = mm_up(hn, ssd_w_in[0].T, F32, 1024).reshape(batch, seq_t, -1)
    y = ssd_mixer(p, ssd_conv_w[0], ssd_conv_b[0], ssd_dt_bias[0], ssd_A_log[0], ssd_D[0], ssd_norm[0])
    h = mm_down([y.reshape(m, SSD_DI)], _b(ssd_w_out[0]), h, seq_t)
    route = router_top2(h, o_norm_ffn[0], moe_router[0])
    src, pos, tile_e, n_used, n_tiles = moe_routing(route, batch, seq_t)
    xs = gather_norm(h, o_norm_ffn[0], src, n_used, n_tiles)
    fs = moe_up(xs, moe_w_gate[0], moe_w_up[0], tile_e, n_used)
    ys = moe_down(fs, moe_w_down[0], tile_e, n_used)
    out = combine_norm(h, route, ys, pos, final_norm, batch, seq_t)
    return out.reshape(batch, seq, d).astype(x.dtype)
```

```python
import functools

import jax
import jax.numpy as jnp
from jax import lax
from jax.experimental import pallas as pl
from jax.experimental.pallas import tpu as pltpu

F32 = jnp.float32
BF16 = jnp.bfloat16
HI = lax.Precision.HIGHEST

N_META = 16
CHUNK = 64
N_PAD = CHUNK - N_META
NEG = -1e30
EPS = 1e-6

ML_HEADS = 4
ML_DQK = 128
ML_DV = 256
ML_GATE_CAP = 15.0
ML_QK = ML_HEADS * ML_DQK
ML_V = ML_HEADS * ML_DV

RW_HEADS = 16
RW_N = 64
RW_DIM = RW_HEADS * RW_N
RW_DECAY_LORA = 64
RW_A_LORA = 64
RW_GATE_LORA = 160
RW_LN_EPS = 64e-5
RW_GROUP = 256
RW_HPG = RW_GROUP // RW_N
RW_GPS = 4
LORA_ROWS = 384

E_OFF_GATES = 2 * ML_QK + 2 * ML_V
E_OFF_RW = E_OFF_GATES + 2 * ML_HEADS
RW_LANE_OFF = E_OFF_RW % 128

SSD_DI = 4096
SSD_P = 64
SSD_H = SSD_DI // SSD_P
SSD_N = 128
SSD_G = 8
SSD_R = SSD_H // SSD_G
SSD_K = 4
SSD_GW = SSD_DI // SSD_G
SSD_BC = 2 * SSD_G * SSD_N
SSD_DT_PAD = 128

N_EXPERTS = 8
ROUTER_PAD = 128

VMEM_LIMIT = 56 * 1024 * 1024


def _tile(n, prefs):
    for p in prefs:
        if n % p == 0:
            return p
    return n


_ROW_TILES = (1040, 832, 640, 520, 512, 256, 128, 64)


def _params(sem):
    return pltpu.CompilerParams(dimension_semantics=sem, vmem_limit_bytes=VMEM_LIMIT)


def _dot(a, b, prec=None):
    return lax.dot_general(a, b, (((1,), (0,)), ((), ())), precision=prec, preferred_element_type=F32)


def _dot_nt(a, b, prec=None):
    return lax.dot_general(a, b, (((1,), (1,)), ((), ())), precision=prec, preferred_element_type=F32)


def _dot_tn(a, b, prec=None):
    return lax.dot_general(a, b, (((0,), (0,)), ((), ())), precision=prec, preferred_element_type=F32)


def _b(x):
    return x.astype(BF16)


def _bdot(dot, a, b):
    return dot(_b(a), _b(b))


def _dot_split(a, b):
    hi = _b(a)
    lo = _b(a - hi.astype(F32))
    bb = _b(b)
    return _dot(hi, bb) + _dot(lo, bb)


def _sigmoid(x):
    return 1.0 / (1.0 + jnp.exp(-x))


def _softplus(x):
    return jnp.maximum(x, 0.0) + jnp.log1p(jnp.exp(-jnp.abs(x)))


def _silu(x):
    return x * _sigmoid(x)


def _iota(shape, dim):
    return lax.broadcasted_iota(jnp.int32, shape, dim)


def _tri(n):
    return (_iota((n, n), 0) >= _iota((n, n), 1)).astype(F32)


def _interleave(chains):
    chains = list(chains)
    while chains:
        for ch in list(chains):
            try:
                next(ch)
            except StopIteration:
                chains.remove(ch)


def _rmsnorm_kernel(x_ref, w_ref, o_ref):
    x = x_ref[...]
    ms = jnp.mean(x * x, axis=-1, keepdims=True)
    o_ref[...] = (x * lax.rsqrt(ms + EPS) * w_ref[...]).astype(o_ref.dtype)


def rmsnorm(x, w, out_dtype):
    m, d = x.shape
    tm = _tile(m, _ROW_TILES)
    return pl.pallas_call(
        _rmsnorm_kernel,
        grid=(m // tm,),
        in_specs=[pl.BlockSpec((tm, d), lambda i: (i, 0)), pl.BlockSpec((1, d), lambda i: (0, 0))],
        out_specs=pl.BlockSpec((tm, d), lambda i: (i, 0)),
        out_shape=jax.ShapeDtypeStruct((m, d), out_dtype),
        compiler_params=_params(("arbitrary",)),
        name="rmsnorm",
    )(x, w.reshape(1, d))


def _mm_up_kernel(a_ref, wt_ref, o_ref, wb_ref, *, n_valid, last_w):
    j = pl.program_id(0)
    tn = wt_ref.shape[0]

    @pl.when(pl.program_id(1) == 0)
    def _():
        row = j * tn + _iota((tn, 1), 0)
        wt = jnp.where(row < n_valid, wt_ref[...], 0.0)
        wb_ref[...] = _b(wt.T)

    if last_w == tn:
        o_ref[...] = _dot(a_ref[...], wb_ref[...]).astype(o_ref.dtype)
    else:
        @pl.when(j < pl.num_programs(0) - 1)
        def _():
            o_ref[...] = _dot(a_ref[...], wb_ref[...]).astype(o_ref.dtype)

        @pl.when(j == pl.num_programs(0) - 1)
        def _():
            o_ref[:, 0:last_w] = _dot(a_ref[...], wb_ref[:, 0:last_w]).astype(o_ref.dtype)


def mm_up(a, wt, out_dtype, tn):
    m, k = a.shape
    n = wt.shape[0]
    tm = _tile(m, _ROW_TILES)
    nt = pl.cdiv(n, tn)
    last_w = pl.cdiv(n - (nt - 1) * tn, 128) * 128
    return pl.pallas_call(
        functools.partial(_mm_up_kernel, n_valid=n, last_w=last_w),
        grid=(nt, m // tm),
        in_specs=[pl.BlockSpec((tm, k), lambda j, i: (i, 0)), pl.BlockSpec((tn, k), lambda j, i: (j, 0))],
        out_specs=pl.BlockSpec((tm, tn), lambda j, i: (i, j)),
        out_shape=jax.ShapeDtypeStruct((m, (nt - 1) * tn + last_w), out_dtype),
        scratch_shapes=[pltpu.VMEM((k, tn), BF16)],
        compiler_params=_params(("arbitrary", "arbitrary")),
        name="mm_up",
    )(a, wt)


def _swiglu_kernel(a_ref, wg_ref, wu_ref, o_ref, wgb_ref, wub_ref):
    @pl.when(pl.program_id(1) == 0)
    def _():
        wgb_ref[...] = _b(wg_ref[...])
        wub_ref[...] = _b(wu_ref[...])

    a = a_ref[...]
    o_ref[...] = (_silu(_dot(a, wgb_ref[...])) * _dot(a, wub_ref[...])).astype(o_ref.dtype)


def swiglu_up(a, wg, wu):
    m, k = a.shape
    f = wg.shape[-1]
    tm = _tile(m, _ROW_TILES)
    tn = _tile(f, (512, 256, 128))
    w_spec = pl.BlockSpec((k, tn), lambda j, i: (0, j))
    return pl.pallas_call(
        _swiglu_kernel,
        grid=(f // tn, m // tm),
        in_specs=[pl.BlockSpec((tm, k), lambda j, i: (i, 0)), w_spec, w_spec],
        out_specs=pl.BlockSpec((tm, tn), lambda j, i: (i, j)),
        out_shape=jax.ShapeDtypeStruct((m, f), BF16),
        scratch_shapes=[pltpu.VMEM((k, tn), BF16), pltpu.VMEM((k, tn), BF16)],
        compiler_params=_params(("arbitrary", "arbitrary")),
        name="swiglu_up",
    )(a, wg, wu)


def _mm_down_kernel(*refs, tm, seq_t):
    *a_refs, w_ref, r_ref, o_ref = refs
    acc = None
    off = 0
    for a_ref in a_refs:
        k = a_ref.shape[1]
        part = _dot(a_ref[...], w_ref[off:off + k, :])
        acc = part if acc is None else acc + part
        off += k
    t0 = (pl.program_id(0) * tm) % seq_t
    valid = ((t0 + _iota((tm, 1), 0)) >= N_PAD).astype(F32)
    o_ref[...] = r_ref[...] + acc * valid


def mm_down(a_parts, w, res, seq_t):
    m = a_parts[0].shape[0]
    k, n = w.shape
    assert sum(a.shape[1] for a in a_parts) == k
    tm = _tile(seq_t, (520, 512, 256, 128, 64))
    tn = _tile(n, (1024, 512, 256, 128))
    a_specs = [pl.BlockSpec((tm, a.shape[1]), lambda i, j: (i, 0)) for a in a_parts]
    return pl.pallas_call(
        functools.partial(_mm_down_kernel, tm=tm, seq_t=seq_t),
        grid=(m // tm, n // tn),
        in_specs=a_specs + [pl.BlockSpec((k, tn), lambda i, j: (0, j)), pl.BlockSpec((tm, tn), lambda i, j: (i, j))],
        out_specs=pl.BlockSpec((tm, tn), lambda i, j: (i, j)),
        out_shape=jax.ShapeDtypeStruct((m, n), F32),
        compiler_params=_params(("arbitrary", "arbitrary")),
        name="mm_down",
    )(*a_parts, w, res)


def _log_sigmoid(x):
    return -_softplus(-x)


def _mlstm_kernel(q_ref, k_ref, v_ref, o_ref, g_ref, bias_ref, onorm_ref, out_ref, c_ref, n_ref, m_ref):
    c = pl.program_id(0)
    L = CHUNK

    @pl.when(c == 0)
    def _():
        c_ref[...] = jnp.zeros_like(c_ref)
        n_ref[...] = jnp.zeros_like(n_ref)
        m_ref[...] = jnp.zeros_like(m_ref)

    valid = (c * L + _iota((L, 1), 0)) >= N_PAD
    causal = _iota((L, L), 0) >= _iota((L, L), 1)
    tri = _tri(L)
    scale = ML_DQK ** -0.5

    def chain(b, h, bcum, ig, b_t, i_t):
        q = q_ref[b, :, h * ML_DQK:(h + 1) * ML_DQK] * scale
        k = k_ref[b, :, h * ML_DQK:(h + 1) * ML_DQK]
        v = v_ref[b, :, h * ML_DV:(h + 1) * ML_DV]
        og = o_ref[b, :, h * ML_DV:(h + 1) * ML_DV]
        bcol = bcum[:, ML_HEADS + h:ML_HEADS + h + 1]
        icol = ig[:, h:h + 1]
        brow = b_t[ML_HEADS + h:ML_HEADS + h + 1, :]
        irow = i_t[h:h + 1, :]
        g = bcol[L - 1:L, :]
        m_prev = m_ref[b, h:h + 1, 0:1]
        c_prev = c_ref[b, h]
        n_prev = n_ref[b, h:h + 1, :]

        dm = jnp.where(causal, bcol - brow + irow, NEG)
        m_inter = bcol + m_prev
        m_t = jnp.maximum(m_inter, jnp.max(dm, axis=-1, keepdims=True))
        yield
        s = _dot_nt(_b(q), _b(k)) * jnp.exp(dm - m_t)
        w_inter = jnp.exp(m_inter - m_t)
        yield
        num = _dot(_b(s), _b(v)) + w_inter * _dot(_b(q), _b(c_prev))
        den = jnp.sum(s, axis=-1, keepdims=True) + w_inter * jnp.sum(q * n_prev, axis=-1, keepdims=True)
        yield
        hh = num / jnp.maximum(jnp.abs(den), jnp.exp(-m_t))
        hh = hh * lax.rsqrt(jnp.mean(hh * hh, axis=-1, keepdims=True) + EPS)
        hh = hh * onorm_ref[:, h * ML_DV:(h + 1) * ML_DV] * _sigmoid(og)
        out_ref[b, :, h * ML_DV:(h + 1) * ML_DV] = hh.astype(out_ref.dtype)
        yield

        a_col = g - bcol + icol
        m_new = jnp.maximum(g + m_prev, jnp.max(a_col, axis=0, keepdims=True))
        dec = jnp.exp(g + m_prev - m_new)
        wk = jnp.exp(a_col - m_new) * k
        yield
        c_ref[b, h] = dec * c_prev + _dot_tn(_b(wk), _b(v))
        n_ref[b, h:h + 1, :] = dec * n_prev + jnp.sum(wk, axis=0, keepdims=True)
        m_ref[b, h:h + 1, :] = jnp.broadcast_to(m_new, (1, m_ref.shape[2]))

    chains = []
    for b in range(q_ref.shape[0]):
        gates = g_ref[b] + bias_ref[...]
        sc = ML_GATE_CAP * jnp.tanh(gates / ML_GATE_CAP)
        ig = jnp.where(valid, sc, NEG)
        lf = jnp.where(valid, _log_sigmoid(sc), 0.0)
        bcum = _dot(tri, lf, HI)
        chains += [chain(b, h, bcum, ig, bcum.T, ig.T) for h in range(ML_HEADS)]
    _interleave(chains)


def mlstm_group(p, i_bias, f_bias, out_norm):
    batch, seq_t, _ = p.shape
    nc = seq_t // CHUNK
    bias = jnp.zeros((1, 128), F32).at[0, 0:ML_HEADS].set(i_bias).at[0, ML_HEADS:2 * ML_HEADS].set(f_bias)
    return pl.pallas_call(
        _mlstm_kernel,
        grid=(nc,),
        in_specs=[
            pl.BlockSpec((batch, CHUNK, ML_QK), lambda c: (0, c, 0)),
            pl.BlockSpec((batch, CHUNK, ML_QK), lambda c: (0, c, 1)),
            pl.BlockSpec((batch, CHUNK, ML_V), lambda c: (0, c, 1)),
            pl.BlockSpec((batch, CHUNK, ML_V), lambda c: (0, c, 2)),
            pl.BlockSpec((batch, CHUNK, 128), lambda c: (0, c, E_OFF_GATES // 128)),
            pl.BlockSpec((1, 128), lambda c: (0, 0)),
            pl.BlockSpec((1, ML_V), lambda c: (0, 0)),
        ],
        out_specs=pl.BlockSpec((batch, CHUNK, ML_V), lambda c: (0, c, 0)),
        out_shape=jax.ShapeDtypeStruct((batch, seq_t, ML_V), BF16),
        scratch_shapes=[pltpu.VMEM((batch, ML_HEADS, ML_DQK, ML_DV), F32), pltpu.VMEM((batch, 8, ML_DQK), F32),
                        pltpu.VMEM((batch, 8, 128), F32)],
        compiler_params=_params(("arbitrary",)),
        name="mlstm",
    )(p, p, p, p, p, bias, out_norm.reshape(1, ML_V))


def _rwkv_kernel(*refs):
    nb = 2 * RW_GPS + 1
    r_blk, k_blk, v_blk = refs[0:nb], refs[nb:2 * nb], refs[2 * nb:3 * nb]
    tail_ref, mu3_ref, mut_ref, chan_ref, wl_ref, out_ref, h_ref, cr_ref, ck_ref, cv_ref, ct_ref = refs[3 * nb:]
    c = pl.program_id(1)
    L = CHUNK
    G = RW_GROUP

    @pl.when(c == 0)
    def _():
        h_ref[...] = jnp.zeros_like(h_ref)
        cr_ref[...] = jnp.zeros_like(cr_ref)
        ck_ref[...] = jnp.zeros_like(ck_ref)
        cv_ref[...] = jnp.zeros_like(cv_ref)
        ct_ref[...] = jnp.zeros_like(ct_ref)

    row = _iota((L, 1), 0)
    valid = (c * L + row) >= N_PAD

    ri = _iota((G, G), 0)
    ci = _iota((G, G), 1)
    same = (ri >> 6) == (ci >> 6)
    tl = ri & (RW_N - 1)
    sl = ci & (RW_N - 1)
    bd_mask = same.astype(F32)
    eye = (ri == ci).astype(F32)
    lower = same & (tl >= sl)
    strict = same & (tl > sl)
    ones_bd = _b(bd_mask)
    tri = _tri(L)
    lane = _iota((L, LORA_ROWS), 1)

    def head_sum(x):
        hi = _b(x)
        return _dot(hi, ones_bd) + _dot(_b(x - hi.astype(F32)), ones_bd)

    def bd(x):
        return _b(jnp.concatenate([x] * RW_HPG, axis=0) * bd_mask)

    def shift_lerp(x, carry_ref, idx, mu):
        prev = carry_ref[idx]
        xs = jnp.where(row == 0, prev, pltpu.roll(x, 1, 0))
        carry_ref[idx] = x[L - 1:L, :]
        return x + mu * (xs - x)

    def unshift(blocks, b, j):
        x = jnp.concatenate([blk[b] for blk in blocks[2 * j:2 * j + 3]], axis=1)
        return x[:, RW_LANE_OFF:RW_LANE_OFF + G]

    def chain(b, j):
        gl = slice(j * G, (j + 1) * G)
        w0 = chan_ref[0:1, gl]
        a0 = chan_ref[1:2, gl]
        k_k = chan_ref[2:3, gl]
        k_a = chan_ref[3:4, gl]
        r_k = chan_ref[4:5, gl]
        ln_w = chan_ref[5:6, gl]
        ln_b = chan_ref[6:7, gl]
        r = shift_lerp(unshift(r_blk, b, j), cr_ref, (b, j), mu3_ref[0:1, gl])
        k = shift_lerp(unshift(k_blk, b, j), ck_ref, (b, j), mu3_ref[1:2, gl])
        v = shift_lerp(unshift(v_blk, b, j), cv_ref, (b, j), mu3_ref[2:3, gl])
        tail = shift_lerp(tail_ref[b], ct_ref, (b, j), mut_ref[...])

        lo1 = RW_LANE_OFF + RW_DECAY_LORA
        lo2 = lo1 + RW_A_LORA
        lo3 = lo2 + RW_GATE_LORA
        act = jnp.where(lane < RW_LANE_OFF, 0.0,
                        jnp.where(lane < lo1, jnp.tanh(tail),
                                  jnp.where(lane < lo2, tail, jnp.where(lane < lo3, _sigmoid(tail), 0.0))))
        lw = _dot(act[:, 0:128], wl_ref[0, 0:128, gl], HI)
        la = _bdot(_dot, act[:, 0:256], wl_ref[1, 0:256, gl])
        gate = _bdot(_dot, act[:, 128:384], wl_ref[2, 128:384, gl])
        yield

        w_log = -_softplus(-(w0 + lw)) - 0.5
        ld = jnp.where(valid, -jnp.exp(w_log), 0.0)
        a = _sigmoid(a0 + la)
        kk = k * k_k
        kk = kk * lax.rsqrt(jnp.maximum(head_sum(kk * kk), 1e-24))
        k2 = k * (1.0 + (a - 1.0) * k_a)
        k2 = jnp.where(valid, k2, 0.0)
        kk = jnp.where(valid, kk, 0.0)
        va = -kk
        vb = kk * a

        yield
        cum = _dot(tri, ld, HI)
        cum_l = cum[L - 1:L, :]
        w_t = jnp.exp(cum)
        w_inv = jnp.exp(-cum)
        w_prev = jnp.exp(cum - ld)
        w_end = jnp.exp(cum_l - cum)
        w_last = jnp.exp(cum_l)

        a_bd = bd(va * w_prev)
        r_bd = bd(r * w_t)
        b_bd = bd(vb * w_inv)
        k_bd = bd(k2 * w_inv)
        bl_bd = bd(vb * w_end)
        kl_bd = bd(k2 * w_end)
        v_bd = bd(v)
        yield

        n_ab = jnp.where(strict, _dot_nt(a_bd, b_bd), 0.0)
        a_ak = _b(jnp.where(strict, _dot_nt(a_bd, k_bd), 0.0))
        m_rb = _b(jnp.where(lower, _dot_nt(r_bd, b_bd), 0.0))
        m_rk = _b(jnp.where(lower, _dot_nt(r_bd, k_bd), 0.0))
        yield

        t_inv = eye + n_ab
        pw = n_ab
        for _ in range(5):
            pw_b = _b(pw)
            pw = _dot(pw_b, pw_b)
            yield
            t_inv = t_inv + _bdot(_dot, t_inv, pw)
            yield

        t_b = _b(t_inv)
        av = _dot(a_ak, v_bd)
        yield
        a_p = _b(_dot(t_b, a_bd))
        u0 = _b(_dot(t_b, _b(av)))
        yield
        p_mat = eye * w_last + _dot_tn(bl_bd, a_p)
        q_mat = _dot_tn(bl_bd, u0) + _dot_tn(kl_bd, v_bd)
        r_p = r_bd.astype(F32) + _dot(m_rb, a_p)
        y0 = _dot(m_rb, u0) + _dot(m_rk, v_bd)
        yield

        h0 = _b(h_ref[b, j])
        y_bd = _dot(_b(r_p), h0) + y0
        h_ref[b, j] = _dot(_b(p_mat), h0) + q_mat
        y = y_bd[0:L] + y_bd[L:2 * L] + y_bd[2 * L:3 * L] + y_bd[3 * L:4 * L]
        yield

        inv_n = 1.0 / RW_N
        mean = head_sum(y) * inv_n
        d = y - mean
        var = head_sum(d * d) * inv_n
        yn = d * lax.rsqrt(var + RW_LN_EPS) * ln_w + ln_b
        bonus = head_sum(r * k2 * r_k) * v
        out_ref[b, :, gl] = ((yn + bonus) * gate).astype(out_ref.dtype)

    _interleave(chain(b, j) for b in range(out_ref.shape[0]) for j in range(RW_GPS))


def rwkv7_group(p, mu, w0, w_up, a0, a_up, g_up, k_k, k_a, r_k, ln_w, ln_b):
    batch, seq_t, _ = p.shape
    nc = seq_t // CHUNK
    ng = RW_DIM // RW_GROUP
    n_lora = RW_DECAY_LORA + RW_A_LORA + RW_GATE_LORA
    o = RW_LANE_OFF
    mu3 = mu[:3 * RW_DIM].reshape(3, RW_DIM)
    mut = jnp.zeros((1, LORA_ROWS), F32).at[0, o:o + n_lora].set(mu[3 * RW_DIM:])
    chan = jnp.stack([w0, a0, k_k, k_a, r_k.reshape(RW_DIM), ln_w, ln_b, jnp.zeros((RW_DIM,), F32)])
    wl = jnp.zeros((3, LORA_ROWS, RW_DIM), F32)
    wl = wl.at[0, o:o + RW_DECAY_LORA].set(w_up)
    wl = wl.at[1, o + RW_DECAY_LORA:o + RW_DECAY_LORA + RW_A_LORA].set(a_up)
    wl = wl.at[2, o + RW_DECAY_LORA + RW_A_LORA:o + n_lora].set(g_up)

    gw = RW_GPS * RW_GROUP

    def col(seg, part):
        first = (E_OFF_RW + seg * RW_DIM - o) // 128
        return pl.BlockSpec((batch, CHUNK, 128), lambda g, c: (0, c, first + 2 * RW_GPS * g + part))

    nb = 2 * RW_GPS + 1
    cols = [col(seg, part) for seg in range(3) for part in range(nb)]
    carry = lambda width: pltpu.VMEM((batch, RW_GPS, 1, width), F32)
    return pl.pallas_call(
        _rwkv_kernel,
        grid=(ng // RW_GPS, nc),
        in_specs=cols + [
            pl.BlockSpec((batch, CHUNK, LORA_ROWS), lambda g, c: (0, c, (E_OFF_RW + 3 * RW_DIM - o) // LORA_ROWS)),
            pl.BlockSpec((3, gw), lambda g, c: (0, g)),
            pl.BlockSpec((1, LORA_ROWS), lambda g, c: (0, 0)),
            pl.BlockSpec((8, gw), lambda g, c: (0, g)),
            pl.BlockSpec((3, LORA_ROWS, gw), lambda g, c: (0, 0, g)),
        ],
        out_specs=pl.BlockSpec((batch, CHUNK, gw), lambda g, c: (0, c, g)),
        out_shape=jax.ShapeDtypeStruct((batch, seq_t, RW_DIM), BF16),
        scratch_shapes=[pltpu.VMEM((batch, RW_GPS, RW_GROUP, RW_GROUP), F32), carry(RW_GROUP), carry(RW_GROUP),
                        carry(RW_GROUP), carry(LORA_ROWS)],
        compiler_params=_params(("arbitrary", "arbitrary")),
        name="rwkv7",
    )(*([p] * (3 * nb + 1)), mu3, mut, chan, wl)


def _ssd_kernel(z_ref, xs_ref, bc_ref, dt_ref, cwx_ref, cwb_ref, cbx_ref, cbb_ref, hp_ref, dexp_ref, nw_ref,
                out_ref, s_ref, px_ref, pb_ref):
    c = pl.program_id(0)
    L = CHUNK

    @pl.when(c == 0)
    def _():
        s_ref[...] = jnp.zeros_like(s_ref)
        px_ref[...] = jnp.zeros_like(px_ref)
        pb_ref[...] = jnp.zeros_like(pb_ref)

    row8 = _iota((8, 1), 0)
    valid = (c * L + _iota((L, 1), 0)) >= N_PAD
    lane = _iota((L, 2 * SSD_P), 1)
    causal = _iota((L, L), 0) >= _iota((L, L), 1)
    tri = _tri(L)

    def conv_silu(x, prev_ref, b, w_ref, b_ref):
        prev = prev_ref[b]
        acc = x * w_ref[SSD_K - 1:SSD_K, :] + b_ref[...]
        for j in range(1, SSD_K):
            xr = pltpu.roll(x, j, 0)
            pr = pltpu.roll(prev, j, 0)
            top = jnp.where(row8 < j, pr, xr[0:8])
            sh = jnp.concatenate([top, xr[8:]], axis=0)
            acc = acc + sh * w_ref[SSD_K - 1 - j:SSD_K - j, :]
        prev_ref[b] = x[L - 8:L, :]
        return _silu(acc)

    def expand(x):
        parts = []
        for j in range(SSD_H // 2):
            lo = jnp.broadcast_to(x[:, 2 * j:2 * j + 1], (L, 2 * SSD_P))
            hi = jnp.broadcast_to(x[:, 2 * j + 1:2 * j + 2], (L, 2 * SSD_P))
            parts.append(jnp.where(lane < SSD_P, lo, hi))
        return jnp.concatenate(parts, axis=1)

    def chain(b):
        xs = conv_silu(xs_ref[b], px_ref, b, cwx_ref, cbx_ref)
        yield
        bc = conv_silu(bc_ref[b], pb_ref, b, cwb_ref, cbb_ref)
        yield

        dt = jnp.where(valid, _softplus(dt_ref[b] + hp_ref[0:1, :]), 0.0)
        a = dt * (-jnp.exp(hp_ref[1:2, :]))
        acum = _dot(tri, a, HI)
        a_t = acum.T
        dt_t = dt.T
        a_last = acum[L - 1:L, :]
        cdec = jnp.exp(a_last)
        yield
        xw = xs * expand(dt * jnp.exp(a_last - acum))
        yield
        e_ea = expand(jnp.exp(acum))
        yield

        ys = []
        for g in range(SSD_G):
            bg = bc[:, g * SSD_N:(g + 1) * SSD_N]
            cg = bc[:, SSD_G * SSD_N + g * SSD_N:SSD_G * SSD_N + (g + 1) * SSD_N]
            cb = _dot_nt(_b(cg), _b(bg))
            s_prev = s_ref[b, g]
            y_inter = _dot_nt(_b(cg), _b(s_prev)) * e_ea[:, g * SSD_GW:(g + 1) * SSD_GW]
            yield
            parts = []
            cd_rows = []
            for r in range(SSD_R):
                h = g * SSD_R + r
                seg = acum[:, h:h + 1] - a_t[h:h + 1, :]
                ldec = jnp.exp(jnp.where(causal, seg, NEG))
                parts.append(_dot(_b(cb * ldec * dt_t[h:h + 1, :]), _b(xs[:, h * SSD_P:(h + 1) * SSD_P])))
                cd_rows.append(jnp.broadcast_to(cdec[:, h:h + 1], (SSD_P, SSD_N)))
                if r % 2:
                    yield
            ys.append(jnp.concatenate(parts, axis=1) + y_inter)
            st = _dot_tn(_b(xw[:, g * SSD_GW:(g + 1) * SSD_GW]), _b(bg))
            s_ref[b, g] = jnp.concatenate(cd_rows, axis=0) * s_prev + st
            yield

        y = jnp.concatenate(ys, axis=1) + dexp_ref[...] * xs
        y = y * _silu(z_ref[b])
        yield
        outs = []
        for g in range(SSD_G):
            yg = y[:, g * SSD_GW:(g + 1) * SSD_GW]
            outs.append(yg * lax.rsqrt(jnp.mean(yg * yg, axis=-1, keepdims=True) + EPS))
        out_ref[b] = (jnp.concatenate(outs, axis=1) * nw_ref[...]).astype(out_ref.dtype)

    _interleave(chain(b) for b in range(out_ref.shape[0]))


def ssd_mixer(p, conv_w, conv_b, dt_bias, a_log, d_skip, norm_w):
    batch, seq_t, _ = p.shape
    nc = seq_t // CHUNK
    hp = jnp.zeros((8, SSD_DT_PAD), F32).at[0, :SSD_H].set(dt_bias).at[1, :SSD_H].set(a_log)
    dexp = jnp.repeat(d_skip, SSD_P).reshape(1, SSD_DI)
    conv_b = conv_b.reshape(1, -1)
    return pl.pallas_call(
        _ssd_kernel,
        grid=(nc,),
        in_specs=[
            pl.BlockSpec((batch, CHUNK, SSD_DI), lambda c: (0, c, 0)),
            pl.BlockSpec((batch, CHUNK, SSD_DI), lambda c: (0, c, 1)),
            pl.BlockSpec((batch, CHUNK, SSD_BC), lambda c: (0, c, 2 * SSD_DI // SSD_BC)),
            pl.BlockSpec((batch, CHUNK, SSD_DT_PAD), lambda c: (0, c, (2 * SSD_DI + SSD_BC) // SSD_DT_PAD)),
            pl.BlockSpec((SSD_K, SSD_DI), lambda c: (0, 0)),
            pl.BlockSpec((SSD_K, SSD_BC), lambda c: (0, SSD_DI // SSD_BC)),
            pl.BlockSpec((1, SSD_DI), lambda c: (0, 0)),
            pl.BlockSpec((1, SSD_BC), lambda c: (0, SSD_DI // SSD_BC)),
            pl.BlockSpec((8, SSD_DT_PAD), lambda c: (0, 0)),
            pl.BlockSpec((1, SSD_DI), lambda c: (0, 0)),
            pl.BlockSpec((1, SSD_DI), lambda c: (0, 0)),
        ],
        out_specs=pl.BlockSpec((batch, CHUNK, SSD_DI), lambda c: (0, c, 0)),
        out_shape=jax.ShapeDtypeStruct((batch, seq_t, SSD_DI), BF16),
        scratch_shapes=[pltpu.VMEM((batch, SSD_G, SSD_GW, SSD_N), F32), pltpu.VMEM((batch, 8, SSD_DI), F32),
                        pltpu.VMEM((batch, 8, SSD_BC), F32)],
        compiler_params=_params(("arbitrary",)),
        name="ssd",
    )(p, p, p, p, conv_w, conv_w, conv_b, conv_b, hp, dexp, norm_w.reshape(1, SSD_DI))


MOE_TILE = 512
ROUTE_W1, ROUTE_W2, ROUTE_I1, ROUTE_I2 = 0, 1, 2, 3


def _router_kernel(x_ref, w_ref, r_ref, o_ref):
    x = x_ref[...]
    hn = x * lax.rsqrt(jnp.mean(x * x, axis=-1, keepdims=True) + EPS) * w_ref[...]
    logits = _dot(hn, r_ref[...], HI)
    lane = _iota(logits.shape, 1)
    lane_f = lane.astype(F32)
    logits = jnp.where(lane < N_EXPERTS, logits, -jnp.inf)
    m1 = jnp.max(logits, axis=-1, keepdims=True)
    i1 = jnp.min(jnp.where(logits == m1, lane_f, float(ROUTER_PAD)), axis=-1, keepdims=True)
    rest = jnp.where(lane_f == i1, -jnp.inf, logits)
    m2 = jnp.max(rest, axis=-1, keepdims=True)
    i2 = jnp.min(jnp.where(rest == m2, lane_f, float(ROUTER_PAD)), axis=-1, keepdims=True)
    e2 = jnp.exp(m2 - m1)
    w1 = 1.0 / (1.0 + e2)
    w2 = e2 / (1.0 + e2)
    o_ref[...] = jnp.where(lane == ROUTE_W1, w1, jnp.where(lane == ROUTE_W2, w2,
                           jnp.where(lane == ROUTE_I1, i1, jnp.where(lane == ROUTE_I2, i2, 0.0))))


def router_top2(x, w, router):
    m, d = x.shape
    tm = _tile(m, (520, 512, 256, 128, 64))
    rpad = jnp.zeros((d, ROUTER_PAD), F32).at[:, :N_EXPERTS].set(router)
    return pl.pallas_call(
        _router_kernel,
        grid=(m // tm,),
        in_specs=[pl.BlockSpec((tm, d), lambda i: (i, 0)), pl.BlockSpec((1, d), lambda i: (0, 0)),
                  pl.BlockSpec((d, ROUTER_PAD), lambda i: (0, 0))],
        out_specs=pl.BlockSpec((tm, ROUTER_PAD), lambda i: (i, 0)),
        out_shape=jax.ShapeDtypeStruct((m, ROUTER_PAD), F32),
        compiler_params=_params(("arbitrary",)),
        name="router_top2",
    )(x, w.reshape(1, d), rpad)


def _row_copy(src_hbm, row, dst, r, sem):
    return pltpu.make_async_copy(src_hbm.at[pl.ds(row, 1)], dst.at[pl.ds(r, 1)], sem)


def _gather_norm_kernel(nused_ref, idx_ref, h_hbm, w_ref, o_ref, buf, sem):
    t = pl.program_id(0)
    tg = buf.shape[1]
    n_used = nused_ref[0]

    def issue(tile, slot):
        def body(r2, carry):
            for u in range(2):
                r = 2 * r2 + u
                _row_copy(h_hbm, idx_ref[tile * tg + r], buf.at[slot], r, sem.at[slot]).start(priority=u)
            return carry

        lax.fori_loop(0, tg // 2, body, 0, unroll=4)

    @pl.when(jnp.logical_and(t == 0, n_used > 0))
    def _():
        issue(0, 0)

    @pl.when(t + 1 < n_used)
    def _():
        issue(t + 1, (t + 1) % 2)

    @pl.when(t < n_used)
    def _():
        slot = t % 2
        pltpu.make_async_copy(h_hbm.at[pl.ds(0, tg)], buf.at[slot], sem.at[slot]).wait()
        x = buf[slot]
        hn = x * lax.rsqrt(jnp.mean(x * x, axis=-1, keepdims=True) + EPS) * w_ref[...]
        o_ref[...] = hn.astype(o_ref.dtype)

    @pl.when(t >= n_used)
    def _():
        o_ref[...] = jnp.zeros_like(o_ref)


def gather_norm(h, w, src, n_used, n_tiles):
    m, d = h.shape
    tg = MOE_TILE
    return pl.pallas_call(
        _gather_norm_kernel,
        grid_spec=pltpu.PrefetchScalarGridSpec(
            num_scalar_prefetch=2,
            grid=(n_tiles,),
            in_specs=[pl.BlockSpec(memory_space=pl.ANY), pl.BlockSpec((1, d), lambda t, *_: (0, 0))],
            out_specs=pl.BlockSpec((tg, d), lambda t, *_: (t, 0)),
            scratch_shapes=[pltpu.VMEM((2, tg, d), F32), pltpu.SemaphoreType.DMA((2,))],
        ),
        out_shape=jax.ShapeDtypeStruct((n_tiles * tg, d), BF16),
        compiler_params=_params(("arbitrary",)),
        name="moe_gather",
    )(n_used, src, h, w.reshape(1, d))


def _stream_tiles(lo_ref, hi_ref, a_copy, o_copy, compute):
    j = pl.program_id(0)
    e = pl.program_id(1)
    n_e = pl.num_programs(1)
    lo = lo_ref[e]
    hi = hi_ref[e]

    @pl.when(jnp.logical_and(jnp.logical_and(j == 0, e == 0), lo < hi))
    def _():
        a_copy(lo, 0).start()

    def body(t, carry):
        slot = (t - lo) % 2
        a_copy(t, slot).wait()

        @pl.when(t + 1 < hi)
        def _():
            a_copy(t + 1, 1 - slot).start()

        @pl.when(t - lo >= 2)
        def _():
            o_copy(t - 2, slot).wait()

        compute(slot)
        o_copy(t, slot).start()
        return carry

    lax.fori_loop(lo, hi, body, 0)

    e_next = jnp.where(e + 1 < n_e, e + 1, 0)
    last_step = jnp.logical_and(j == pl.num_programs(0) - 1, e == n_e - 1)

    @pl.when(jnp.logical_and(jnp.logical_not(last_step), lo_ref[e_next] < hi_ref[e_next]))
    def _():
        a_copy(lo_ref[e_next], 0).start()

    @pl.when(hi - lo >= 2)
    def _():
        o_copy(hi - 2, (hi - 2 - lo) % 2).wait()

    @pl.when(hi - lo >= 1)
    def _():
        o_copy(hi - 1, (hi - 1 - lo) % 2).wait()


def _zero_idle_tiles(first, last, zbuf, o_copy_from, sem):
    zbuf[...] = jnp.zeros_like(zbuf)

    def body(t, carry):
        cp = o_copy_from(zbuf, t, sem)
        cp.start()
        cp.wait()
        return carry

    lax.fori_loop(first, last, body, 0)


def _moe_up_kernel(lo_ref, hi_ref, xs_hbm, wg_ref, wu_ref, out_hbm, wgb_ref, wub_ref, abuf, obuf, zbuf,
                   asem, osem, zsem):
    j = pl.program_id(0)
    e = pl.program_id(1)
    tg = abuf.shape[1]
    tn = obuf.shape[2]
    col = pl.multiple_of(j * tn, tn)
    wgb_ref[...] = _b(wg_ref[...])
    wub_ref[...] = _b(wu_ref[...])

    class _Load:
        def __init__(self, t, slot):
            self.cp = pltpu.make_async_copy(xs_hbm.at[pl.ds(pl.multiple_of(t * tg, tg), tg)], abuf.at[slot],
                                            asem.at[slot])

        def start(self):
            self.cp.start(priority=1)

        def wait(self):
            self.cp.wait()

    def o_copy_from(src, t, sem):
        return pltpu.make_async_copy(src, out_hbm.at[pl.ds(pl.multiple_of(t * tg, tg), tg), pl.ds(col, tn)], sem)

    def o_copy(t, slot):
        return o_copy_from(obuf.at[slot], t, osem.at[slot])

    def compute(slot):
        a = abuf[slot]
        obuf[slot] = (_silu(_dot(a, wgb_ref[...])) * _dot(a, wub_ref[...])).astype(obuf.dtype)

    _stream_tiles(lo_ref, hi_ref, _Load, o_copy, compute)

    @pl.when(e == pl.num_programs(1) - 1)
    def _():
        _zero_idle_tiles(hi_ref[e], out_hbm.shape[0] // tg, zbuf, o_copy_from, zsem)


def moe_up(xs, wg, wu, tile_lo, tile_hi):
    p, k = xs.shape
    n_e, _, f = wg.shape
    tg = MOE_TILE
    tn = _tile(f, (512, 256, 128))
    w_spec = pl.BlockSpec((None, k, tn), lambda j, e, lo, hi: (e, 0, j))
    any_spec = pl.BlockSpec(memory_space=pl.ANY)
    return pl.pallas_call(
        _moe_up_kernel,
        grid_spec=pltpu.PrefetchScalarGridSpec(
            num_scalar_prefetch=2,
            grid=(f // tn, n_e),
            in_specs=[any_spec, w_spec, w_spec],
            out_specs=any_spec,
            scratch_shapes=[pltpu.VMEM((k, tn), BF16), pltpu.VMEM((k, tn), BF16), pltpu.VMEM((2, tg, k), BF16),
                            pltpu.VMEM((2, tg, tn), BF16), pltpu.VMEM((tg, tn), BF16),
                            pltpu.SemaphoreType.DMA((2,)), pltpu.SemaphoreType.DMA((2,)),
                            pltpu.SemaphoreType.DMA(())],
        ),
        out_shape=jax.ShapeDtypeStruct((p, f), BF16),
        compiler_params=_params(("arbitrary", "arbitrary")),
        name="moe_up",
    )(tile_lo, tile_hi, xs, wg, wu)


def _moe_down_kernel(lo_ref, hi_ref, fs_hbm, w_ref, out_hbm, wb_ref, abuf, obuf, zbuf, asem, osem, zsem):
    j = pl.program_id(0)
    e = pl.program_id(1)
    tg = abuf.shape[1]
    tn = obuf.shape[2]
    col = pl.multiple_of(j * tn, tn)
    wb_ref[...] = _b(w_ref[...])

    class _Load:
        def __init__(self, t, slot):
            self.cp = pltpu.make_async_copy(fs_hbm.at[pl.ds(pl.multiple_of(t * tg, tg), tg)], abuf.at[slot],
                                            asem.at[slot])

        def start(self):
            self.cp.start(priority=1)

        def wait(self):
            self.cp.wait()

    def o_copy_from(src, t, sem):
        return pltpu.make_async_copy(src, out_hbm.at[pl.ds(pl.multiple_of(t * tg, tg), tg), pl.ds(col, tn)], sem)

    def o_copy(t, slot):
        return o_copy_from(obuf.at[slot], t, osem.at[slot])

    def compute(slot):
        obuf[slot] = _dot(abuf[slot], wb_ref[...])

    _stream_tiles(lo_ref, hi_ref, _Load, o_copy, compute)

    @pl.when(e == pl.num_programs(1) - 1)
    def _():
        _zero_idle_tiles(hi_ref[e], out_hbm.shape[0] // tg, zbuf, o_copy_from, zsem)


def moe_down(fs, wd, tile_lo, tile_hi):
    p, f = fs.shape
    n_e, _, d = wd.shape
    tg = MOE_TILE
    tn = _tile(d, (512, 256, 128))
    any_spec = pl.BlockSpec(memory_space=pl.ANY)
    return pl.pallas_call(
        _moe_down_kernel,
        grid_spec=pltpu.PrefetchScalarGridSpec(
            num_scalar_prefetch=2,
            grid=(d // tn, n_e),
            in_specs=[any_spec, pl.BlockSpec((None, f, tn), lambda j, e, lo, hi: (e, 0, j))],
            out_specs=any_spec,
            scratch_shapes=[pltpu.VMEM((f, tn), BF16), pltpu.VMEM((2, tg, f), BF16), pltpu.VMEM((2, tg, tn), F32),
                            pltpu.VMEM((tg, tn), F32), pltpu.SemaphoreType.DMA((2,)),
                            pltpu.SemaphoreType.DMA((2,)), pltpu.SemaphoreType.DMA(())],
        ),
        out_shape=jax.ShapeDtypeStruct((p, d), F32),
        compiler_params=_params(("arbitrary", "arbitrary")),
        name="moe_down",
    )(tile_lo, tile_hi, fs, wd)


def _combine_norm_kernel(pos_ref, h_hbm, route_hbm, y_hbm, w_ref, o_ref, hbuf, rbuf, buf0, buf1, sem,
                         *, seq_t, m, tiles_per_seq):
    s = pl.program_id(0)
    tm = hbuf.shape[1]

    def first_row(step):
        return (step // tiles_per_seq) * seq_t + (N_PAD + N_META) + (step % tiles_per_seq) * tm

    def copies(base, slot):
        return (pltpu.make_async_copy(h_hbm.at[pl.ds(base, tm)], hbuf.at[slot], sem.at[slot, 0]),
                pltpu.make_async_copy(route_hbm.at[pl.ds(base, tm)], rbuf.at[slot], sem.at[slot, 1]))

    def issue(step, slot):
        base = first_row(step)
        for cp in copies(base, slot):
            cp.start()

        def body(r, carry):
            _row_copy(y_hbm, pos_ref[base + r], buf0.at[slot], r, sem.at[slot, 2]).start(priority=0)
            _row_copy(y_hbm, pos_ref[m + base + r], buf1.at[slot], r, sem.at[slot, 3]).start(priority=1)
            return carry

        lax.fori_loop(0, tm, body, 0, unroll=4)

    @pl.when(s == 0)
    def _():
        issue(0, 0)

    @pl.when(s + 1 < pl.num_programs(0))
    def _():
        issue(s + 1, (s + 1) % 2)

    slot = s % 2
    for cp in copies(0, slot):
        cp.wait()
    pltpu.make_async_copy(y_hbm.at[pl.ds(0, tm)], buf0.at[slot], sem.at[slot, 2]).wait()
    pltpu.make_async_copy(y_hbm.at[pl.ds(0, tm)], buf1.at[slot], sem.at[slot, 3]).wait()
    w1 = rbuf[slot, :, ROUTE_W1:ROUTE_W1 + 1]
    w2 = rbuf[slot, :, ROUTE_W2:ROUTE_W2 + 1]
    x = hbuf[slot] + (w1 * buf0[slot] + w2 * buf1[slot])
    o_ref[...] = (x * lax.rsqrt(jnp.mean(x * x, axis=-1, keepdims=True) + EPS) * w_ref[...]).astype(o_ref.dtype)


def combine_norm(h, route, ys, pos, w, batch, seq_t):
    m, d = h.shape
    seq = seq_t - N_PAD - N_META
    tm = _tile(seq, (256, 128, 64))
    any_spec = pl.BlockSpec(memory_space=pl.ANY)
    return pl.pallas_call(
        functools.partial(_combine_norm_kernel, seq_t=seq_t, m=m, tiles_per_seq=seq // tm),
        grid_spec=pltpu.PrefetchScalarGridSpec(
            num_scalar_prefetch=1,
            grid=(batch * seq // tm,),
            in_specs=[any_spec, any_spec, any_spec, pl.BlockSpec((1, d), lambda s, *_: (0, 0))],
            out_specs=pl.BlockSpec((tm, d), lambda s, *_: (s, 0)),
            scratch_shapes=[pltpu.VMEM((2, tm, d), F32), pltpu.VMEM((2, tm, ROUTER_PAD), F32),
                            pltpu.VMEM((2, tm, d), F32), pltpu.VMEM((2, tm, d), F32),
                            pltpu.SemaphoreType.DMA((2, 4))],
        ),
        out_shape=jax.ShapeDtypeStruct((batch * seq, d), F32),
        compiler_params=_params(("arbitrary",)),
        name="moe_combine",
    )(pos, h, route, ys, w.reshape(1, d))


def moe_routing(route, batch, seq_t):
    m = batch * seq_t
    tg = MOE_TILE
    n_tiles = (2 * m + tg - 1) // tg + N_EXPERTS
    e_flat = jnp.concatenate([route[:, ROUTE_I1], route[:, ROUTE_I2]]).astype(jnp.int32)
    tok = jnp.concatenate([jnp.arange(m, dtype=jnp.int32)] * 2)
    real = (tok % seq_t) >= (N_PAD + N_META)
    onehot = ((e_flat[:, None] == jnp.arange(N_EXPERTS, dtype=jnp.int32)[None, :]) & real[:, None]).astype(jnp.int32)
    csum = jnp.cumsum(onehot, axis=0)
    rank = jnp.sum(csum * onehot, axis=1) - 1
    counts = csum[-1]
    padded = ((counts + tg - 1) // tg) * tg
    ends = jnp.cumsum(padded)
    starts = ends - padded
    pos = jnp.sum(starts[None, :] * onehot, axis=1) + rank
    pos = jnp.where(real, pos, n_tiles * tg)
    src = jnp.zeros((n_tiles * tg,), jnp.int32).at[pos].set(tok, mode="drop")
    n_used = (ends[-1] // tg).astype(jnp.int32)
    return src, pos, starts // tg, ends // tg, n_used.reshape(1), n_tiles


def kernel(x, meta_tokens, e_norm_mix, e_w_in, ml_i_bias, ml_f_bias, ml_out_norm, rw_mu, rw_w0, rw_w_up, rw_a0, rw_a_up, rw_g_up, rw_k_k, rw_k_a, rw_r_k, rw_ln_w, rw_ln_b, e_w_out, e_norm_ffn, ffn_w_gate, ffn_w_up, ffn_w_down, o_norm_mix, ssd_w_in, ssd_conv_w, ssd_conv_b, ssd_dt_bias, ssd_A_log, ssd_D, ssd_norm, ssd_w_out, o_norm_ffn, moe_router, moe_w_gate, moe_w_up, moe_w_down, final_norm):
    batch, seq, d = x.shape
    seq_t = N_PAD + N_META + seq
    m = batch * seq_t
    h = jnp.concatenate([jnp.zeros((batch, N_PAD, d), x.dtype),
                         jnp.broadcast_to(meta_tokens.astype(x.dtype), (batch, N_META, d)), x], axis=1)
    h = h.reshape(m, d)

    hn = rmsnorm(h, e_norm_mix[0], BF16)
    p = mm_up(hn, e_w_in[0].T, F32, 512).reshape(batch, seq_t, -1)
    y_ml = mlstm_group(p, ml_i_bias[0], ml_f_bias[0], ml_out_norm[0]).reshape(m, ML_V)
    y_rw = rwkv7_group(p, rw_mu[0], rw_w0[0], rw_w_up[0], rw_a0[0], rw_a_up[0], rw_g_up[0], rw_k_k[0],
                       rw_k_a[0], rw_r_k[0], rw_ln_w[0], rw_ln_b[0]).reshape(m, RW_DIM)
    h = mm_down([y_ml, y_rw], _b(e_w_out[0]), h, seq_t)
    hn = rmsnorm(h, e_norm_ffn[0], BF16)
    f = swiglu_up(hn, ffn_w_gate[0], ffn_w_up[0])
    h = mm_down([f], _b(ffn_w_down[0]), h, seq_t)

    hn = rmsnorm(h, o_norm_mix[0], BF16)
    p = mm_up(hn, ssd_w_in[0].T, F32, 1024).reshape(batch, seq_t, -1)
    y = ssd_mixer(p, ssd_conv_w[0], ssd_conv_b[0], ssd_dt_bias[0], ssd_A_log[0], ssd_D[0], ssd_norm[0])
    h = mm_down([y.reshape(m, SSD_DI)], _b(ssd_w_out[0]), h, seq_t)
    route = router_top2(h, o_norm_ffn[0], moe_router[0])
    src, pos, tile_lo, tile_hi, n_used, n_tiles = moe_routing(route, batch, seq_t)
    xs = gather_norm(h, o_norm_ffn[0], src, n_used, n_tiles)
    fs = moe_up(xs, moe_w_gate[0], moe_w_up[0], tile_lo, tile_hi)
    ys = moe_down(fs, moe_w_down[0], tile_lo, tile_hi)
    out = combine_norm(h, route, ys, pos, final_norm, batch, seq_t)
    return out.reshape(batch, seq, d).astype(x.dtype)
```

```python
import functools

import jax
import jax.numpy as jnp
from jax import lax
from jax.experimental import pallas as pl
from jax.experimental.pallas import tpu as pltpu

F32 = jnp.float32
BF16 = jnp.bfloat16
HI = lax.Precision.HIGHEST

N_META = 16
CHUNK = 64
N_PAD = CHUNK - N_META
NEG = -1e30
EPS = 1e-6

ML_HEADS = 4
ML_DQK = 128
ML_DV = 256
ML_GATE_CAP = 15.0
ML_QK = ML_HEADS * ML_DQK
ML_V = ML_HEADS * ML_DV

RW_HEADS = 16
RW_N = 64
RW_DIM = RW_HEADS * RW_N
RW_DECAY_LORA = 64
RW_A_LORA = 64
RW_GATE_LORA = 160
RW_LN_EPS = 64e-5
RW_GROUP = 256
RW_HPG = RW_GROUP // RW_N
RW_GPS = 4
LORA_ROWS = 384

E_OFF_GATES = 2 * ML_QK + 2 * ML_V
E_OFF_RW = E_OFF_GATES + 2 * ML_HEADS
RW_LANE_OFF = E_OFF_RW % 128

SSD_DI = 4096
SSD_P = 64
SSD_H = SSD_DI // SSD_P
SSD_N = 128
SSD_G = 8
SSD_R = SSD_H // SSD_G
SSD_K = 4
SSD_GW = SSD_DI // SSD_G
SSD_BC = 2 * SSD_G * SSD_N
SSD_DT_PAD = 128

N_EXPERTS = 8
ROUTER_PAD = 128

VMEM_LIMIT = 56 * 1024 * 1024


def _tile(n, prefs):
    for p in prefs:
        if n % p == 0:
            return p
    return n


_ROW_TILES = (1040, 832, 640, 520, 512, 256, 128, 64)


def _params(sem):
    return pltpu.CompilerParams(dimension_semantics=sem, vmem_limit_bytes=VMEM_LIMIT)


def _dot(a, b, prec=None):
    return lax.dot_general(a, b, (((1,), (0,)), ((), ())), precision=prec, preferred_element_type=F32)


def _dot_nt(a, b, prec=None):
    return lax.dot_general(a, b, (((1,), (1,)), ((), ())), precision=prec, preferred_element_type=F32)


def _dot_tn(a, b, prec=None):
    return lax.dot_general(a, b, (((0,), (0,)), ((), ())), precision=prec, preferred_element_type=F32)


def _b(x):
    return x.astype(BF16)


def _bdot(dot, a, b):
    return dot(_b(a), _b(b))


def _dot_split(a, b):
    hi = _b(a)
    lo = _b(a - hi.astype(F32))
    bb = _b(b)
    return _dot(hi, bb) + _dot(lo, bb)


def _sigmoid(x):
    return 1.0 / (1.0 + jnp.exp(-x))


def _softplus(x):
    return jnp.maximum(x, 0.0) + jnp.log1p(jnp.exp(-jnp.abs(x)))


def _silu(x):
    return x * _sigmoid(x)


def _iota(shape, dim):
    return lax.broadcasted_iota(jnp.int32, shape, dim)


def _tri(n):
    return (_iota((n, n), 0) >= _iota((n, n), 1)).astype(F32)


def _interleave(chains):
    chains = list(chains)
    while chains:
        for ch in list(chains):
            try:
                next(ch)
            except StopIteration:
                chains.remove(ch)


def _rmsnorm_kernel(x_ref, w_ref, o_ref):
    x = x_ref[...]
    ms = jnp.mean(x * x, axis=-1, keepdims=True)
    o_ref[...] = (x * lax.rsqrt(ms + EPS) * w_ref[...]).astype(o_ref.dtype)


def rmsnorm(x, w, out_dtype):
    m, d = x.shape
    tm = _tile(m, _ROW_TILES)
    return pl.pallas_call(
        _rmsnorm_kernel,
        grid=(m // tm,),
        in_specs=[pl.BlockSpec((tm, d), lambda i: (i, 0)), pl.BlockSpec((1, d), lambda i: (0, 0))],
        out_specs=pl.BlockSpec((tm, d), lambda i: (i, 0)),
        out_shape=jax.ShapeDtypeStruct((m, d), out_dtype),
        compiler_params=_params(("arbitrary",)),
        name="rmsnorm",
    )(x, w.reshape(1, d))


def _mm_up_kernel(a_ref, wt_ref, o_ref, wb_ref, *, n_valid, last_w):
    j = pl.program_id(0)
    tn = wt_ref.shape[0]

    @pl.when(pl.program_id(1) == 0)
    def _():
        row = j * tn + _iota((tn, 1), 0)
        wt = jnp.where(row < n_valid, wt_ref[...], 0.0)
        wb_ref[...] = _b(wt.T)

    if last_w == tn:
        o_ref[...] = _dot(a_ref[...], wb_ref[...]).astype(o_ref.dtype)
    else:
        @pl.when(j < pl.num_programs(0) - 1)
        def _():
            o_ref[...] = _dot(a_ref[...], wb_ref[...]).astype(o_ref.dtype)

        @pl.when(j == pl.num_programs(0) - 1)
        def _():
            o_ref[:, 0:last_w] = _dot(a_ref[...], wb_ref[:, 0:last_w]).astype(o_ref.dtype)


def mm_up(a, wt, out_dtype, tn):
    m, k = a.shape
    n = wt.shape[0]
    tm = _tile(m, _ROW_TILES)
    nt = pl.cdiv(n, tn)
    last_w = pl.cdiv(n - (nt - 1) * tn, 128) * 128
    return pl.pallas_call(
        functools.partial(_mm_up_kernel, n_valid=n, last_w=last_w),
        grid=(nt, m // tm),
        in_specs=[pl.BlockSpec((tm, k), lambda j, i: (i, 0)), pl.BlockSpec((tn, k), lambda j, i: (j, 0))],
        out_specs=pl.BlockSpec((tm, tn), lambda j, i: (i, j)),
        out_shape=jax.ShapeDtypeStruct((m, (nt - 1) * tn + last_w), out_dtype),
        scratch_shapes=[pltpu.VMEM((k, tn), BF16)],
        compiler_params=_params(("arbitrary", "arbitrary")),
        name="mm_up",
    )(a, wt)


def _swiglu_kernel(a_ref, wg_ref, wu_ref, o_ref, wgb_ref, wub_ref):
    @pl.when(pl.program_id(1) == 0)
    def _():
        wgb_ref[...] = _b(wg_ref[...])
        wub_ref[...] = _b(wu_ref[...])

    a = a_ref[...]
    o_ref[...] = (_silu(_dot(a, wgb_ref[...])) * _dot(a, wub_ref[...])).astype(o_ref.dtype)


def swiglu_up(a, wg, wu):
    m, k = a.shape
    f = wg.shape[-1]
    tm = _tile(m, _ROW_TILES)
    tn = _tile(f, (512, 256, 128))
    w_spec = pl.BlockSpec((k, tn), lambda j, i: (0, j))
    return pl.pallas_call(
        _swiglu_kernel,
        grid=(f // tn, m // tm),
        in_specs=[pl.BlockSpec((tm, k), lambda j, i: (i, 0)), w_spec, w_spec],
        out_specs=pl.BlockSpec((tm, tn), lambda j, i: (i, j)),
        out_shape=jax.ShapeDtypeStruct((m, f), BF16),
        scratch_shapes=[pltpu.VMEM((k, tn), BF16), pltpu.VMEM((k, tn), BF16)],
        compiler_params=_params(("arbitrary", "arbitrary")),
        name="swiglu_up",
    )(a, wg, wu)


def _mm_down_kernel(*refs, tm, seq_t):
    *a_refs, w_ref, r_ref, o_ref = refs
    acc = None
    off = 0
    for a_ref in a_refs:
        k = a_ref.shape[1]
        part = _dot(a_ref[...], w_ref[off:off + k, :])
        acc = part if acc is None else acc + part
        off += k
    t0 = (pl.program_id(0) * tm) % seq_t
    valid = ((t0 + _iota((tm, 1), 0)) >= N_PAD).astype(F32)
    o_ref[...] = r_ref[...] + acc * valid


def mm_down(a_parts, w, res, seq_t):
    m = a_parts[0].shape[0]
    k, n = w.shape
    assert sum(a.shape[1] for a in a_parts) == k
    tm = _tile(seq_t, (520, 512, 256, 128, 64))
    tn = _tile(n, (1024, 512, 256, 128))
    a_specs = [pl.BlockSpec((tm, a.shape[1]), lambda i, j: (i, 0)) for a in a_parts]
    return pl.pallas_call(
        functools.partial(_mm_down_kernel, tm=tm, seq_t=seq_t),
        grid=(m // tm, n // tn),
        in_specs=a_specs + [pl.BlockSpec((k, tn), lambda i, j: (0, j)), pl.BlockSpec((tm, tn), lambda i, j: (i, j))],
        out_specs=pl.BlockSpec((tm, tn), lambda i, j: (i, j)),
        out_shape=jax.ShapeDtypeStruct((m, n), F32),
        compiler_params=_params(("arbitrary", "arbitrary")),
        name="mm_down",
    )(*a_parts, w, res)


def _log_sigmoid(x):
    return -_softplus(-x)


def _mlstm_kernel(q_ref, k_ref, v_ref, o_ref, g_ref, bias_ref, onorm_ref, out_ref, c_ref, n_ref, m_ref):
    c = pl.program_id(0)
    L = CHUNK

    @pl.when(c == 0)
    def _():
        c_ref[...] = jnp.zeros_like(c_ref)
        n_ref[...] = jnp.zeros_like(n_ref)
        m_ref[...] = jnp.zeros_like(m_ref)

    valid = (c * L + _iota((L, 1), 0)) >= N_PAD
    causal = _iota((L, L), 0) >= _iota((L, L), 1)
    tri = _tri(L)
    scale = ML_DQK ** -0.5

    def chain(b, h, bcum, ig, b_t, i_t):
        q = q_ref[b, :, h * ML_DQK:(h + 1) * ML_DQK] * scale
        k = k_ref[b, :, h * ML_DQK:(h + 1) * ML_DQK]
        v = v_ref[b, :, h * ML_DV:(h + 1) * ML_DV]
        og = o_ref[b, :, h * ML_DV:(h + 1) * ML_DV]
        bcol = bcum[:, ML_HEADS + h:ML_HEADS + h + 1]
        icol = ig[:, h:h + 1]
        brow = b_t[ML_HEADS + h:ML_HEADS + h + 1, :]
        irow = i_t[h:h + 1, :]
        g = bcol[L - 1:L, :]
        m_prev = m_ref[b, h:h + 1, 0:1]
        c_prev = c_ref[b, h]
        n_prev = n_ref[b, h:h + 1, :]

        dm = jnp.where(causal, bcol - brow + irow, NEG)
        m_inter = bcol + m_prev
        m_t = jnp.maximum(m_inter, jnp.max(dm, axis=-1, keepdims=True))
        yield
        s = _dot_nt(_b(q), _b(k)) * jnp.exp(dm - m_t)
        w_inter = jnp.exp(m_inter - m_t)
        yield
        num = _dot(_b(s), _b(v)) + w_inter * _dot(_b(q), _b(c_prev))
        den = jnp.sum(s, axis=-1, keepdims=True) + w_inter * jnp.sum(q * n_prev, axis=-1, keepdims=True)
        yield
        hh = num / jnp.maximum(jnp.abs(den), jnp.exp(-m_t))
        hh = hh * lax.rsqrt(jnp.mean(hh * hh, axis=-1, keepdims=True) + EPS)
        hh = hh * onorm_ref[:, h * ML_DV:(h + 1) * ML_DV] * _sigmoid(og)
        out_ref[b, :, h * ML_DV:(h + 1) * ML_DV] = hh.astype(out_ref.dtype)
        yield

        a_col = g - bcol + icol
        m_new = jnp.maximum(g + m_prev, jnp.max(a_col, axis=0, keepdims=True))
        dec = jnp.exp(g + m_prev - m_new)
        wk = jnp.exp(a_col - m_new) * k
        yield
        c_ref[b, h] = dec * c_prev + _dot_tn(_b(wk), _b(v))
        n_ref[b, h:h + 1, :] = dec * n_prev + jnp.sum(wk, axis=0, keepdims=True)
        m_ref[b, h:h + 1, :] = jnp.broadcast_to(m_new, (1, m_ref.shape[2]))

    chains = []
    for b in range(q_ref.shape[0]):
        gates = g_ref[b] + bias_ref[...]
        sc = ML_GATE_CAP * jnp.tanh(gates / ML_GATE_CAP)
        ig = jnp.where(valid, sc, NEG)
        lf = jnp.where(valid, _log_sigmoid(sc), 0.0)
        bcum = _dot(tri, lf, HI)
        chains += [chain(b, h, bcum, ig, bcum.T, ig.T) for h in range(ML_HEADS)]
    _interleave(chains)


def mlstm_group(p, i_bias, f_bias, out_norm):
    batch, seq_t, _ = p.shape
    nc = seq_t // CHUNK
    bias = jnp.zeros((1, 128), F32).at[0, 0:ML_HEADS].set(i_bias).at[0, ML_HEADS:2 * ML_HEADS].set(f_bias)
    return pl.pallas_call(
        _mlstm_kernel,
        grid=(nc,),
        in_specs=[
            pl.BlockSpec((batch, CHUNK, ML_QK), lambda c: (0, c, 0)),
            pl.BlockSpec((batch, CHUNK, ML_QK), lambda c: (0, c, 1)),
            pl.BlockSpec((batch, CHUNK, ML_V), lambda c: (0, c, 1)),
            pl.BlockSpec((batch, CHUNK, ML_V), lambda c: (0, c, 2)),
            pl.BlockSpec((batch, CHUNK, 128), lambda c: (0, c, E_OFF_GATES // 128)),
            pl.BlockSpec((1, 128), lambda c: (0, 0)),
            pl.BlockSpec((1, ML_V), lambda c: (0, 0)),
        ],
        out_specs=pl.BlockSpec((batch, CHUNK, ML_V), lambda c: (0, c, 0)),
        out_shape=jax.ShapeDtypeStruct((batch, seq_t, ML_V), BF16),
        scratch_shapes=[pltpu.VMEM((batch, ML_HEADS, ML_DQK, ML_DV), F32), pltpu.VMEM((batch, 8, ML_DQK), F32),
                        pltpu.VMEM((batch, 8, 128), F32)],
        compiler_params=_params(("arbitrary",)),
        name="mlstm",
    )(p, p, p, p, p, bias, out_norm.reshape(1, ML_V))


def _rwkv_kernel(*refs):
    nb = 2 * RW_GPS + 1
    r_blk, k_blk, v_blk = refs[0:nb], refs[nb:2 * nb], refs[2 * nb:3 * nb]
    tail_ref, mu3_ref, mut_ref, chan_ref, wl_ref, out_ref, h_ref, cr_ref, ck_ref, cv_ref, ct_ref = refs[3 * nb:]
    c = pl.program_id(1)
    L = CHUNK
    G = RW_GROUP

    @pl.when(c == 0)
    def _():
        h_ref[...] = jnp.zeros_like(h_ref)
        cr_ref[...] = jnp.zeros_like(cr_ref)
        ck_ref[...] = jnp.zeros_like(ck_ref)
        cv_ref[...] = jnp.zeros_like(cv_ref)
        ct_ref[...] = jnp.zeros_like(ct_ref)

    row = _iota((L, 1), 0)
    valid = (c * L + row) >= N_PAD

    ri = _iota((G, G), 0)
    ci = _iota((G, G), 1)
    same = (ri >> 6) == (ci >> 6)
    tl = ri & (RW_N - 1)
    sl = ci & (RW_N - 1)
    bd_mask = same.astype(F32)
    eye = (ri == ci).astype(F32)
    lower = same & (tl >= sl)
    strict = same & (tl > sl)
    ones_bd = _b(bd_mask)
    tri = _tri(L)
    lane = _iota((L, LORA_ROWS), 1)

    def head_sum(x):
        hi = _b(x)
        return _dot(hi, ones_bd) + _dot(_b(x - hi.astype(F32)), ones_bd)

    def bd(x):
        return _b(jnp.concatenate([x] * RW_HPG, axis=0) * bd_mask)

    def shift_lerp(x, carry_ref, idx, mu):
        prev = carry_ref[idx]
        xs = jnp.where(row == 0, prev, pltpu.roll(x, 1, 0))
        carry_ref[idx] = x[L - 1:L, :]
        return x + mu * (xs - x)

    def unshift(blocks, b, j):
        x = jnp.concatenate([blk[b] for blk in blocks[2 * j:2 * j + 3]], axis=1)
        return x[:, RW_LANE_OFF:RW_LANE_OFF + G]

    def chain(b, j):
        gl = slice(j * G, (j + 1) * G)
        w0 = chan_ref[0:1, gl]
        a0 = chan_ref[1:2, gl]
        k_k = chan_ref[2:3, gl]
        k_a = chan_ref[3:4, gl]
        r_k = chan_ref[4:5, gl]
        ln_w = chan_ref[5:6, gl]
        ln_b = chan_ref[6:7, gl]
        r = shift_lerp(unshift(r_blk, b, j), cr_ref, (b, j), mu3_ref[0:1, gl])
        k = shift_lerp(unshift(k_blk, b, j), ck_ref, (b, j), mu3_ref[1:2, gl])
        v = shift_lerp(unshift(v_blk, b, j), cv_ref, (b, j), mu3_ref[2:3, gl])
        tail = shift_lerp(tail_ref[b], ct_ref, (b, j), mut_ref[...])

        lo1 = RW_LANE_OFF + RW_DECAY_LORA
        lo2 = lo1 + RW_A_LORA
        lo3 = lo2 + RW_GATE_LORA
        act = jnp.where(lane < RW_LANE_OFF, 0.0,
                        jnp.where(lane < lo1, jnp.tanh(tail),
                                  jnp.where(lane < lo2, tail, jnp.where(lane < lo3, _sigmoid(tail), 0.0))))
        lw = _dot(act[:, 0:128], wl_ref[0, 0:128, gl], HI)
        la = _bdot(_dot, act[:, 0:256], wl_ref[1, 0:256, gl])
        gate = _bdot(_dot, act[:, 128:384], wl_ref[2, 128:384, gl])
        yield

        w_log = -_softplus(-(w0 + lw)) - 0.5
        ld = jnp.where(valid, -jnp.exp(w_log), 0.0)
        a = _sigmoid(a0 + la)
        kk = k * k_k
        kk = kk * lax.rsqrt(jnp.maximum(head_sum(kk * kk), 1e-24))
        k2 = k * (1.0 + (a - 1.0) * k_a)
        k2 = jnp.where(valid, k2, 0.0)
        kk = jnp.where(valid, kk, 0.0)
        va = -kk
        vb = kk * a

        yield
        cum = _dot(tri, ld, HI)
        cum_l = cum[L - 1:L, :]
        w_t = jnp.exp(cum)
        w_inv = jnp.exp(-cum)
        w_prev = jnp.exp(cum - ld)
        w_end = jnp.exp(cum_l - cum)
        w_last = jnp.exp(cum_l)

        a_bd = bd(va * w_prev)
        r_bd = bd(r * w_t)
        b_bd = bd(vb * w_inv)
        k_bd = bd(k2 * w_inv)
        bl_bd = bd(vb * w_end)
        kl_bd = bd(k2 * w_end)
        v_bd = bd(v)
        yield

        n_ab = jnp.where(strict, _dot_nt(a_bd, b_bd), 0.0)
        a_ak = _b(jnp.where(strict, _dot_nt(a_bd, k_bd), 0.0))
        m_rb = _b(jnp.where(lower, _dot_nt(r_bd, b_bd), 0.0))
        m_rk = _b(jnp.where(lower, _dot_nt(r_bd, k_bd), 0.0))
        yield

        t_inv = eye + n_ab
        pw = n_ab
        for _ in range(5):
            pw_b = _b(pw)
            pw = _dot(pw_b, pw_b)
            yield
            t_inv = t_inv + _bdot(_dot, t_inv, pw)
            yield

        t_b = _b(t_inv)
        av = _dot(a_ak, v_bd)
        yield
        a_p = _b(_dot(t_b, a_bd))
        u0 = _b(_dot(t_b, _b(av)))
        yield
        p_mat = eye * w_last + _dot_tn(bl_bd, a_p)
        q_mat = _dot_tn(bl_bd, u0) + _dot_tn(kl_bd, v_bd)
        r_p = r_bd.astype(F32) + _dot(m_rb, a_p)
        y0 = _dot(m_rb, u0) + _dot(m_rk, v_bd)
        yield

        h0 = _b(h_ref[b, j])
        y_bd = _dot(_b(r_p), h0) + y0
        h_ref[b, j] = _dot(_b(p_mat), h0) + q_mat
        y = y_bd[0:L] + y_bd[L:2 * L] + y_bd[2 * L:3 * L] + y_bd[3 * L:4 * L]
        yield

        inv_n = 1.0 / RW_N
        mean = head_sum(y) * inv_n
        d = y - mean
        var = head_sum(d * d) * inv_n
        yn = d * lax.rsqrt(var + RW_LN_EPS) * ln_w + ln_b
        bonus = head_sum(r * k2 * r_k) * v
        out_ref[b, :, gl] = ((yn + bonus) * gate).astype(out_ref.dtype)

    _interleave(chain(b, j) for b in range(out_ref.shape[0]) for j in range(RW_GPS))


def rwkv7_group(p, mu, w0, w_up, a0, a_up, g_up, k_k, k_a, r_k, ln_w, ln_b):
    batch, seq_t, _ = p.shape
    nc = seq_t // CHUNK
    ng = RW_DIM // RW_GROUP
    n_lora = RW_DECAY_LORA + RW_A_LORA + RW_GATE_LORA
    o = RW_LANE_OFF
    mu3 = mu[:3 * RW_DIM].reshape(3, RW_DIM)
    mut = jnp.zeros((1, LORA_ROWS), F32).at[0, o:o + n_lora].set(mu[3 * RW_DIM:])
    chan = jnp.stack([w0, a0, k_k, k_a, r_k.reshape(RW_DIM), ln_w, ln_b, jnp.zeros((RW_DIM,), F32)])
    wl = jnp.zeros((3, LORA_ROWS, RW_DIM), F32)
    wl = wl.at[0, o:o + RW_DECAY_LORA].set(w_up)
    wl = wl.at[1, o + RW_DECAY_LORA:o + RW_DECAY_LORA + RW_A_LORA].set(a_up)
    wl = wl.at[2, o + RW_DECAY_LORA + RW_A_LORA:o + n_lora].set(g_up)

    gw = RW_GPS * RW_GROUP

    def col(seg, part):
        first = (E_OFF_RW + seg * RW_DIM - o) // 128
        return pl.BlockSpec((batch, CHUNK, 128), lambda g, c: (0, c, first + 2 * RW_GPS * g + part))

    nb = 2 * RW_GPS + 1
    cols = [col(seg, part) for seg in range(3) for part in range(nb)]
    carry = lambda width: pltpu.VMEM((batch, RW_GPS, 1, width), F32)
    return pl.pallas_call(
        _rwkv_kernel,
        grid=(ng // RW_GPS, nc),
        in_specs=cols + [
            pl.BlockSpec((batch, CHUNK, LORA_ROWS), lambda g, c: (0, c, (E_OFF_RW + 3 * RW_DIM - o) // LORA_ROWS)),
            pl.BlockSpec((3, gw), lambda g, c: (0, g)),
            pl.BlockSpec((1, LORA_ROWS), lambda g, c: (0, 0)),
            pl.BlockSpec((8, gw), lambda g, c: (0, g)),
            pl.BlockSpec((3, LORA_ROWS, gw), lambda g, c: (0, 0, g)),
        ],
        out_specs=pl.BlockSpec((batch, CHUNK, gw), lambda g, c: (0, c, g)),
        out_shape=jax.ShapeDtypeStruct((batch, seq_t, RW_DIM), BF16),
        scratch_shapes=[pltpu.VMEM((batch, RW_GPS, RW_GROUP, RW_GROUP), F32), carry(RW_GROUP), carry(RW_GROUP),
                        carry(RW_GROUP), carry(LORA_ROWS)],
        compiler_params=_params(("arbitrary", "arbitrary")),
        name="rwkv7",
    )(*([p] * (3 * nb + 1)), mu3, mut, chan, wl)


def _ssd_kernel(z_ref, xs_ref, bc_ref, dt_ref, cwx_ref, cwb_ref, cbx_ref, cbb_ref, hp_ref, dexp_ref, nw_ref,
                out_ref, s_ref, px_ref, pb_ref):
    c = pl.program_id(0)
    L = CHUNK

    @pl.when(c == 0)
    def _():
        s_ref[...] = jnp.zeros_like(s_ref)
        px_ref[...] = jnp.zeros_like(px_ref)
        pb_ref[...] = jnp.zeros_like(pb_ref)

    row8 = _iota((8, 1), 0)
    valid = (c * L + _iota((L, 1), 0)) >= N_PAD
    lane = _iota((L, 2 * SSD_P), 1)
    causal = _iota((L, L), 0) >= _iota((L, L), 1)
    tri = _tri(L)

    def conv_silu(x, prev_ref, b, w_ref, b_ref):
        prev = prev_ref[b]
        acc = x * w_ref[SSD_K - 1:SSD_K, :] + b_ref[...]
        for j in range(1, SSD_K):
            xr = pltpu.roll(x, j, 0)
            pr = pltpu.roll(prev, j, 0)
            top = jnp.where(row8 < j, pr, xr[0:8])
            sh = jnp.concatenate([top, xr[8:]], axis=0)
            acc = acc + sh * w_ref[SSD_K - 1 - j:SSD_K - j, :]
        prev_ref[b] = x[L - 8:L, :]
        return _silu(acc)

    def expand(x):
        parts = []
        for j in range(SSD_H // 2):
            lo = jnp.broadcast_to(x[:, 2 * j:2 * j + 1], (L, 2 * SSD_P))
            hi = jnp.broadcast_to(x[:, 2 * j + 1:2 * j + 2], (L, 2 * SSD_P))
            parts.append(jnp.where(lane < SSD_P, lo, hi))
        return jnp.concatenate(parts, axis=1)

    def chain(b):
        xs = conv_silu(xs_ref[b], px_ref, b, cwx_ref, cbx_ref)
        yield
        bc = conv_silu(bc_ref[b], pb_ref, b, cwb_ref, cbb_ref)
        yield

        dt = jnp.where(valid, _softplus(dt_ref[b] + hp_ref[0:1, :]), 0.0)
        a = dt * (-jnp.exp(hp_ref[1:2, :]))
        acum = _dot(tri, a, HI)
        a_t = acum.T
        dt_t = dt.T
        a_last = acum[L - 1:L, :]
        cdec = jnp.exp(a_last)
        yield
        xw = xs * expand(dt * jnp.exp(a_last - acum))
        yield
        e_ea = expand(jnp.exp(acum))
        yield

        ys = []
        for g in range(SSD_G):
            bg = bc[:, g * SSD_N:(g + 1) * SSD_N]
            cg = bc[:, SSD_G * SSD_N + g * SSD_N:SSD_G * SSD_N + (g + 1) * SSD_N]
            cb = _dot_nt(_b(cg), _b(bg))
            s_prev = s_ref[b, g]
            y_inter = _dot_nt(_b(cg), _b(s_prev)) * e_ea[:, g * SSD_GW:(g + 1) * SSD_GW]
            yield
            parts = []
            cd_rows = []
            for r in range(SSD_R):
                h = g * SSD_R + r
                seg = acum[:, h:h + 1] - a_t[h:h + 1, :]
                ldec = jnp.exp(jnp.where(causal, seg, NEG))
                parts.append(_dot(_b(cb * ldec * dt_t[h:h + 1, :]), _b(xs[:, h * SSD_P:(h + 1) * SSD_P])))
                cd_rows.append(jnp.broadcast_to(cdec[:, h:h + 1], (SSD_P, SSD_N)))
                if r % 2:
                    yield
            ys.append(jnp.concatenate(parts, axis=1) + y_inter)
            st = _dot_tn(_b(xw[:, g * SSD_GW:(g + 1) * SSD_GW]), _b(bg))
            s_ref[b, g] = jnp.concatenate(cd_rows, axis=0) * s_prev + st
            yield

        y = jnp.concatenate(ys, axis=1) + dexp_ref[...] * xs
        y = y * _silu(z_ref[b])
        yield
        outs = []
        for g in range(SSD_G):
            yg = y[:, g * SSD_GW:(g + 1) * SSD_GW]
            outs.append(yg * lax.rsqrt(jnp.mean(yg * yg, axis=-1, keepdims=True) + EPS))
        out_ref[b] = (jnp.concatenate(outs, axis=1) * nw_ref[...]).astype(out_ref.dtype)

    _interleave(chain(b) for b in range(out_ref.shape[0]))


def ssd_mixer(p, conv_w, conv_b, dt_bias, a_log, d_skip, norm_w):
    batch, seq_t, _ = p.shape
    nc = seq_t // CHUNK
    hp = jnp.zeros((8, SSD_DT_PAD), F32).at[0, :SSD_H].set(dt_bias).at[1, :SSD_H].set(a_log)
    dexp = jnp.repeat(d_skip, SSD_P).reshape(1, SSD_DI)
    conv_b = conv_b.reshape(1, -1)
    return pl.pallas_call(
        _ssd_kernel,
        grid=(nc,),
        in_specs=[
            pl.BlockSpec((batch, CHUNK, SSD_DI), lambda c: (0, c, 0)),
            pl.BlockSpec((batch, CHUNK, SSD_DI), lambda c: (0, c, 1)),
            pl.BlockSpec((batch, CHUNK, SSD_BC), lambda c: (0, c, 2 * SSD_DI // SSD_BC)),
            pl.BlockSpec((batch, CHUNK, SSD_DT_PAD), lambda c: (0, c, (2 * SSD_DI + SSD_BC) // SSD_DT_PAD)),
            pl.BlockSpec((SSD_K, SSD_DI), lambda c: (0, 0)),
            pl.BlockSpec((SSD_K, SSD_BC), lambda c: (0, SSD_DI // SSD_BC)),
            pl.BlockSpec((1, SSD_DI), lambda c: (0, 0)),
            pl.BlockSpec((1, SSD_BC), lambda c: (0, SSD_DI // SSD_BC)),
            pl.BlockSpec((8, SSD_DT_PAD), lambda c: (0, 0)),
            pl.BlockSpec((1, SSD_DI), lambda c: (0, 0)),
            pl.BlockSpec((1, SSD_DI), lambda c: (0, 0)),
        ],
        out_specs=pl.BlockSpec((batch, CHUNK, SSD_DI), lambda c: (0, c, 0)),
        out_shape=jax.ShapeDtypeStruct((batch, seq_t, SSD_DI), BF16),
        scratch_shapes=[pltpu.VMEM((batch, SSD_G, SSD_GW, SSD_N), F32), pltpu.VMEM((batch, 8, SSD_DI), F32),
                        pltpu.VMEM((batch, 8, SSD_BC), F32)],
        compiler_params=_params(("arbitrary",)),
        name="ssd",
    )(p, p, p, p, conv_w, conv_w, conv_b, conv_b, hp, dexp, norm_w.reshape(1, SSD_DI))


MOE_TILE = 512
ROUTE_W1, ROUTE_W2, ROUTE_I1, ROUTE_I2 = 0, 1, 2, 3


def _router_kernel(x_ref, w_ref, r_ref, o_ref):
    x = x_ref[...]
    hn = x * lax.rsqrt(jnp.mean(x * x, axis=-1, keepdims=True) + EPS) * w_ref[...]
    logits = _dot(hn, r_ref[...], HI)
    lane = _iota(logits.shape, 1)
    lane_f = lane.astype(F32)
    logits = jnp.where(lane < N_EXPERTS, logits, -jnp.inf)
    m1 = jnp.max(logits, axis=-1, keepdims=True)
    i1 = jnp.min(jnp.where(logits == m1, lane_f, float(ROUTER_PAD)), axis=-1, keepdims=True)
    rest = jnp.where(lane_f == i1, -jnp.inf, logits)
    m2 = jnp.max(rest, axis=-1, keepdims=True)
    i2 = jnp.min(jnp.where(rest == m2, lane_f, float(ROUTER_PAD)), axis=-1, keepdims=True)
    e2 = jnp.exp(m2 - m1)
    w1 = 1.0 / (1.0 + e2)
    w2 = e2 / (1.0 + e2)
    o_ref[...] = jnp.where(lane == ROUTE_W1, w1, jnp.where(lane == ROUTE_W2, w2,
                           jnp.where(lane == ROUTE_I1, i1, jnp.where(lane == ROUTE_I2, i2, 0.0))))


def router_top2(x, w, router):
    m, d = x.shape
    tm = _tile(m, (520, 512, 256, 128, 64))
    rpad = jnp.zeros((d, ROUTER_PAD), F32).at[:, :N_EXPERTS].set(router)
    return pl.pallas_call(
        _router_kernel,
        grid=(m // tm,),
        in_specs=[pl.BlockSpec((tm, d), lambda i: (i, 0)), pl.BlockSpec((1, d), lambda i: (0, 0)),
                  pl.BlockSpec((d, ROUTER_PAD), lambda i: (0, 0))],
        out_specs=pl.BlockSpec((tm, ROUTER_PAD), lambda i: (i, 0)),
        out_shape=jax.ShapeDtypeStruct((m, ROUTER_PAD), F32),
        compiler_params=_params(("arbitrary",)),
        name="router_top2",
    )(x, w.reshape(1, d), rpad)


def _row_copy(src_hbm, row, dst, r, sem):
    return pltpu.make_async_copy(src_hbm.at[pl.ds(row, 1)], dst.at[pl.ds(r, 1)], sem)


def _gather_norm_kernel(nused_ref, idx_ref, h_hbm, w_ref, o_ref, buf, sem):
    t = pl.program_id(0)
    tg = buf.shape[1]
    n_used = nused_ref[0]

    def issue(tile, slot):
        def body(r2, carry):
            for u in range(2):
                r = 2 * r2 + u
                _row_copy(h_hbm, idx_ref[tile * tg + r], buf.at[slot], r, sem.at[slot]).start(priority=u)
            return carry

        lax.fori_loop(0, tg // 2, body, 0, unroll=4)

    @pl.when(jnp.logical_and(t == 0, n_used > 0))
    def _():
        issue(0, 0)

    @pl.when(t + 1 < n_used)
    def _():
        issue(t + 1, (t + 1) % 2)

    @pl.when(t < n_used)
    def _():
        slot = t % 2
        pltpu.make_async_copy(h_hbm.at[pl.ds(0, tg)], buf.at[slot], sem.at[slot]).wait()
        x = buf[slot]
        hn = x * lax.rsqrt(jnp.mean(x * x, axis=-1, keepdims=True) + EPS) * w_ref[...]
        o_ref[...] = hn.astype(o_ref.dtype)

    @pl.when(t >= n_used)
    def _():
        o_ref[...] = jnp.zeros_like(o_ref)


def gather_norm(h, w, src, n_used, n_tiles):
    m, d = h.shape
    tg = MOE_TILE
    return pl.pallas_call(
        _gather_norm_kernel,
        grid_spec=pltpu.PrefetchScalarGridSpec(
            num_scalar_prefetch=2,
            grid=(n_tiles,),
            in_specs=[pl.BlockSpec(memory_space=pl.ANY), pl.BlockSpec((1, d), lambda t, *_: (0, 0))],
            out_specs=pl.BlockSpec((tg, d), lambda t, *_: (t, 0)),
            scratch_shapes=[pltpu.VMEM((2, tg, d), F32), pltpu.SemaphoreType.DMA((2,))],
        ),
        out_shape=jax.ShapeDtypeStruct((n_tiles * tg, d), BF16),
        compiler_params=_params(("arbitrary",)),
        name="moe_gather",
    )(n_used, src, h, w.reshape(1, d))


def _new_expert(te_ref, t):
    return jnp.logical_or(t == 0, te_ref[t] != te_ref[jnp.maximum(t - 1, 0)])


def _used_tile(t, nu):
    return jnp.minimum(t, nu[0] - 1)


LOAD_BUFS = 3


def _stream_tiles(lo_ref, hi_ref, a_copy, o_copy, compute):
    j = pl.program_id(0)
    e = pl.program_id(1)
    n_e = pl.num_programs(1)
    lo = lo_ref[e]
    hi = hi_ref[e]

    @pl.when(jnp.logical_and(jnp.logical_and(j == 0, e == 0), lo < hi))
    def _():
        a_copy(lo, 0).start()

    for ahead in range(1, LOAD_BUFS - 1):
        @pl.when(lo + ahead < hi)
        def _():
            a_copy(lo + ahead, ahead).start()

    def body(t, carry):
        slot_o = (t - lo) % 2
        a_copy(t, (t - lo) % LOAD_BUFS).wait()
        nxt = t + LOAD_BUFS - 1

        @pl.when(nxt < hi)
        def _():
            a_copy(nxt, (nxt - lo) % LOAD_BUFS).start()

        @pl.when(t - lo >= 2)
        def _():
            o_copy(t - 2, slot_o).wait()

        compute((t - lo) % LOAD_BUFS, slot_o)
        o_copy(t, slot_o).start()
        return carry

    lax.fori_loop(lo, hi, body, 0)

    e_next = jnp.where(e + 1 < n_e, e + 1, 0)
    last_step = jnp.logical_and(j == pl.num_programs(0) - 1, e == n_e - 1)

    @pl.when(jnp.logical_and(jnp.logical_not(last_step), lo_ref[e_next] < hi_ref[e_next]))
    def _():
        a_copy(lo_ref[e_next], 0).start()

    @pl.when(hi - lo >= 2)
    def _():
        o_copy(hi - 2, (hi - 2 - lo) % 2).wait()

    @pl.when(hi - lo >= 1)
    def _():
        o_copy(hi - 1, (hi - 1 - lo) % 2).wait()


def _zero_idle_tiles(first, last, zbuf, o_copy_from, sem):
    zbuf[...] = jnp.zeros_like(zbuf)

    def body(t, carry):
        cp = o_copy_from(zbuf, t, sem)
        cp.start()
        cp.wait()
        return carry

    lax.fori_loop(first, last, body, 0)


def _moe_up_kernel(lo_ref, hi_ref, xs_hbm, wg_ref, wu_ref, out_hbm, wgb_ref, wub_ref, abuf, obuf, zbuf,
                   asem, osem, zsem):
    j = pl.program_id(0)
    e = pl.program_id(1)
    tg = abuf.shape[1]
    tn = obuf.shape[2]
    col = pl.multiple_of(j * tn, tn)
    wgb_ref[...] = _b(wg_ref[...])
    wub_ref[...] = _b(wu_ref[...])

    class _Load:
        def __init__(self, t, slot):
            self.cp = pltpu.make_async_copy(xs_hbm.at[pl.ds(pl.multiple_of(t * tg, tg), tg)], abuf.at[slot],
                                            asem.at[slot])

        def start(self):
            self.cp.start(priority=1)

        def wait(self):
            self.cp.wait()

    def o_copy_from(src, t, sem):
        return pltpu.make_async_copy(src, out_hbm.at[pl.ds(pl.multiple_of(t * tg, tg), tg), pl.ds(col, tn)], sem)

    def o_copy(t, slot):
        return o_copy_from(obuf.at[slot], t, osem.at[slot])

    def compute(slot_a, slot_o):
        a = abuf[slot_a]
        obuf[slot_o] = (_silu(_dot(a, wgb_ref[...])) * _dot(a, wub_ref[...])).astype(obuf.dtype)

    _stream_tiles(lo_ref, hi_ref, _Load, o_copy, compute)

    @pl.when(e == pl.num_programs(1) - 1)
    def _():
        _zero_idle_tiles(hi_ref[e], out_hbm.shape[0] // tg, zbuf, o_copy_from, zsem)


def moe_up(xs, wg, wu, tile_lo, tile_hi):
    p, k = xs.shape
    n_e, _, f = wg.shape
    tg = MOE_TILE
    tn = _tile(f, (512, 256, 128))
    w_spec = pl.BlockSpec((None, k, tn), lambda j, e, lo, hi: (e, 0, j))
    any_spec = pl.BlockSpec(memory_space=pl.ANY)
    return pl.pallas_call(
        _moe_up_kernel,
        grid_spec=pltpu.PrefetchScalarGridSpec(
            num_scalar_prefetch=2,
            grid=(f // tn, n_e),
            in_specs=[any_spec, w_spec, w_spec],
            out_specs=any_spec,
            scratch_shapes=[pltpu.VMEM((k, tn), BF16), pltpu.VMEM((k, tn), BF16),
                            pltpu.VMEM((LOAD_BUFS, tg, k), BF16), pltpu.VMEM((2, tg, tn), BF16),
                            pltpu.VMEM((tg, tn), BF16), pltpu.SemaphoreType.DMA((LOAD_BUFS,)),
                            pltpu.SemaphoreType.DMA((2,)),
                            pltpu.SemaphoreType.DMA(())],
        ),
        out_shape=jax.ShapeDtypeStruct((p, f), BF16),
        compiler_params=_params(("arbitrary", "arbitrary")),
        name="moe_up",
    )(tile_lo, tile_hi, xs, wg, wu)


def _moe_down_kernel(te_ref, nused_ref, a_ref, w_ref, o_ref, wb_ref):
    t = pl.program_id(1)

    @pl.when(_new_expert(te_ref, t))
    def _():
        wb_ref[...] = _b(w_ref[...])

    @pl.when(t < nused_ref[0])
    def _():
        o_ref[...] = _dot(a_ref[...], wb_ref[...])

    @pl.when(t >= nused_ref[0])
    def _():
        o_ref[...] = jnp.zeros_like(o_ref)


def moe_down(fs, wd, tile_e, n_used):
    p, f = fs.shape
    d = wd.shape[-1]
    tg = MOE_TILE
    tn = _tile(d, (512, 256, 128))
    return pl.pallas_call(
        _moe_down_kernel,
        grid_spec=pltpu.PrefetchScalarGridSpec(
            num_scalar_prefetch=2,
            grid=(d // tn, p // tg),
            in_specs=[pl.BlockSpec((tg, f), lambda j, t, te, nu: (_used_tile(t, nu), 0)),
                      pl.BlockSpec((None, f, tn), lambda j, t, te, nu: (te[t], 0, j))],
            out_specs=pl.BlockSpec((tg, tn), lambda j, t, te, nu: (t, j)),
            scratch_shapes=[pltpu.VMEM((f, tn), BF16)],
        ),
        out_shape=jax.ShapeDtypeStruct((p, d), F32),
        compiler_params=_params(("arbitrary", "arbitrary")),
        name="moe_down",
    )(tile_e, n_used, fs, wd)


def _combine_norm_kernel(pos_ref, h_hbm, route_hbm, y_hbm, w_ref, o_ref, hbuf, rbuf, buf0, buf1, sem,
                         *, seq_t, m, tiles_per_seq):
    s = pl.program_id(0)
    tm = hbuf.shape[1]

    def first_row(step):
        return (step // tiles_per_seq) * seq_t + (N_PAD + N_META) + (step % tiles_per_seq) * tm

    def copies(base, slot):
        return (pltpu.make_async_copy(h_hbm.at[pl.ds(base, tm)], hbuf.at[slot], sem.at[slot, 0]),
                pltpu.make_async_copy(route_hbm.at[pl.ds(base, tm)], rbuf.at[slot], sem.at[slot, 1]))

    def issue(step, slot):
        base = first_row(step)
        for cp in copies(base, slot):
            cp.start()

        def body(r, carry):
            _row_copy(y_hbm, pos_ref[base + r], buf0.at[slot], r, sem.at[slot, 2]).start(priority=0)
            _row_copy(y_hbm, pos_ref[m + base + r], buf1.at[slot], r, sem.at[slot, 3]).start(priority=1)
            return carry

        lax.fori_loop(0, tm, body, 0, unroll=4)

    @pl.when(s == 0)
    def _():
        issue(0, 0)

    @pl.when(s + 1 < pl.num_programs(0))
    def _():
        issue(s + 1, (s + 1) % 2)

    slot = s % 2
    for cp in copies(0, slot):
        cp.wait()
    pltpu.make_async_copy(y_hbm.at[pl.ds(0, tm)], buf0.at[slot], sem.at[slot, 2]).wait()
    pltpu.make_async_copy(y_hbm.at[pl.ds(0, tm)], buf1.at[slot], sem.at[slot, 3]).wait()
    w1 = rbuf[slot, :, ROUTE_W1:ROUTE_W1 + 1]
    w2 = rbuf[slot, :, ROUTE_W2:ROUTE_W2 + 1]
    x = hbuf[slot] + (w1 * buf0[slot] + w2 * buf1[slot])
    o_ref[...] = (x * lax.rsqrt(jnp.mean(x * x, axis=-1, keepdims=True) + EPS) * w_ref[...]).astype(o_ref.dtype)


def combine_norm(h, route, ys, pos, w, batch, seq_t):
    m, d = h.shape
    seq = seq_t - N_PAD - N_META
    tm = _tile(seq, (256, 128, 64))
    any_spec = pl.BlockSpec(memory_space=pl.ANY)
    return pl.pallas_call(
        functools.partial(_combine_norm_kernel, seq_t=seq_t, m=m, tiles_per_seq=seq // tm),
        grid_spec=pltpu.PrefetchScalarGridSpec(
            num_scalar_prefetch=1,
            grid=(batch * seq // tm,),
            in_specs=[any_spec, any_spec, any_spec, pl.BlockSpec((1, d), lambda s, *_: (0, 0))],
            out_specs=pl.BlockSpec((tm, d), lambda s, *_: (s, 0)),
            scratch_shapes=[pltpu.VMEM((2, tm, d), F32), pltpu.VMEM((2, tm, ROUTER_PAD), F32),
                            pltpu.VMEM((2, tm, d), F32), pltpu.VMEM((2, tm, d), F32),
                            pltpu.SemaphoreType.DMA((2, 4))],
        ),
        out_shape=jax.ShapeDtypeStruct((batch * seq, d), F32),
        compiler_params=_params(("arbitrary",)),
        name="moe_combine",
    )(pos, h, route, ys, w.reshape(1, d))


def moe_routing(route, batch, seq_t):
    m = batch * seq_t
    tg = MOE_TILE
    n_tiles = (2 * m + tg - 1) // tg + N_EXPERTS
    e_flat = jnp.concatenate([route[:, ROUTE_I1], route[:, ROUTE_I2]]).astype(jnp.int32)
    tok = jnp.concatenate([jnp.arange(m, dtype=jnp.int32)] * 2)
    real = (tok % seq_t) >= (N_PAD + N_META)
    onehot = ((e_flat[:, None] == jnp.arange(N_EXPERTS, dtype=jnp.int32)[None, :]) & real[:, None]).astype(jnp.int32)
    csum = jnp.cumsum(onehot, axis=0)
    rank = jnp.sum(csum * onehot, axis=1) - 1
    counts = csum[-1]
    padded = ((counts + tg - 1) // tg) * tg
    ends = jnp.cumsum(padded)
    starts = ends - padded
    pos = jnp.sum(starts[None, :] * onehot, axis=1) + rank
    pos = jnp.where(real, pos, n_tiles * tg)
    src = jnp.zeros((n_tiles * tg,), jnp.int32).at[pos].set(tok, mode="drop")
    n_used = (ends[-1] // tg).astype(jnp.int32)
    tile_start = jnp.arange(n_tiles, dtype=jnp.int32) * tg
    tile_e = jnp.sum((tile_start[:, None] >= ends[None, :]).astype(jnp.int32), axis=1)
    last_e = jnp.sum((((n_used - 1) * tg) >= ends).astype(jnp.int32))
    tile_e = jnp.where(jnp.arange(n_tiles) < n_used, tile_e, last_e)
    return src, pos, (starts // tg, ends // tg), tile_e, n_used.reshape(1), n_tiles


def kernel(x, meta_tokens, e_norm_mix, e_w_in, ml_i_bias, ml_f_bias, ml_out_norm, rw_mu, rw_w0, rw_w_up, rw_a0, rw_a_up, rw_g_up, rw_k_k, rw_k_a, rw_r_k, rw_ln_w, rw_ln_b, e_w_out, e_norm_ffn, ffn_w_gate, ffn_w_up, ffn_w_down, o_norm_mix, ssd_w_in, ssd_conv_w, ssd_conv_b, ssd_dt_bias, ssd_A_log, ssd_D, ssd_norm, ssd_w_out, o_norm_ffn, moe_router, moe_w_gate, moe_w_up, moe_w_down, final_norm):
    batch, seq, d = x.shape
    seq_t = N_PAD + N_META + seq
    m = batch * seq_t
    h = jnp.concatenate([jnp.zeros((batch, N_PAD, d), x.dtype),
                         jnp.broadcast_to(meta_tokens.astype(x.dtype), (batch, N_META, d)), x], axis=1)
    h = h.reshape(m, d)

    hn = rmsnorm(h, e_norm_mix[0], BF16)
    p = mm_up(hn, e_w_in[0].T, F32, 512).reshape(batch, seq_t, -1)
    y_ml = mlstm_group(p, ml_i_bias[0], ml_f_bias[0], ml_out_norm[0]).reshape(m, ML_V)
    y_rw = rwkv7_group(p, rw_mu[0], rw_w0[0], rw_w_up[0], rw_a0[0], rw_a_up[0], rw_g_up[0], rw_k_k[0],
                       rw_k_a[0], rw_r_k[0], rw_ln_w[0], rw_ln_b[0]).reshape(m, RW_DIM)
    h = mm_down([y_ml, y_rw], _b(e_w_out[0]), h, seq_t)
    hn = rmsnorm(h, e_norm_ffn[0], BF16)
    f = swiglu_up(hn, ffn_w_gate[0], ffn_w_up[0])
    h = mm_down([f], _b(ffn_w_down[0]), h, seq_t)

    hn = rmsnorm(h, o_norm_mix[0], BF16)
    p = mm_up(hn, ssd_w_in[0].T, F32, 1024).reshape(batch, seq_t, -1)
    y = ssd_mixer(p, ssd_conv_w[0], ssd_conv_b[0], ssd_dt_bias[0], ssd_A_log[0], ssd_D[0], ssd_norm[0])
    h = mm_down([y.reshape(m, SSD_DI)], _b(ssd_w_out[0]), h, seq_t)
    route = router_top2(h, o_norm_ffn[0], moe_router[0])
    src, pos, (tile_lo, tile_hi), tile_e, n_used, n_tiles = moe_routing(route, batch, seq_t)
    xs = gather_norm(h, o_norm_ffn[0], src, n_used, n_tiles)
    fs = moe_up(xs, moe_w_gate[0], moe_w_up[0], tile_lo, tile_hi)
    ys = moe_down(fs, moe_w_down[0], tile_e, n_used)
    out = combine_norm(h, route, ys, pos, final_norm, batch, seq_t)
    return out.reshape(batch, seq, d).astype(x.dtype)
```

```python
import functools

import jax
import jax.numpy as jnp
from jax import lax
from jax.experimental import pallas as pl
from jax.experimental.pallas import tpu as pltpu

F32 = jnp.float32
BF16 = jnp.bfloat16
HI = lax.Precision.HIGHEST

N_META = 16
CHUNK = 64
N_PAD = CHUNK - N_META
NEG = -1e30
EPS = 1e-6

ML_HEADS = 4
ML_DQK = 128
ML_DV = 256
ML_GATE_CAP = 15.0
ML_QK = ML_HEADS * ML_DQK
ML_V = ML_HEADS * ML_DV

RW_HEADS = 16
RW_N = 64
RW_DIM = RW_HEADS * RW_N
RW_DECAY_LORA = 64
RW_A_LORA = 64
RW_GATE_LORA = 160
RW_LN_EPS = 64e-5
RW_GROUP = 256
RW_HPG = RW_GROUP // RW_N
RW_GPS = 4
LORA_ROWS = 384

E_OFF_GATES = 2 * ML_QK + 2 * ML_V
E_OFF_RW = E_OFF_GATES + 2 * ML_HEADS
RW_LANE_OFF = E_OFF_RW % 128

SSD_DI = 4096
SSD_P = 64
SSD_H = SSD_DI // SSD_P
SSD_N = 128
SSD_G = 8
SSD_R = SSD_H // SSD_G
SSD_K = 4
SSD_GW = SSD_DI // SSD_G
SSD_BC = 2 * SSD_G * SSD_N
SSD_DT_PAD = 128

N_EXPERTS = 8
ROUTER_PAD = 128

VMEM_LIMIT = 56 * 1024 * 1024


def _tile(n, prefs):
    for p in prefs:
        if n % p == 0:
            return p
    return n


_ROW_TILES = (1040, 832, 640, 520, 512, 256, 128, 64)


def _params(sem):
    return pltpu.CompilerParams(dimension_semantics=sem, vmem_limit_bytes=VMEM_LIMIT)


def _dot(a, b, prec=None):
    return lax.dot_general(a, b, (((1,), (0,)), ((), ())), precision=prec, preferred_element_type=F32)


def _dot_nt(a, b, prec=None):
    return lax.dot_general(a, b, (((1,), (1,)), ((), ())), precision=prec, preferred_element_type=F32)


def _dot_tn(a, b, prec=None):
    return lax.dot_general(a, b, (((0,), (0,)), ((), ())), precision=prec, preferred_element_type=F32)


def _b(x):
    return x.astype(BF16)


def _bdot(dot, a, b):
    return dot(_b(a), _b(b))


def _dot_split(a, b):
    hi = _b(a)
    lo = _b(a - hi.astype(F32))
    bb = _b(b)
    return _dot(hi, bb) + _dot(lo, bb)


def _sigmoid(x):
    return 1.0 / (1.0 + jnp.exp(-x))


def _softplus(x):
    return jnp.maximum(x, 0.0) + jnp.log1p(jnp.exp(-jnp.abs(x)))


def _silu(x):
    return x * _sigmoid(x)


def _iota(shape, dim):
    return lax.broadcasted_iota(jnp.int32, shape, dim)


def _tri(n):
    return (_iota((n, n), 0) >= _iota((n, n), 1)).astype(F32)


def _interleave(chains):
    chains = list(chains)
    while chains:
        for ch in list(chains):
            try:
                next(ch)
            except StopIteration:
                chains.remove(ch)


def _rmsnorm_kernel(x_ref, w_ref, o_ref):
    x = x_ref[...]
    ms = jnp.mean(x * x, axis=-1, keepdims=True)
    o_ref[...] = (x * lax.rsqrt(ms + EPS) * w_ref[...]).astype(o_ref.dtype)


def rmsnorm(x, w, out_dtype):
    m, d = x.shape
    tm = _tile(m, _ROW_TILES)
    return pl.pallas_call(
        _rmsnorm_kernel,
        grid=(m // tm,),
        in_specs=[pl.BlockSpec((tm, d), lambda i: (i, 0)), pl.BlockSpec((1, d), lambda i: (0, 0))],
        out_specs=pl.BlockSpec((tm, d), lambda i: (i, 0)),
        out_shape=jax.ShapeDtypeStruct((m, d), out_dtype),
        compiler_params=_params(("arbitrary",)),
        name="rmsnorm",
    )(x, w.reshape(1, d))


def _mm_up_kernel(a_ref, wt_ref, o_ref, wb_ref, *, n_valid, last_w):
    j = pl.program_id(0)
    tn = wt_ref.shape[0]

    @pl.when(pl.program_id(1) == 0)
    def _():
        row = j * tn + _iota((tn, 1), 0)
        wt = jnp.where(row < n_valid, wt_ref[...], 0.0)
        wb_ref[...] = _b(wt.T)

    if last_w == tn:
        o_ref[...] = _dot(a_ref[...], wb_ref[...]).astype(o_ref.dtype)
    else:
        @pl.when(j < pl.num_programs(0) - 1)
        def _():
            o_ref[...] = _dot(a_ref[...], wb_ref[...]).astype(o_ref.dtype)

        @pl.when(j == pl.num_programs(0) - 1)
        def _():
            o_ref[:, 0:last_w] = _dot(a_ref[...], wb_ref[:, 0:last_w]).astype(o_ref.dtype)


def mm_up(a, wt, out_dtype, tn):
    m, k = a.shape
    n = wt.shape[0]
    tm = _tile(m, _ROW_TILES)
    nt = pl.cdiv(n, tn)
    last_w = pl.cdiv(n - (nt - 1) * tn, 128) * 128
    return pl.pallas_call(
        functools.partial(_mm_up_kernel, n_valid=n, last_w=last_w),
        grid=(nt, m // tm),
        in_specs=[pl.BlockSpec((tm, k), lambda j, i: (i, 0)), pl.BlockSpec((tn, k), lambda j, i: (j, 0))],
        out_specs=pl.BlockSpec((tm, tn), lambda j, i: (i, j)),
        out_shape=jax.ShapeDtypeStruct((m, (nt - 1) * tn + last_w), out_dtype),
        scratch_shapes=[pltpu.VMEM((k, tn), BF16)],
        compiler_params=_params(("arbitrary", "arbitrary")),
        name="mm_up",
    )(a, wt)


def _swiglu_kernel(a_ref, wg_ref, wu_ref, o_ref, wgb_ref, wub_ref):
    @pl.when(pl.program_id(1) == 0)
    def _():
        wgb_ref[...] = _b(wg_ref[...])
        wub_ref[...] = _b(wu_ref[...])

    a = a_ref[...]
    o_ref[...] = (_silu(_dot(a, wgb_ref[...])) * _dot(a, wub_ref[...])).astype(o_ref.dtype)


def swiglu_up(a, wg, wu):
    m, k = a.shape
    f = wg.shape[-1]
    tm = _tile(m, _ROW_TILES)
    tn = _tile(f, (512, 256, 128))
    w_spec = pl.BlockSpec((k, tn), lambda j, i: (0, j))
    return pl.pallas_call(
        _swiglu_kernel,
        grid=(f // tn, m // tm),
        in_specs=[pl.BlockSpec((tm, k), lambda j, i: (i, 0)), w_spec, w_spec],
        out_specs=pl.BlockSpec((tm, tn), lambda j, i: (i, j)),
        out_shape=jax.ShapeDtypeStruct((m, f), BF16),
        scratch_shapes=[pltpu.VMEM((k, tn), BF16), pltpu.VMEM((k, tn), BF16)],
        compiler_params=_params(("arbitrary", "arbitrary")),
        name="swiglu_up",
    )(a, wg, wu)


def _mm_down_kernel(*refs, tm, seq_t):
    *a_refs, w_ref, r_ref, o_ref = refs
    acc = None
    off = 0
    for a_ref in a_refs:
        k = a_ref.shape[1]
        part = _dot(a_ref[...], w_ref[off:off + k, :])
        acc = part if acc is None else acc + part
        off += k
    t0 = (pl.program_id(0) * tm) % seq_t
    valid = ((t0 + _iota((tm, 1), 0)) >= N_PAD).astype(F32)
    o_ref[...] = r_ref[...] + acc * valid


def mm_down(a_parts, w, res, seq_t):
    m = a_parts[0].shape[0]
    k, n = w.shape
    assert sum(a.shape[1] for a in a_parts) == k
    tm = _tile(seq_t, (520, 512, 256, 128, 64))
    tn = _tile(n, (2048, 1024, 512, 256, 128) if k <= 2048 else (1024, 512, 256, 128))
    a_specs = [pl.BlockSpec((tm, a.shape[1]), lambda i, j: (i, 0)) for a in a_parts]
    return pl.pallas_call(
        functools.partial(_mm_down_kernel, tm=tm, seq_t=seq_t),
        grid=(m // tm, n // tn),
        in_specs=a_specs + [pl.BlockSpec((k, tn), lambda i, j: (0, j)), pl.BlockSpec((tm, tn), lambda i, j: (i, j))],
        out_specs=pl.BlockSpec((tm, tn), lambda i, j: (i, j)),
        out_shape=jax.ShapeDtypeStruct((m, n), F32),
        compiler_params=_params(("arbitrary", "arbitrary")),
        name="mm_down",
    )(*a_parts, w, res)


def _log_sigmoid(x):
    return -_softplus(-x)


def _mlstm_kernel(q_ref, k_ref, v_ref, o_ref, g_ref, bias_ref, onorm_ref, out_ref, c_ref, n_ref, m_ref):
    c = pl.program_id(0)
    L = CHUNK

    @pl.when(c == 0)
    def _():
        c_ref[...] = jnp.zeros_like(c_ref)
        n_ref[...] = jnp.zeros_like(n_ref)
        m_ref[...] = jnp.zeros_like(m_ref)

    valid = (c * L + _iota((L, 1), 0)) >= N_PAD
    causal = _iota((L, L), 0) >= _iota((L, L), 1)
    tri = _tri(L)
    scale = ML_DQK ** -0.5

    def chain(b, h, bcum, ig, b_t, i_t):
        q = q_ref[b, :, h * ML_DQK:(h + 1) * ML_DQK] * scale
        k = k_ref[b, :, h * ML_DQK:(h + 1) * ML_DQK]
        v = v_ref[b, :, h * ML_DV:(h + 1) * ML_DV]
        og = o_ref[b, :, h * ML_DV:(h + 1) * ML_DV]
        bcol = bcum[:, ML_HEADS + h:ML_HEADS + h + 1]
        icol = ig[:, h:h + 1]
        brow = b_t[ML_HEADS + h:ML_HEADS + h + 1, :]
        irow = i_t[h:h + 1, :]
        g = bcol[L - 1:L, :]
        m_prev = m_ref[b, h:h + 1, 0:1]
        c_prev = c_ref[b, h]
        n_prev = n_ref[b, h:h + 1, :]

        dm = jnp.where(causal, bcol - brow + irow, NEG)
        m_inter = bcol + m_prev
        m_t = jnp.maximum(m_inter, jnp.max(dm, axis=-1, keepdims=True))
        yield
        s = _dot_nt(_b(q), _b(k)) * jnp.exp(dm - m_t)
        w_inter = jnp.exp(m_inter - m_t)
        yield
        num = _dot(_b(s), _b(v)) + w_inter * _dot(_b(q), _b(c_prev))
        den = jnp.sum(s, axis=-1, keepdims=True) + w_inter * jnp.sum(q * n_prev, axis=-1, keepdims=True)
        yield
        hh = num / jnp.maximum(jnp.abs(den), jnp.exp(-m_t))
        hh = hh * lax.rsqrt(jnp.mean(hh * hh, axis=-1, keepdims=True) + EPS)
        hh = hh * onorm_ref[:, h * ML_DV:(h + 1) * ML_DV] * _sigmoid(og)
        out_ref[b, :, h * ML_DV:(h + 1) * ML_DV] = hh.astype(out_ref.dtype)
        yield

        a_col = g - bcol + icol
        m_new = jnp.maximum(g + m_prev, jnp.max(a_col, axis=0, keepdims=True))
        dec = jnp.exp(g + m_prev - m_new)
        wk = jnp.exp(a_col - m_new) * k
        yield
        c_ref[b, h] = dec * c_prev + _dot_tn(_b(wk), _b(v))
        n_ref[b, h:h + 1, :] = dec * n_prev + jnp.sum(wk, axis=0, keepdims=True)
        m_ref[b, h:h + 1, :] = jnp.broadcast_to(m_new, (1, m_ref.shape[2]))

    chains = []
    for b in range(q_ref.shape[0]):
        gates = g_ref[b] + bias_ref[...]
        sc = ML_GATE_CAP * jnp.tanh(gates / ML_GATE_CAP)
        ig = jnp.where(valid, sc, NEG)
        lf = jnp.where(valid, _log_sigmoid(sc), 0.0)
        bcum = _dot(tri, lf, HI)
        chains += [chain(b, h, bcum, ig, bcum.T, ig.T) for h in range(ML_HEADS)]
    _interleave(chains)


def mlstm_group(p, i_bias, f_bias, out_norm):
    batch, seq_t, _ = p.shape
    nc = seq_t // CHUNK
    bias = jnp.zeros((1, 128), F32).at[0, 0:ML_HEADS].set(i_bias).at[0, ML_HEADS:2 * ML_HEADS].set(f_bias)
    return pl.pallas_call(
        _mlstm_kernel,
        grid=(nc,),
        in_specs=[
            pl.BlockSpec((batch, CHUNK, ML_QK), lambda c: (0, c, 0)),
            pl.BlockSpec((batch, CHUNK, ML_QK), lambda c: (0, c, 1)),
            pl.BlockSpec((batch, CHUNK, ML_V), lambda c: (0, c, 1)),
            pl.BlockSpec((batch, CHUNK, ML_V), lambda c: (0, c, 2)),
            pl.BlockSpec((batch, CHUNK, 128), lambda c: (0, c, E_OFF_GATES // 128)),
            pl.BlockSpec((1, 128), lambda c: (0, 0)),
            pl.BlockSpec((1, ML_V), lambda c: (0, 0)),
        ],
        out_specs=pl.BlockSpec((batch, CHUNK, ML_V), lambda c: (0, c, 0)),
        out_shape=jax.ShapeDtypeStruct((batch, seq_t, ML_V), BF16),
        scratch_shapes=[pltpu.VMEM((batch, ML_HEADS, ML_DQK, ML_DV), F32), pltpu.VMEM((batch, 8, ML_DQK), F32),
                        pltpu.VMEM((batch, 8, 128), F32)],
        compiler_params=_params(("arbitrary",)),
        name="mlstm",
    )(p, p, p, p, p, bias, out_norm.reshape(1, ML_V))


def _rwkv_kernel(*refs):
    nb = 2 * RW_GPS + 1
    r_blk, k_blk, v_blk = refs[0:nb], refs[nb:2 * nb], refs[2 * nb:3 * nb]
    tail_ref, mu3_ref, mut_ref, chan_ref, wl_ref, out_ref, h_ref, cr_ref, ck_ref, cv_ref, ct_ref = refs[3 * nb:]
    c = pl.program_id(1)
    L = CHUNK
    G = RW_GROUP

    @pl.when(c == 0)
    def _():
        h_ref[...] = jnp.zeros_like(h_ref)
        cr_ref[...] = jnp.zeros_like(cr_ref)
        ck_ref[...] = jnp.zeros_like(ck_ref)
        cv_ref[...] = jnp.zeros_like(cv_ref)
        ct_ref[...] = jnp.zeros_like(ct_ref)

    row = _iota((L, 1), 0)
    valid = (c * L + row) >= N_PAD

    ri = _iota((G, G), 0)
    ci = _iota((G, G), 1)
    same = (ri >> 6) == (ci >> 6)
    tl = ri & (RW_N - 1)
    sl = ci & (RW_N - 1)
    bd_mask = same.astype(F32)
    eye = (ri == ci).astype(F32)
    lower = same & (tl >= sl)
    strict = same & (tl > sl)
    ones_bd = _b(bd_mask)
    tri = _tri(L)
    lane = _iota((L, LORA_ROWS), 1)

    def head_sum(x):
        hi = _b(x)
        return _dot(hi, ones_bd) + _dot(_b(x - hi.astype(F32)), ones_bd)

    def bd(x):
        return _b(jnp.concatenate([x] * RW_HPG, axis=0) * bd_mask)

    def shift_lerp(x, carry_ref, idx, mu):
        prev = carry_ref[idx]
        xs = jnp.where(row == 0, prev, pltpu.roll(x, 1, 0))
        carry_ref[idx] = x[L - 1:L, :]
        return x + mu * (xs - x)

    def unshift(blocks, b, j):
        x = jnp.concatenate([blk[b] for blk in blocks[2 * j:2 * j + 3]], axis=1)
        return x[:, RW_LANE_OFF:RW_LANE_OFF + G]

    def chain(b, j):
        gl = slice(j * G, (j + 1) * G)
        w0 = chan_ref[0:1, gl]
        a0 = chan_ref[1:2, gl]
        k_k = chan_ref[2:3, gl]
        k_a = chan_ref[3:4, gl]
        r_k = chan_ref[4:5, gl]
        ln_w = chan_ref[5:6, gl]
        ln_b = chan_ref[6:7, gl]
        r = shift_lerp(unshift(r_blk, b, j), cr_ref, (b, j), mu3_ref[0:1, gl])
        k = shift_lerp(unshift(k_blk, b, j), ck_ref, (b, j), mu3_ref[1:2, gl])
        v = shift_lerp(unshift(v_blk, b, j), cv_ref, (b, j), mu3_ref[2:3, gl])
        tail = shift_lerp(tail_ref[b], ct_ref, (b, j), mut_ref[...])

        lo1 = RW_LANE_OFF + RW_DECAY_LORA
        lo2 = lo1 + RW_A_LORA
        lo3 = lo2 + RW_GATE_LORA
        act = jnp.where(lane < RW_LANE_OFF, 0.0,
                        jnp.where(lane < lo1, jnp.tanh(tail),
                                  jnp.where(lane < lo2, tail, jnp.where(lane < lo3, _sigmoid(tail), 0.0))))
        lw = _dot(act[:, 0:128], wl_ref[0, 0:128, gl], HI)
        la = _bdot(_dot, act[:, 0:256], wl_ref[1, 0:256, gl])
        gate = _bdot(_dot, act[:, 128:384], wl_ref[2, 128:384, gl])
        yield

        w_log = -_softplus(-(w0 + lw)) - 0.5
        ld = jnp.where(valid, -jnp.exp(w_log), 0.0)
        a = _sigmoid(a0 + la)
        kk = k * k_k
        kk = kk * lax.rsqrt(jnp.maximum(head_sum(kk * kk), 1e-24))
        k2 = k * (1.0 + (a - 1.0) * k_a)
        k2 = jnp.where(valid, k2, 0.0)
        kk = jnp.where(valid, kk, 0.0)
        va = -kk
        vb = kk * a

        yield
        cum = _dot(tri, ld, HI)
        cum_l = cum[L - 1:L, :]
        w_t = jnp.exp(cum)
        w_inv = jnp.exp(-cum)
        w_prev = jnp.exp(cum - ld)
        w_end = jnp.exp(cum_l - cum)
        w_last = jnp.exp(cum_l)

        a_bd = bd(va * w_prev)
        r_bd = bd(r * w_t)
        b_bd = bd(vb * w_inv)
        k_bd = bd(k2 * w_inv)
        bl_bd = bd(vb * w_end)
        kl_bd = bd(k2 * w_end)
        v_bd = bd(v)
        yield

        n_ab = jnp.where(strict, _dot_nt(a_bd, b_bd), 0.0)
        a_ak = _b(jnp.where(strict, _dot_nt(a_bd, k_bd), 0.0))
        m_rb = _b(jnp.where(lower, _dot_nt(r_bd, b_bd), 0.0))
        m_rk = _b(jnp.where(lower, _dot_nt(r_bd, k_bd), 0.0))
        yield

        t_inv = eye + n_ab
        pw = n_ab
        for _ in range(5):
            pw_b = _b(pw)
            pw = _dot(pw_b, pw_b)
            yield
            t_inv = t_inv + _bdot(_dot, t_inv, pw)
            yield

        t_b = _b(t_inv)
        av = _dot(a_ak, v_bd)
        yield
        a_p = _b(_dot(t_b, a_bd))
        u0 = _b(_dot(t_b, _b(av)))
        yield
        p_mat = eye * w_last + _dot_tn(bl_bd, a_p)
        q_mat = _dot_tn(bl_bd, u0) + _dot_tn(kl_bd, v_bd)
        r_p = r_bd.astype(F32) + _dot(m_rb, a_p)
        y0 = _dot(m_rb, u0) + _dot(m_rk, v_bd)
        yield

        h0 = _b(h_ref[b, j])
        y_bd = _dot(_b(r_p), h0) + y0
        h_ref[b, j] = _dot(_b(p_mat), h0) + q_mat
        y = y_bd[0:L] + y_bd[L:2 * L] + y_bd[2 * L:3 * L] + y_bd[3 * L:4 * L]
        yield

        inv_n = 1.0 / RW_N
        mean = head_sum(y) * inv_n
        d = y - mean
        var = head_sum(d * d) * inv_n
        yn = d * lax.rsqrt(var + RW_LN_EPS) * ln_w + ln_b
        bonus = head_sum(r * k2 * r_k) * v
        out_ref[b, :, gl] = ((yn + bonus) * gate).astype(out_ref.dtype)

    _interleave(chain(b, j) for b in range(out_ref.shape[0]) for j in range(RW_GPS))


def rwkv7_group(p, mu, w0, w_up, a0, a_up, g_up, k_k, k_a, r_k, ln_w, ln_b):
    batch, seq_t, _ = p.shape
    nc = seq_t // CHUNK
    ng = RW_DIM // RW_GROUP
    n_lora = RW_DECAY_LORA + RW_A_LORA + RW_GATE_LORA
    o = RW_LANE_OFF
    mu3 = mu[:3 * RW_DIM].reshape(3, RW_DIM)
    mut = jnp.zeros((1, LORA_ROWS), F32).at[0, o:o + n_lora].set(mu[3 * RW_DIM:])
    chan = jnp.stack([w0, a0, k_k, k_a, r_k.reshape(RW_DIM), ln_w, ln_b, jnp.zeros((RW_DIM,), F32)])
    wl = jnp.zeros((3, LORA_ROWS, RW_DIM), F32)
    wl = wl.at[0, o:o + RW_DECAY_LORA].set(w_up)
    wl = wl.at[1, o + RW_DECAY_LORA:o + RW_DECAY_LORA + RW_A_LORA].set(a_up)
    wl = wl.at[2, o + RW_DECAY_LORA + RW_A_LORA:o + n_lora].set(g_up)

    gw = RW_GPS * RW_GROUP

    def col(seg, part):
        first = (E_OFF_RW + seg * RW_DIM - o) // 128
        return pl.BlockSpec((batch, CHUNK, 128), lambda g, c: (0, c, first + 2 * RW_GPS * g + part))

    nb = 2 * RW_GPS + 1
    cols = [col(seg, part) for seg in range(3) for part in range(nb)]
    carry = lambda width: pltpu.VMEM((batch, RW_GPS, 1, width), F32)
    return pl.pallas_call(
        _rwkv_kernel,
        grid=(ng // RW_GPS, nc),
        in_specs=cols + [
            pl.BlockSpec((batch, CHUNK, LORA_ROWS), lambda g, c: (0, c, (E_OFF_RW + 3 * RW_DIM - o) // LORA_ROWS)),
            pl.BlockSpec((3, gw), lambda g, c: (0, g)),
            pl.BlockSpec((1, LORA_ROWS), lambda g, c: (0, 0)),
            pl.BlockSpec((8, gw), lambda g, c: (0, g)),
            pl.BlockSpec((3, LORA_ROWS, gw), lambda g, c: (0, 0, g)),
        ],
        out_specs=pl.BlockSpec((batch, CHUNK, gw), lambda g, c: (0, c, g)),
        out_shape=jax.ShapeDtypeStruct((batch, seq_t, RW_DIM), BF16),
        scratch_shapes=[pltpu.VMEM((batch, RW_GPS, RW_GROUP, RW_GROUP), F32), carry(RW_GROUP), carry(RW_GROUP),
                        carry(RW_GROUP), carry(LORA_ROWS)],
        compiler_params=_params(("arbitrary", "arbitrary")),
        name="rwkv7",
    )(*([p] * (3 * nb + 1)), mu3, mut, chan, wl)


def _ssd_kernel(z_ref, xs_ref, bc_ref, dt_ref, cwx_ref, cwb_ref, cbx_ref, cbb_ref, hp_ref, dexp_ref, nw_ref,
                out_ref, s_ref, px_ref, pb_ref):
    c = pl.program_id(0)
    L = CHUNK

    @pl.when(c == 0)
    def _():
        s_ref[...] = jnp.zeros_like(s_ref)
        px_ref[...] = jnp.zeros_like(px_ref)
        pb_ref[...] = jnp.zeros_like(pb_ref)

    row8 = _iota((8, 1), 0)
    valid = (c * L + _iota((L, 1), 0)) >= N_PAD
    lane = _iota((L, 2 * SSD_P), 1)
    causal = _iota((L, L), 0) >= _iota((L, L), 1)
    tri = _tri(L)

    def conv_silu(x, prev_ref, b, w_ref, b_ref):
        prev = prev_ref[b]
        acc = x * w_ref[SSD_K - 1:SSD_K, :] + b_ref[...]
        for j in range(1, SSD_K):
            xr = pltpu.roll(x, j, 0)
            pr = pltpu.roll(prev, j, 0)
            top = jnp.where(row8 < j, pr, xr[0:8])
            sh = jnp.concatenate([top, xr[8:]], axis=0)
            acc = acc + sh * w_ref[SSD_K - 1 - j:SSD_K - j, :]
        prev_ref[b] = x[L - 8:L, :]
        return _silu(acc)

    def expand(x):
        parts = []
        for j in range(SSD_H // 2):
            lo = jnp.broadcast_to(x[:, 2 * j:2 * j + 1], (L, 2 * SSD_P))
            hi = jnp.broadcast_to(x[:, 2 * j + 1:2 * j + 2], (L, 2 * SSD_P))
            parts.append(jnp.where(lane < SSD_P, lo, hi))
        return jnp.concatenate(parts, axis=1)

    def chain(b):
        xs = conv_silu(xs_ref[b], px_ref, b, cwx_ref, cbx_ref)
        yield
        bc = conv_silu(bc_ref[b], pb_ref, b, cwb_ref, cbb_ref)
        yield

        dt = jnp.where(valid, _softplus(dt_ref[b] + hp_ref[0:1, :]), 0.0)
        a = dt * (-jnp.exp(hp_ref[1:2, :]))
        acum = _dot(tri, a, HI)
        a_t = acum.T
        dt_t = dt.T
        a_last = acum[L - 1:L, :]
        cdec = jnp.exp(a_last)
        yield
        xw = xs * expand(dt * jnp.exp(a_last - acum))
        yield
        e_ea = expand(jnp.exp(acum))
        yield

        ys = []
        for g in range(SSD_G):
            bg = bc[:, g * SSD_N:(g + 1) * SSD_N]
            cg = bc[:, SSD_G * SSD_N + g * SSD_N:SSD_G * SSD_N + (g + 1) * SSD_N]
            cb = _dot_nt(_b(cg), _b(bg))
            s_prev = s_ref[b, g]
            y_inter = _dot_nt(_b(cg), _b(s_prev)) * e_ea[:, g * SSD_GW:(g + 1) * SSD_GW]
            yield
            parts = []
            cd_rows = []
            for r in range(SSD_R):
                h = g * SSD_R + r
                seg = acum[:, h:h + 1] - a_t[h:h + 1, :]
                ldec = jnp.exp(jnp.where(causal, seg, NEG))
                parts.append(_dot(_b(cb * ldec * dt_t[h:h + 1, :]), _b(xs[:, h * SSD_P:(h + 1) * SSD_P])))
                cd_rows.append(jnp.broadcast_to(cdec[:, h:h + 1], (SSD_P, SSD_N)))
                if r % 2:
                    yield
            ys.append(jnp.concatenate(parts, axis=1) + y_inter)
            st = _dot_tn(_b(xw[:, g * SSD_GW:(g + 1) * SSD_GW]), _b(bg))
            s_ref[b, g] = jnp.concatenate(cd_rows, axis=0) * s_prev + st
            yield

        y = jnp.concatenate(ys, axis=1) + dexp_ref[...] * xs
        y = y * _silu(z_ref[b])
        yield
        outs = []
        for g in range(SSD_G):
            yg = y[:, g * SSD_GW:(g + 1) * SSD_GW]
            outs.append(yg * lax.rsqrt(jnp.mean(yg * yg, axis=-1, keepdims=True) + EPS))
        out_ref[b] = (jnp.concatenate(outs, axis=1) * nw_ref[...]).astype(out_ref.dtype)

    _interleave(chain(b) for b in range(out_ref.shape[0]))


def ssd_mixer(p, conv_w, conv_b, dt_bias, a_log, d_skip, norm_w):
    batch, seq_t, _ = p.shape
    nc = seq_t // CHUNK
    hp = jnp.zeros((8, SSD_DT_PAD), F32).at[0, :SSD_H].set(dt_bias).at[1, :SSD_H].set(a_log)
    dexp = jnp.repeat(d_skip, SSD_P).reshape(1, SSD_DI)
    conv_b = conv_b.reshape(1, -1)
    return pl.pallas_call(
        _ssd_kernel,
        grid=(nc,),
        in_specs=[
            pl.BlockSpec((batch, CHUNK, SSD_DI), lambda c: (0, c, 0)),
            pl.BlockSpec((batch, CHUNK, SSD_DI), lambda c: (0, c, 1)),
            pl.BlockSpec((batch, CHUNK, SSD_BC), lambda c: (0, c, 2 * SSD_DI // SSD_BC)),
            pl.BlockSpec((batch, CHUNK, SSD_DT_PAD), lambda c: (0, c, (2 * SSD_DI + SSD_BC) // SSD_DT_PAD)),
            pl.BlockSpec((SSD_K, SSD_DI), lambda c: (0, 0)),
            pl.BlockSpec((SSD_K, SSD_BC), lambda c: (0, SSD_DI // SSD_BC)),
            pl.BlockSpec((1, SSD_DI), lambda c: (0, 0)),
            pl.BlockSpec((1, SSD_BC), lambda c: (0, SSD_DI // SSD_BC)),
            pl.BlockSpec((8, SSD_DT_PAD), lambda c: (0, 0)),
            pl.BlockSpec((1, SSD_DI), lambda c: (0, 0)),
            pl.BlockSpec((1, SSD_DI), lambda c: (0, 0)),
        ],
        out_specs=pl.BlockSpec((batch, CHUNK, SSD_DI), lambda c: (0, c, 0)),
        out_shape=jax.ShapeDtypeStruct((batch, seq_t, SSD_DI), BF16),
        scratch_shapes=[pltpu.VMEM((batch, SSD_G, SSD_GW, SSD_N), F32), pltpu.VMEM((batch, 8, SSD_DI), F32),
                        pltpu.VMEM((batch, 8, SSD_BC), F32)],
        compiler_params=_params(("arbitrary",)),
        name="ssd",
    )(p, p, p, p, conv_w, conv_w, conv_b, conv_b, hp, dexp, norm_w.reshape(1, SSD_DI))


MOE_TILE = 512
ROUTE_W1, ROUTE_W2, ROUTE_I1, ROUTE_I2 = 0, 1, 2, 3


def _router_kernel(x_ref, w_ref, r_ref, o_ref):
    x = x_ref[...]
    hn = x * lax.rsqrt(jnp.mean(x * x, axis=-1, keepdims=True) + EPS) * w_ref[...]
    logits = _dot(hn, r_ref[...], HI)
    lane = _iota(logits.shape, 1)
    lane_f = lane.astype(F32)
    logits = jnp.where(lane < N_EXPERTS, logits, -jnp.inf)
    m1 = jnp.max(logits, axis=-1, keepdims=True)
    i1 = jnp.min(jnp.where(logits == m1, lane_f, float(ROUTER_PAD)), axis=-1, keepdims=True)
    rest = jnp.where(lane_f == i1, -jnp.inf, logits)
    m2 = jnp.max(rest, axis=-1, keepdims=True)
    i2 = jnp.min(jnp.where(rest == m2, lane_f, float(ROUTER_PAD)), axis=-1, keepdims=True)
    e2 = jnp.exp(m2 - m1)
    w1 = 1.0 / (1.0 + e2)
    w2 = e2 / (1.0 + e2)
    o_ref[...] = jnp.where(lane == ROUTE_W1, w1, jnp.where(lane == ROUTE_W2, w2,
                           jnp.where(lane == ROUTE_I1, i1, jnp.where(lane == ROUTE_I2, i2, 0.0))))


def router_top2(x, w, router):
    m, d = x.shape
    tm = _tile(m, (520, 512, 256, 128, 64))
    rpad = jnp.zeros((d, ROUTER_PAD), F32).at[:, :N_EXPERTS].set(router)
    return pl.pallas_call(
        _router_kernel,
        grid=(m // tm,),
        in_specs=[pl.BlockSpec((tm, d), lambda i: (i, 0)), pl.BlockSpec((1, d), lambda i: (0, 0)),
                  pl.BlockSpec((d, ROUTER_PAD), lambda i: (0, 0))],
        out_specs=pl.BlockSpec((tm, ROUTER_PAD), lambda i: (i, 0)),
        out_shape=jax.ShapeDtypeStruct((m, ROUTER_PAD), F32),
        compiler_params=_params(("arbitrary",)),
        name="router_top2",
    )(x, w.reshape(1, d), rpad)


def _row_copy(src_hbm, row, dst, r, sem):
    return pltpu.make_async_copy(src_hbm.at[pl.ds(row, 1)], dst.at[pl.ds(r, 1)], sem)


def _gather_norm_kernel(nused_ref, idx_ref, h_hbm, w_ref, o_ref, buf, sem):
    t = pl.program_id(0)
    tg = buf.shape[1]
    n_used = nused_ref[0]

    def issue(tile, slot):
        def body(r2, carry):
            for u in range(2):
                r = 2 * r2 + u
                _row_copy(h_hbm, idx_ref[tile * tg + r], buf.at[slot], r, sem.at[slot]).start(priority=u)
            return carry

        lax.fori_loop(0, tg // 2, body, 0, unroll=4)

    @pl.when(jnp.logical_and(t == 0, n_used > 0))
    def _():
        issue(0, 0)

    @pl.when(t + 1 < n_used)
    def _():
        issue(t + 1, (t + 1) % 2)

    @pl.when(t < n_used)
    def _():
        slot = t % 2
        pltpu.make_async_copy(h_hbm.at[pl.ds(0, tg)], buf.at[slot], sem.at[slot]).wait()
        x = buf[slot]
        hn = x * lax.rsqrt(jnp.mean(x * x, axis=-1, keepdims=True) + EPS) * w_ref[...]
        o_ref[...] = hn.astype(o_ref.dtype)

    @pl.when(t >= n_used)
    def _():
        o_ref[...] = jnp.zeros_like(o_ref)


def gather_norm(h, w, src, n_used, n_tiles):
    m, d = h.shape
    tg = MOE_TILE
    return pl.pallas_call(
        _gather_norm_kernel,
        grid_spec=pltpu.PrefetchScalarGridSpec(
            num_scalar_prefetch=2,
            grid=(n_tiles,),
            in_specs=[pl.BlockSpec(memory_space=pl.ANY), pl.BlockSpec((1, d), lambda t, *_: (0, 0))],
            out_specs=pl.BlockSpec((tg, d), lambda t, *_: (t, 0)),
            scratch_shapes=[pltpu.VMEM((2, tg, d), F32), pltpu.SemaphoreType.DMA((2,))],
        ),
        out_shape=jax.ShapeDtypeStruct((n_tiles * tg, d), BF16),
        compiler_params=_params(("arbitrary",)),
        name="moe_gather",
    )(n_used, src, h, w.reshape(1, d))


def _new_expert(te_ref, t):
    return jnp.logical_or(t == 0, te_ref[t] != te_ref[jnp.maximum(t - 1, 0)])


def _used_tile(t, nu):
    return jnp.minimum(t, nu[0] - 1)


def _moe_up_kernel(te_ref, nused_ref, a_ref, wg_ref, wu_ref, o_ref, wgb_ref, wub_ref):
    t = pl.program_id(1)

    @pl.when(_new_expert(te_ref, t))
    def _():
        wgb_ref[...] = _b(wg_ref[...])
        wub_ref[...] = _b(wu_ref[...])

    @pl.when(t < nused_ref[0])
    def _():
        a = a_ref[...]
        o_ref[...] = (_silu(_dot(a, wgb_ref[...])) * _dot(a, wub_ref[...])).astype(o_ref.dtype)

    @pl.when(t >= nused_ref[0])
    def _():
        o_ref[...] = jnp.zeros_like(o_ref)


def moe_up(xs, wg, wu, tile_e, n_used):
    p, k = xs.shape
    f = wg.shape[-1]
    tg = MOE_TILE
    tn = _tile(f, (512, 256, 128))
    w_spec = pl.BlockSpec((None, k, tn), lambda j, t, te, nu: (te[t], 0, j))
    return pl.pallas_call(
        _moe_up_kernel,
        grid_spec=pltpu.PrefetchScalarGridSpec(
            num_scalar_prefetch=2,
            grid=(f // tn, p // tg),
            in_specs=[pl.BlockSpec((tg, k), lambda j, t, te, nu: (_used_tile(t, nu), 0)), w_spec, w_spec],
            out_specs=pl.BlockSpec((tg, tn), lambda j, t, te, nu: (t, j)),
            scratch_shapes=[pltpu.VMEM((k, tn), BF16), pltpu.VMEM((k, tn), BF16)],
        ),
        out_shape=jax.ShapeDtypeStruct((p, f), BF16),
        compiler_params=_params(("arbitrary", "arbitrary")),
        name="moe_up",
    )(tile_e, n_used, xs, wg, wu)


def _moe_down_kernel(te_ref, nused_ref, a_ref, w_ref, o_ref, wb_ref):
    t = pl.program_id(1)

    @pl.when(_new_expert(te_ref, t))
    def _():
        wb_ref[...] = _b(w_ref[...])

    @pl.when(t < nused_ref[0])
    def _():
        o_ref[...] = _dot(a_ref[...], wb_ref[...])

    @pl.when(t >= nused_ref[0])
    def _():
        o_ref[...] = jnp.zeros_like(o_ref)


def moe_down(fs, wd, tile_e, n_used):
    p, f = fs.shape
    d = wd.shape[-1]
    tg = MOE_TILE
    tn = _tile(d, (512, 256, 128))
    return pl.pallas_call(
        _moe_down_kernel,
        grid_spec=pltpu.PrefetchScalarGridSpec(
            num_scalar_prefetch=2,
            grid=(d // tn, p // tg),
            in_specs=[pl.BlockSpec((tg, f), lambda j, t, te, nu: (_used_tile(t, nu), 0)),
                      pl.BlockSpec((None, f, tn), lambda j, t, te, nu: (te[t], 0, j))],
            out_specs=pl.BlockSpec((tg, tn), lambda j, t, te, nu: (t, j)),
            scratch_shapes=[pltpu.VMEM((f, tn), BF16)],
        ),
        out_shape=jax.ShapeDtypeStruct((p, d), F32),
        compiler_params=_params(("arbitrary", "arbitrary")),
        name="moe_down",
    )(tile_e, n_used, fs, wd)


def _combine_norm_kernel(pos_ref, h_hbm, route_hbm, y_hbm, w_ref, o_ref, hbuf, rbuf, buf0, buf1, sem,
                         *, seq_t, m, tiles_per_seq):
    s = pl.program_id(0)
    tm = hbuf.shape[1]

    def first_row(step):
        return (step // tiles_per_seq) * seq_t + (N_PAD + N_META) + (step % tiles_per_seq) * tm

    def copies(base, slot):
        return (pltpu.make_async_copy(h_hbm.at[pl.ds(base, tm)], hbuf.at[slot], sem.at[slot, 0]),
                pltpu.make_async_copy(route_hbm.at[pl.ds(base, tm)], rbuf.at[slot], sem.at[slot, 1]))

    def issue(step, slot):
        base = first_row(step)
        for cp in copies(base, slot):
            cp.start()

        def body(r, carry):
            _row_copy(y_hbm, pos_ref[base + r], buf0.at[slot], r, sem.at[slot, 2]).start(priority=0)
            _row_copy(y_hbm, pos_ref[m + base + r], buf1.at[slot], r, sem.at[slot, 3]).start(priority=1)
            return carry

        lax.fori_loop(0, tm, body, 0, unroll=4)

    @pl.when(s == 0)
    def _():
        issue(0, 0)

    @pl.when(s + 1 < pl.num_programs(0))
    def _():
        issue(s + 1, (s + 1) % 2)

    slot = s % 2
    for cp in copies(0, slot):
        cp.wait()
    pltpu.make_async_copy(y_hbm.at[pl.ds(0, tm)], buf0.at[slot], sem.at[slot, 2]).wait()
    pltpu.make_async_copy(y_hbm.at[pl.ds(0, tm)], buf1.at[slot], sem.at[slot, 3]).wait()
    w1 = rbuf[slot, :, ROUTE_W1:ROUTE_W1 + 1]
    w2 = rbuf[slot, :, ROUTE_W2:ROUTE_W2 + 1]
    x = hbuf[slot] + (w1 * buf0[slot] + w2 * buf1[slot])
    o_ref[...] = (x * lax.rsqrt(jnp.mean(x * x, axis=-1, keepdims=True) + EPS) * w_ref[...]).astype(o_ref.dtype)


def combine_norm(h, route, ys, pos, w, batch, seq_t):
    m, d = h.shape
    seq = seq_t - N_PAD - N_META
    tm = _tile(seq, (256, 128, 64))
    any_spec = pl.BlockSpec(memory_space=pl.ANY)
    return pl.pallas_call(
        functools.partial(_combine_norm_kernel, seq_t=seq_t, m=m, tiles_per_seq=seq // tm),
        grid_spec=pltpu.PrefetchScalarGridSpec(
            num_scalar_prefetch=1,
            grid=(batch * seq // tm,),
            in_specs=[any_spec, any_spec, any_spec, pl.BlockSpec((1, d), lambda s, *_: (0, 0))],
            out_specs=pl.BlockSpec((tm, d), lambda s, *_: (s, 0)),
            scratch_shapes=[pltpu.VMEM((2, tm, d), F32), pltpu.VMEM((2, tm, ROUTER_PAD), F32),
                            pltpu.VMEM((2, tm, d), F32), pltpu.VMEM((2, tm, d), F32),
                            pltpu.SemaphoreType.DMA((2, 4))],
        ),
        out_shape=jax.ShapeDtypeStruct((batch * seq, d), F32),
        compiler_params=_params(("arbitrary",)),
        name="moe_combine",
    )(pos, h, route, ys, w.reshape(1, d))


def moe_routing(route, batch, seq_t):
    m = batch * seq_t
    tg = MOE_TILE
    n_tiles = (2 * m + tg - 1) // tg + N_EXPERTS
    e_flat = jnp.concatenate([route[:, ROUTE_I1], route[:, ROUTE_I2]]).astype(jnp.int32)
    tok = jnp.concatenate([jnp.arange(m, dtype=jnp.int32)] * 2)
    real = (tok % seq_t) >= (N_PAD + N_META)
    onehot = ((e_flat[:, None] == jnp.arange(N_EXPERTS, dtype=jnp.int32)[None, :]) & real[:, None]).astype(jnp.int32)
    csum = jnp.cumsum(onehot, axis=0)
    rank = jnp.sum(csum * onehot, axis=1) - 1
    counts = csum[-1]
    padded = ((counts + tg - 1) // tg) * tg
    ends = jnp.cumsum(padded)
    starts = ends - padded
    pos = jnp.sum(starts[None, :] * onehot, axis=1) + rank
    pos = jnp.where(real, pos, n_tiles * tg)
    src = jnp.zeros((n_tiles * tg,), jnp.int32).at[pos].set(tok, mode="drop")
    n_used = (ends[-1] // tg).astype(jnp.int32)
    tile_start = jnp.arange(n_tiles, dtype=jnp.int32) * tg
    tile_e = jnp.sum((tile_start[:, None] >= ends[None, :]).astype(jnp.int32), axis=1)
    last_e = jnp.sum((((n_used - 1) * tg) >= ends).astype(jnp.int32))
    tile_e = jnp.where(jnp.arange(n_tiles) < n_used, tile_e, last_e)
    return src, pos, tile_e, n_used.reshape(1), n_tiles


def kernel(x, meta_tokens, e_norm_mix, e_w_in, ml_i_bias, ml_f_bias, ml_out_norm, rw_mu, rw_w0, rw_w_up, rw_a0, rw_a_up, rw_g_up, rw_k_k, rw_k_a, rw_r_k, rw_ln_w, rw_ln_b, e_w_out, e_norm_ffn, ffn_w_gate, ffn_w_up, ffn_w_down, o_norm_mix, ssd_w_in, ssd_conv_w, ssd_conv_b, ssd_dt_bias, ssd_A_log, ssd_D, ssd_norm, ssd_w_out, o_norm_ffn, moe_router, moe_w_gate, moe_w_up, moe_w_down, final_norm):
    batch, seq, d = x.shape
    seq_t = N_PAD + N_META + seq
    m = batch * seq_t
    h = jnp.concatenate([jnp.zeros((batch, N_PAD, d), x.dtype),
                         jnp.broadcast_to(meta_tokens.astype(x.dtype), (batch, N_META, d)), x], axis=1)
    h = h.reshape(m, d)

    hn = rmsnorm(h, e_norm_mix[0], BF16)
    p = mm_up(hn, e_w_in[0].T, F32, 1024).reshape(batch, seq_t, -1)
    y_ml = mlstm_group(p, ml_i_bias[0], ml_f_bias[0], ml_out_norm[0]).reshape(m, ML_V)
    y_rw = rwkv7_group(p, rw_mu[0], rw_w0[0], rw_w_up[0], rw_a0[0], rw_a_up[0], rw_g_up[0], rw_k_k[0],
                       rw_k_a[0], rw_r_k[0], rw_ln_w[0], rw_ln_b[0]).reshape(m, RW_DIM)
    h = mm_down([y_ml, y_rw], _b(e_w_out[0]), h, seq_t)
    hn = rmsnorm(h, e_norm_ffn[0], BF16)
    f = swiglu_up(hn, ffn_w_gate[0], ffn_w_up[0])
    h = mm_down([f], _b(ffn_w_down[0]), h, seq_t)

    hn = rmsnorm(h, o_norm_mix[0], BF16)
    p = mm_up(hn, ssd_w_in[0].T, F32, 1024).reshape(batch, seq_t, -1)
    y = ssd_mixer(p, ssd_conv_w[0], ssd_conv_b[0], ssd_dt_bias[0], ssd_A_log[0], ssd_D[0], ssd_norm[0])
    h = mm_down([y.reshape(m, SSD_DI)], _b(ssd_w_out[0]), h, seq_t)
    route = router_top2(h, o_norm_ffn[0], moe_router[0])
    src, pos, tile_e, n_used, n_tiles = moe_routing(route, batch, seq_t)
    xs = gather_norm(h, o_norm_ffn[0], src, n_used, n_tiles)
    fs = moe_up(xs, moe_w_gate[0], moe_w_up[0], tile_e, n_used)
    ys = moe_down(fs, moe_w_down[0], tile_e, n_used)
    out = combine_norm(h, route, ys, pos, final_norm, batch, seq_t)
    return out.reshape(batch, seq, d).astype(x.dtype)
```

```python
import functools

import jax
import jax.numpy as jnp
from jax import lax
from jax.experimental import pallas as pl
from jax.experimental.pallas import tpu as pltpu

F32 = jnp.float32
BF16 = jnp.bfloat16
HI = lax.Precision.HIGHEST

N_META = 16
CHUNK = 64
N_PAD = CHUNK - N_META
NEG = -1e30
EPS = 1e-6

ML_HEADS = 4
ML_DQK = 128
ML_DV = 256
ML_GATE_CAP = 15.0
ML_QK = ML_HEADS * ML_DQK
ML_V = ML_HEADS * ML_DV

RW_HEADS = 16
RW_N = 64
RW_DIM = RW_HEADS * RW_N
RW_DECAY_LORA = 64
RW_A_LORA = 64
RW_GATE_LORA = 160
RW_LN_EPS = 64e-5
RW_GROUP = 256
RW_HPG = RW_GROUP // RW_N
RW_GPS = 4
LORA_ROWS = 384

E_OFF_GATES = 2 * ML_QK + 2 * ML_V
E_OFF_RW = E_OFF_GATES + 2 * ML_HEADS
RW_LANE_OFF = E_OFF_RW % 128

SSD_DI = 4096
SSD_P = 64
SSD_H = SSD_DI // SSD_P
SSD_N = 128
SSD_G = 8
SSD_R = SSD_H // SSD_G
SSD_K = 4
SSD_GW = SSD_DI // SSD_G
SSD_BC = 2 * SSD_G * SSD_N
SSD_DT_PAD = 128

N_EXPERTS = 8
ROUTER_PAD = 128

VMEM_LIMIT = 56 * 1024 * 1024


def _tile(n, prefs):
    for p in prefs:
        if n % p == 0:
            return p
    return n


_ROW_TILES = (1040, 832, 640, 520, 512, 256, 128, 64)


def _params(sem):
    return pltpu.CompilerParams(dimension_semantics=sem, vmem_limit_bytes=VMEM_LIMIT)


def _dot(a, b, prec=None):
    return lax.dot_general(a, b, (((1,), (0,)), ((), ())), precision=prec, preferred_element_type=F32)


def _dot_nt(a, b, prec=None):
    return lax.dot_general(a, b, (((1,), (1,)), ((), ())), precision=prec, preferred_element_type=F32)


def _dot_tn(a, b, prec=None):
    return lax.dot_general(a, b, (((0,), (0,)), ((), ())), precision=prec, preferred_element_type=F32)


def _b(x):
    return x.astype(BF16)


def _bdot(dot, a, b):
    return dot(_b(a), _b(b))


def _dot_split(a, b):
    hi = _b(a)
    lo = _b(a - hi.astype(F32))
    bb = _b(b)
    return _dot(hi, bb) + _dot(lo, bb)


def _sigmoid(x):
    return 1.0 / (1.0 + jnp.exp(-x))


def _softplus(x):
    return jnp.maximum(x, 0.0) + jnp.log1p(jnp.exp(-jnp.abs(x)))


def _silu(x):
    return x * _sigmoid(x)


def _iota(shape, dim):
    return lax.broadcasted_iota(jnp.int32, shape, dim)


def _tri(n):
    return (_iota((n, n), 0) >= _iota((n, n), 1)).astype(F32)


def _interleave(chains):
    chains = list(chains)
    while chains:
        for ch in list(chains):
            try:
                next(ch)
            except StopIteration:
                chains.remove(ch)


def _rmsnorm_kernel(x_ref, w_ref, o_ref):
    x = x_ref[...]
    ms = jnp.mean(x * x, axis=-1, keepdims=True)
    o_ref[...] = (x * lax.rsqrt(ms + EPS) * w_ref[...]).astype(o_ref.dtype)


def rmsnorm(x, w, out_dtype):
    m, d = x.shape
    tm = _tile(m, _ROW_TILES)
    return pl.pallas_call(
        _rmsnorm_kernel,
        grid=(m // tm,),
        in_specs=[pl.BlockSpec((tm, d), lambda i: (i, 0)), pl.BlockSpec((1, d), lambda i: (0, 0))],
        out_specs=pl.BlockSpec((tm, d), lambda i: (i, 0)),
        out_shape=jax.ShapeDtypeStruct((m, d), out_dtype),
        compiler_params=_params(("arbitrary",)),
        name="rmsnorm",
    )(x, w.reshape(1, d))


def _mm_up_kernel(a_ref, wt_ref, o_ref, wb_ref, *, n_valid, last_w):
    j = pl.program_id(0)
    tn = wt_ref.shape[0]

    @pl.when(pl.program_id(1) == 0)
    def _():
        row = j * tn + _iota((tn, 1), 0)
        wt = jnp.where(row < n_valid, wt_ref[...], 0.0)
        wb_ref[...] = _b(wt.T)

    if last_w == tn:
        o_ref[...] = _dot(a_ref[...], wb_ref[...]).astype(o_ref.dtype)
    else:
        @pl.when(j < pl.num_programs(0) - 1)
        def _():
            o_ref[...] = _dot(a_ref[...], wb_ref[...]).astype(o_ref.dtype)

        @pl.when(j == pl.num_programs(0) - 1)
        def _():
            o_ref[:, 0:last_w] = _dot(a_ref[...], wb_ref[:, 0:last_w]).astype(o_ref.dtype)


def mm_up(a, wt, out_dtype, tn):
    m, k = a.shape
    n = wt.shape[0]
    tm = _tile(m, _ROW_TILES)
    nt = pl.cdiv(n, tn)
    last_w = pl.cdiv(n - (nt - 1) * tn, 128) * 128
    return pl.pallas_call(
        functools.partial(_mm_up_kernel, n_valid=n, last_w=last_w),
        grid=(nt, m // tm),
        in_specs=[pl.BlockSpec((tm, k), lambda j, i: (i, 0)), pl.BlockSpec((tn, k), lambda j, i: (j, 0))],
        out_specs=pl.BlockSpec((tm, tn), lambda j, i: (i, j)),
        out_shape=jax.ShapeDtypeStruct((m, (nt - 1) * tn + last_w), out_dtype),
        scratch_shapes=[pltpu.VMEM((k, tn), BF16)],
        compiler_params=_params(("arbitrary", "arbitrary")),
        name="mm_up",
    )(a, wt)


def _swiglu_kernel(a_ref, wg_ref, wu_ref, o_ref, wgb_ref, wub_ref):
    @pl.when(pl.program_id(1) == 0)
    def _():
        wgb_ref[...] = _b(wg_ref[...])
        wub_ref[...] = _b(wu_ref[...])

    a = a_ref[...]
    o_ref[...] = (_silu(_dot(a, wgb_ref[...])) * _dot(a, wub_ref[...])).astype(o_ref.dtype)


def swiglu_up(a, wg, wu):
    m, k = a.shape
    f = wg.shape[-1]
    tm = _tile(m, _ROW_TILES)
    tn = _tile(f, (512, 256, 128))
    w_spec = pl.BlockSpec((k, tn), lambda j, i: (0, j))
    return pl.pallas_call(
        _swiglu_kernel,
        grid=(f // tn, m // tm),
        in_specs=[pl.BlockSpec((tm, k), lambda j, i: (i, 0)), w_spec, w_spec],
        out_specs=pl.BlockSpec((tm, tn), lambda j, i: (i, j)),
        out_shape=jax.ShapeDtypeStruct((m, f), BF16),
        scratch_shapes=[pltpu.VMEM((k, tn), BF16), pltpu.VMEM((k, tn), BF16)],
        compiler_params=_params(("arbitrary", "arbitrary")),
        name="swiglu_up",
    )(a, wg, wu)


def _mm_down_kernel(*refs, tm, seq_t):
    *a_refs, w_ref, r_ref, o_ref = refs
    acc = None
    off = 0
    for a_ref in a_refs:
        k = a_ref.shape[1]
        part = _dot(a_ref[...], w_ref[off:off + k, :])
        acc = part if acc is None else acc + part
        off += k
    t0 = (pl.program_id(0) * tm) % seq_t
    valid = ((t0 + _iota((tm, 1), 0)) >= N_PAD).astype(F32)
    o_ref[...] = r_ref[...] + acc * valid


def mm_down(a_parts, w, res, seq_t):
    m = a_parts[0].shape[0]
    k, n = w.shape
    assert sum(a.shape[1] for a in a_parts) == k
    tm = _tile(seq_t, (520, 512, 256, 128, 64) if k <= 4096 else (320, 256, 128, 64))
    tn = n
    w_spec = pl.BlockSpec((k, tn), lambda i, j: (0, j), pipeline_mode=pl.Buffered(1))
    a_specs = [pl.BlockSpec((tm, a.shape[1]), lambda i, j: (i, 0)) for a in a_parts]
    return pl.pallas_call(
        functools.partial(_mm_down_kernel, tm=tm, seq_t=seq_t),
        grid=(m // tm, n // tn),
        in_specs=a_specs + [w_spec, pl.BlockSpec((tm, tn), lambda i, j: (i, j))],
        out_specs=pl.BlockSpec((tm, tn), lambda i, j: (i, j)),
        out_shape=jax.ShapeDtypeStruct((m, n), F32),
        compiler_params=_params(("arbitrary", "arbitrary")),
        name="mm_down",
    )(*a_parts, w, res)


def _log_sigmoid(x):
    return -_softplus(-x)


def _mlstm_kernel(q_ref, k_ref, v_ref, o_ref, g_ref, bias_ref, onorm_ref, out_ref, c_ref, n_ref, m_ref):
    c = pl.program_id(0)
    L = CHUNK

    @pl.when(c == 0)
    def _():
        c_ref[...] = jnp.zeros_like(c_ref)
        n_ref[...] = jnp.zeros_like(n_ref)
        m_ref[...] = jnp.zeros_like(m_ref)

    valid = (c * L + _iota((L, 1), 0)) >= N_PAD
    causal = _iota((L, L), 0) >= _iota((L, L), 1)
    tri = _tri(L)
    scale = ML_DQK ** -0.5

    def chain(b, h, bcum, ig, b_t, i_t):
        q = q_ref[b, :, h * ML_DQK:(h + 1) * ML_DQK] * scale
        k = k_ref[b, :, h * ML_DQK:(h + 1) * ML_DQK]
        v = v_ref[b, :, h * ML_DV:(h + 1) * ML_DV]
        og = o_ref[b, :, h * ML_DV:(h + 1) * ML_DV]
        bcol = bcum[:, ML_HEADS + h:ML_HEADS + h + 1]
        icol = ig[:, h:h + 1]
        brow = b_t[ML_HEADS + h:ML_HEADS + h + 1, :]
        irow = i_t[h:h + 1, :]
        g = bcol[L - 1:L, :]
        m_prev = m_ref[b, h:h + 1, 0:1]
        c_prev = c_ref[b, h]
        n_prev = n_ref[b, h:h + 1, :]

        dm = jnp.where(causal, bcol - brow + irow, NEG)
        m_inter = bcol + m_prev
        m_t = jnp.maximum(m_inter, jnp.max(dm, axis=-1, keepdims=True))
        yield
        s = _dot_nt(_b(q), _b(k)) * jnp.exp(dm - m_t)
        w_inter = jnp.exp(m_inter - m_t)
        yield
        num = _dot(_b(s), _b(v)) + w_inter * _dot(_b(q), _b(c_prev))
        den = jnp.sum(s, axis=-1, keepdims=True) + w_inter * jnp.sum(q * n_prev, axis=-1, keepdims=True)
        yield
        hh = num / jnp.maximum(jnp.abs(den), jnp.exp(-m_t))
        hh = hh * lax.rsqrt(jnp.mean(hh * hh, axis=-1, keepdims=True) + EPS)
        hh = hh * onorm_ref[:, h * ML_DV:(h + 1) * ML_DV] * _sigmoid(og)
        out_ref[b, :, h * ML_DV:(h + 1) * ML_DV] = hh.astype(out_ref.dtype)
        yield

        a_col = g - bcol + icol
        m_new = jnp.maximum(g + m_prev, jnp.max(a_col, axis=0, keepdims=True))
        dec = jnp.exp(g + m_prev - m_new)
        wk = jnp.exp(a_col - m_new) * k
        yield
        c_ref[b, h] = dec * c_prev + _dot_tn(_b(wk), _b(v))
        n_ref[b, h:h + 1, :] = dec * n_prev + jnp.sum(wk, axis=0, keepdims=True)
        m_ref[b, h:h + 1, :] = jnp.broadcast_to(m_new, (1, m_ref.shape[2]))

    chains = []
    for b in range(q_ref.shape[0]):
        gates = g_ref[b] + bias_ref[...]
        sc = ML_GATE_CAP * jnp.tanh(gates / ML_GATE_CAP)
        ig = jnp.where(valid, sc, NEG)
        lf = jnp.where(valid, _log_sigmoid(sc), 0.0)
        bcum = _dot(tri, lf, HI)
        chains += [chain(b, h, bcum, ig, bcum.T, ig.T) for h in range(ML_HEADS)]
    _interleave(chains)


def mlstm_group(p, i_bias, f_bias, out_norm):
    batch, seq_t, _ = p.shape
    nc = seq_t // CHUNK
    bias = jnp.zeros((1, 128), F32).at[0, 0:ML_HEADS].set(i_bias).at[0, ML_HEADS:2 * ML_HEADS].set(f_bias)
    return pl.pallas_call(
        _mlstm_kernel,
        grid=(nc,),
        in_specs=[
            pl.BlockSpec((batch, CHUNK, ML_QK), lambda c: (0, c, 0)),
            pl.BlockSpec((batch, CHUNK, ML_QK), lambda c: (0, c, 1)),
            pl.BlockSpec((batch, CHUNK, ML_V), lambda c: (0, c, 1)),
            pl.BlockSpec((batch, CHUNK, ML_V), lambda c: (0, c, 2)),
            pl.BlockSpec((batch, CHUNK, 128), lambda c: (0, c, E_OFF_GATES // 128)),
            pl.BlockSpec((1, 128), lambda c: (0, 0)),
            pl.BlockSpec((1, ML_V), lambda c: (0, 0)),
        ],
        out_specs=pl.BlockSpec((batch, CHUNK, ML_V), lambda c: (0, c, 0)),
        out_shape=jax.ShapeDtypeStruct((batch, seq_t, ML_V), BF16),
        scratch_shapes=[pltpu.VMEM((batch, ML_HEADS, ML_DQK, ML_DV), F32), pltpu.VMEM((batch, 8, ML_DQK), F32),
                        pltpu.VMEM((batch, 8, 128), F32)],
        compiler_params=_params(("arbitrary",)),
        name="mlstm",
    )(p, p, p, p, p, bias, out_norm.reshape(1, ML_V))


def _rwkv_kernel(*refs):
    nb = 2 * RW_GPS + 1
    r_blk, k_blk, v_blk = refs[0:nb], refs[nb:2 * nb], refs[2 * nb:3 * nb]
    tail_ref, mu3_ref, mut_ref, chan_ref, wl_ref, out_ref, h_ref, cr_ref, ck_ref, cv_ref, ct_ref = refs[3 * nb:]
    c = pl.program_id(1)
    L = CHUNK
    G = RW_GROUP

    @pl.when(c == 0)
    def _():
        h_ref[...] = jnp.zeros_like(h_ref)
        cr_ref[...] = jnp.zeros_like(cr_ref)
        ck_ref[...] = jnp.zeros_like(ck_ref)
        cv_ref[...] = jnp.zeros_like(cv_ref)
        ct_ref[...] = jnp.zeros_like(ct_ref)

    row = _iota((L, 1), 0)
    valid = (c * L + row) >= N_PAD

    ri = _iota((G, G), 0)
    ci = _iota((G, G), 1)
    same = (ri >> 6) == (ci >> 6)
    tl = ri & (RW_N - 1)
    sl = ci & (RW_N - 1)
    bd_mask = same.astype(F32)
    eye = (ri == ci).astype(F32)
    lower = same & (tl >= sl)
    strict = same & (tl > sl)
    ones_bd = _b(bd_mask)
    tri = _tri(L)
    lane = _iota((L, LORA_ROWS), 1)

    def head_sum(x):
        hi = _b(x)
        return _dot(hi, ones_bd) + _dot(_b(x - hi.astype(F32)), ones_bd)

    def bd(x):
        return _b(jnp.concatenate([x] * RW_HPG, axis=0) * bd_mask)

    def shift_lerp(x, carry_ref, idx, mu):
        prev = carry_ref[idx]
        xs = jnp.where(row == 0, prev, pltpu.roll(x, 1, 0))
        carry_ref[idx] = x[L - 1:L, :]
        return x + mu * (xs - x)

    def unshift(blocks, b, j):
        x = jnp.concatenate([blk[b] for blk in blocks[2 * j:2 * j + 3]], axis=1)
        return x[:, RW_LANE_OFF:RW_LANE_OFF + G]

    def chain(b, j):
        gl = slice(j * G, (j + 1) * G)
        w0 = chan_ref[0:1, gl]
        a0 = chan_ref[1:2, gl]
        k_k = chan_ref[2:3, gl]
        k_a = chan_ref[3:4, gl]
        r_k = chan_ref[4:5, gl]
        ln_w = chan_ref[5:6, gl]
        ln_b = chan_ref[6:7, gl]
        r = shift_lerp(unshift(r_blk, b, j), cr_ref, (b, j), mu3_ref[0:1, gl])
        k = shift_lerp(unshift(k_blk, b, j), ck_ref, (b, j), mu3_ref[1:2, gl])
        v = shift_lerp(unshift(v_blk, b, j), cv_ref, (b, j), mu3_ref[2:3, gl])
        tail = shift_lerp(tail_ref[b], ct_ref, (b, j), mut_ref[...])

        lo1 = RW_LANE_OFF + RW_DECAY_LORA
        lo2 = lo1 + RW_A_LORA
        lo3 = lo2 + RW_GATE_LORA
        act = jnp.where(lane < RW_LANE_OFF, 0.0,
                        jnp.where(lane < lo1, jnp.tanh(tail),
                                  jnp.where(lane < lo2, tail, jnp.where(lane < lo3, _sigmoid(tail), 0.0))))
        lw = _dot(act[:, 0:128], wl_ref[0, 0:128, gl], HI)
        la = _bdot(_dot, act[:, 0:256], wl_ref[1, 0:256, gl])
        gate = _bdot(_dot, act[:, 128:384], wl_ref[2, 128:384, gl])
        yield

        w_log = -_softplus(-(w0 + lw)) - 0.5
        ld = jnp.where(valid, -jnp.exp(w_log), 0.0)
        a = _sigmoid(a0 + la)
        kk = k * k_k
        kk = kk * lax.rsqrt(jnp.maximum(head_sum(kk * kk), 1e-24))
        k2 = k * (1.0 + (a - 1.0) * k_a)
        k2 = jnp.where(valid, k2, 0.0)
        kk = jnp.where(valid, kk, 0.0)
        va = -kk
        vb = kk * a

        yield
        cum = _dot(tri, ld, HI)
        cum_l = cum[L - 1:L, :]
        w_t = jnp.exp(cum)
        w_inv = jnp.exp(-cum)
        w_prev = jnp.exp(cum - ld)
        w_end = jnp.exp(cum_l - cum)
        w_last = jnp.exp(cum_l)

        a_bd = bd(va * w_prev)
        r_bd = bd(r * w_t)
        b_bd = bd(vb * w_inv)
        k_bd = bd(k2 * w_inv)
        bl_bd = bd(vb * w_end)
        kl_bd = bd(k2 * w_end)
        v_bd = bd(v)
        yield

        n_ab = jnp.where(strict, _dot_nt(a_bd, b_bd), 0.0)
        a_ak = _b(jnp.where(strict, _dot_nt(a_bd, k_bd), 0.0))
        m_rb = _b(jnp.where(lower, _dot_nt(r_bd, b_bd), 0.0))
        m_rk = _b(jnp.where(lower, _dot_nt(r_bd, k_bd), 0.0))
        yield

        t_inv = eye + n_ab
        pw = n_ab
        for _ in range(5):
            pw_b = _b(pw)
            pw = _dot(pw_b, pw_b)
            yield
            t_inv = t_inv + _bdot(_dot, t_inv, pw)
            yield

        t_b = _b(t_inv)
        av = _dot(a_ak, v_bd)
        yield
        a_p = _b(_dot(t_b, a_bd))
        u0 = _b(_dot(t_b, _b(av)))
        yield
        p_mat = eye * w_last + _dot_tn(bl_bd, a_p)
        q_mat = _dot_tn(bl_bd, u0) + _dot_tn(kl_bd, v_bd)
        r_p = r_bd.astype(F32) + _dot(m_rb, a_p)
        y0 = _dot(m_rb, u0) + _dot(m_rk, v_bd)
        yield

        h0 = _b(h_ref[b, j])
        y_bd = _dot(_b(r_p), h0) + y0
        h_ref[b, j] = _dot(_b(p_mat), h0) + q_mat
        y = y_bd[0:L] + y_bd[L:2 * L] + y_bd[2 * L:3 * L] + y_bd[3 * L:4 * L]
        yield

        inv_n = 1.0 / RW_N
        mean = head_sum(y) * inv_n
        d = y - mean
        var = head_sum(d * d) * inv_n
        yn = d * lax.rsqrt(var + RW_LN_EPS) * ln_w + ln_b
        bonus = head_sum(r * k2 * r_k) * v
        out_ref[b, :, gl] = ((yn + bonus) * gate).astype(out_ref.dtype)

    _interleave(chain(b, j) for b in range(out_ref.shape[0]) for j in range(RW_GPS))


def rwkv7_group(p, mu, w0, w_up, a0, a_up, g_up, k_k, k_a, r_k, ln_w, ln_b):
    batch, seq_t, _ = p.shape
    nc = seq_t // CHUNK
    ng = RW_DIM // RW_GROUP
    n_lora = RW_DECAY_LORA + RW_A_LORA + RW_GATE_LORA
    o = RW_LANE_OFF
    mu3 = mu[:3 * RW_DIM].reshape(3, RW_DIM)
    mut = jnp.zeros((1, LORA_ROWS), F32).at[0, o:o + n_lora].set(mu[3 * RW_DIM:])
    chan = jnp.stack([w0, a0, k_k, k_a, r_k.reshape(RW_DIM), ln_w, ln_b, jnp.zeros((RW_DIM,), F32)])
    wl = jnp.zeros((3, LORA_ROWS, RW_DIM), F32)
    wl = wl.at[0, o:o + RW_DECAY_LORA].set(w_up)
    wl = wl.at[1, o + RW_DECAY_LORA:o + RW_DECAY_LORA + RW_A_LORA].set(a_up)
    wl = wl.at[2, o + RW_DECAY_LORA + RW_A_LORA:o + n_lora].set(g_up)

    gw = RW_GPS * RW_GROUP

    def col(seg, part):
        first = (E_OFF_RW + seg * RW_DIM - o) // 128
        return pl.BlockSpec((batch, CHUNK, 128), lambda g, c: (0, c, first + 2 * RW_GPS * g + part))

    nb = 2 * RW_GPS + 1
    cols = [col(seg, part) for seg in range(3) for part in range(nb)]
    carry = lambda width: pltpu.VMEM((batch, RW_GPS, 1, width), F32)
    return pl.pallas_call(
        _rwkv_kernel,
        grid=(ng // RW_GPS, nc),
        in_specs=cols + [
            pl.BlockSpec((batch, CHUNK, LORA_ROWS), lambda g, c: (0, c, (E_OFF_RW + 3 * RW_DIM - o) // LORA_ROWS)),
            pl.BlockSpec((3, gw), lambda g, c: (0, g)),
            pl.BlockSpec((1, LORA_ROWS), lambda g, c: (0, 0)),
            pl.BlockSpec((8, gw), lambda g, c: (0, g)),
            pl.BlockSpec((3, LORA_ROWS, gw), lambda g, c: (0, 0, g)),
        ],
        out_specs=pl.BlockSpec((batch, CHUNK, gw), lambda g, c: (0, c, g)),
        out_shape=jax.ShapeDtypeStruct((batch, seq_t, RW_DIM), BF16),
        scratch_shapes=[pltpu.VMEM((batch, RW_GPS, RW_GROUP, RW_GROUP), F32), carry(RW_GROUP), carry(RW_GROUP),
                        carry(RW_GROUP), carry(LORA_ROWS)],
        compiler_params=_params(("arbitrary", "arbitrary")),
        name="rwkv7",
    )(*([p] * (3 * nb + 1)), mu3, mut, chan, wl)


def _ssd_kernel(z_ref, xs_ref, bc_ref, dt_ref, cwx_ref, cwb_ref, cbx_ref, cbb_ref, hp_ref, dexp_ref, nw_ref,
                out_ref, s_ref, px_ref, pb_ref):
    c = pl.program_id(0)
    L = CHUNK

    @pl.when(c == 0)
    def _():
        s_ref[...] = jnp.zeros_like(s_ref)
        px_ref[...] = jnp.zeros_like(px_ref)
        pb_ref[...] = jnp.zeros_like(pb_ref)

    row8 = _iota((8, 1), 0)
    valid = (c * L + _iota((L, 1), 0)) >= N_PAD
    lane = _iota((L, 2 * SSD_P), 1)
    causal = _iota((L, L), 0) >= _iota((L, L), 1)
    tri = _tri(L)

    def conv_silu(x, prev_ref, b, w_ref, b_ref):
        prev = prev_ref[b]
        acc = x * w_ref[SSD_K - 1:SSD_K, :] + b_ref[...]
        for j in range(1, SSD_K):
            xr = pltpu.roll(x, j, 0)
            pr = pltpu.roll(prev, j, 0)
            top = jnp.where(row8 < j, pr, xr[0:8])
            sh = jnp.concatenate([top, xr[8:]], axis=0)
            acc = acc + sh * w_ref[SSD_K - 1 - j:SSD_K - j, :]
        prev_ref[b] = x[L - 8:L, :]
        return _silu(acc)

    def expand(x):
        parts = []
        for j in range(SSD_H // 2):
            lo = jnp.broadcast_to(x[:, 2 * j:2 * j + 1], (L, 2 * SSD_P))
            hi = jnp.broadcast_to(x[:, 2 * j + 1:2 * j + 2], (L, 2 * SSD_P))
            parts.append(jnp.where(lane < SSD_P, lo, hi))
        return jnp.concatenate(parts, axis=1)

    def chain(b):
        xs = conv_silu(xs_ref[b], px_ref, b, cwx_ref, cbx_ref)
        yield
        bc = conv_silu(bc_ref[b], pb_ref, b, cwb_ref, cbb_ref)
        yield

        dt = jnp.where(valid, _softplus(dt_ref[b] + hp_ref[0:1, :]), 0.0)
        a = dt * (-jnp.exp(hp_ref[1:2, :]))
        acum = _dot(tri, a, HI)
        a_t = acum.T
        dt_t = dt.T
        a_last = acum[L - 1:L, :]
        cdec = jnp.exp(a_last)
        yield
        xw = xs * expand(dt * jnp.exp(a_last - acum))
        yield
        e_ea = expand(jnp.exp(acum))
        yield

        ys = []
        for g in range(SSD_G):
            bg = bc[:, g * SSD_N:(g + 1) * SSD_N]
            cg = bc[:, SSD_G * SSD_N + g * SSD_N:SSD_G * SSD_N + (g + 1) * SSD_N]
            cb = _dot_nt(_b(cg), _b(bg))
            s_prev = s_ref[b, g]
            y_inter = _dot_nt(_b(cg), _b(s_prev)) * e_ea[:, g * SSD_GW:(g + 1) * SSD_GW]
            yield
            parts = []
            cd_rows = []
            for r in range(SSD_R):
                h = g * SSD_R + r
                seg = acum[:, h:h + 1] - a_t[h:h + 1, :]
                ldec = jnp.exp(jnp.where(causal, seg, NEG))
                parts.append(_dot(_b(cb * ldec * dt_t[h:h + 1, :]), _b(xs[:, h * SSD_P:(h + 1) * SSD_P])))
                cd_rows.append(jnp.broadcast_to(cdec[:, h:h + 1], (SSD_P, SSD_N)))
                if r % 2:
                    yield
            ys.append(jnp.concatenate(parts, axis=1) + y_inter)
            st = _dot_tn(_b(xw[:, g * SSD_GW:(g + 1) * SSD_GW]), _b(bg))
            s_ref[b, g] = jnp.concatenate(cd_rows, axis=0) * s_prev + st
            yield

        y = jnp.concatenate(ys, axis=1) + dexp_ref[...] * xs
        y = y * _silu(z_ref[b])
        yield
        outs = []
        for g in range(SSD_G):
            yg = y[:, g * SSD_GW:(g + 1) * SSD_GW]
            outs.append(yg * lax.rsqrt(jnp.mean(yg * yg, axis=-1, keepdims=True) + EPS))
        out_ref[b] = (jnp.concatenate(outs, axis=1) * nw_ref[...]).astype(out_ref.dtype)

    _interleave(chain(b) for b in range(out_ref.shape[0]))


def ssd_mixer(p, conv_w, conv_b, dt_bias, a_log, d_skip, norm_w):
    batch, seq_t, _ = p.shape
    nc = seq_t // CHUNK
    hp = jnp.zeros((8, SSD_DT_PAD), F32).at[0, :SSD_H].set(dt_bias).at[1, :SSD_H].set(a_log)
    dexp = jnp.repeat(d_skip, SSD_P).reshape(1, SSD_DI)
    conv_b = conv_b.reshape(1, -1)
    return pl.pallas_call(
        _ssd_kernel,
        grid=(nc,),
        in_specs=[
            pl.BlockSpec((batch, CHUNK, SSD_DI), lambda c: (0, c, 0)),
            pl.BlockSpec((batch, CHUNK, SSD_DI), lambda c: (0, c, 1)),
            pl.BlockSpec((batch, CHUNK, SSD_BC), lambda c: (0, c, 2 * SSD_DI // SSD_BC)),
            pl.BlockSpec((batch, CHUNK, SSD_DT_PAD), lambda c: (0, c, (2 * SSD_DI + SSD_BC) // SSD_DT_PAD)),
            pl.BlockSpec((SSD_K, SSD_DI), lambda c: (0, 0)),
            pl.BlockSpec((SSD_K, SSD_BC), lambda c: (0, SSD_DI // SSD_BC)),
            pl.BlockSpec((1, SSD_DI), lambda c: (0, 0)),
            pl.BlockSpec((1, SSD_BC), lambda c: (0, SSD_DI // SSD_BC)),
            pl.BlockSpec((8, SSD_DT_PAD), lambda c: (0, 0)),
            pl.BlockSpec((1, SSD_DI), lambda c: (0, 0)),
            pl.BlockSpec((1, SSD_DI), lambda c: (0, 0)),
        ],
        out_specs=pl.BlockSpec((batch, CHUNK, SSD_DI), lambda c: (0, c, 0)),
        out_shape=jax.ShapeDtypeStruct((batch, seq_t, SSD_DI), BF16),
        scratch_shapes=[pltpu.VMEM((batch, SSD_G, SSD_GW, SSD_N), F32), pltpu.VMEM((batch, 8, SSD_DI), F32),
                        pltpu.VMEM((batch, 8, SSD_BC), F32)],
        compiler_params=_params(("arbitrary",)),
        name="ssd",
    )(p, p, p, p, conv_w, conv_w, conv_b, conv_b, hp, dexp, norm_w.reshape(1, SSD_DI))


MOE_TILE = 512
ROUTE_W1, ROUTE_W2, ROUTE_I1, ROUTE_I2 = 0, 1, 2, 3


def _router_kernel(x_ref, w_ref, r_ref, o_ref):
    x = x_ref[...]
    hn = x * lax.rsqrt(jnp.mean(x * x, axis=-1, keepdims=True) + EPS) * w_ref[...]
    logits = _dot(hn, r_ref[...], HI)
    lane = _iota(logits.shape, 1)
    lane_f = lane.astype(F32)
    logits = jnp.where(lane < N_EXPERTS, logits, -jnp.inf)
    m1 = jnp.max(logits, axis=-1, keepdims=True)
    i1 = jnp.min(jnp.where(logits == m1, lane_f, float(ROUTER_PAD)), axis=-1, keepdims=True)
    rest = jnp.where(lane_f == i1, -jnp.inf, logits)
    m2 = jnp.max(rest, axis=-1, keepdims=True)
    i2 = jnp.min(jnp.where(rest == m2, lane_f, float(ROUTER_PAD)), axis=-1, keepdims=True)
    e2 = jnp.exp(m2 - m1)
    w1 = 1.0 / (1.0 + e2)
    w2 = e2 / (1.0 + e2)
    o_ref[...] = jnp.where(lane == ROUTE_W1, w1, jnp.where(lane == ROUTE_W2, w2,
                           jnp.where(lane == ROUTE_I1, i1, jnp.where(lane == ROUTE_I2, i2, 0.0))))


def router_top2(x, w, router):
    m, d = x.shape
    tm = _tile(m, (520, 512, 256, 128, 64))
    rpad = jnp.zeros((d, ROUTER_PAD), F32).at[:, :N_EXPERTS].set(router)
    return pl.pallas_call(
        _router_kernel,
        grid=(m // tm,),
        in_specs=[pl.BlockSpec((tm, d), lambda i: (i, 0)), pl.BlockSpec((1, d), lambda i: (0, 0)),
                  pl.BlockSpec((d, ROUTER_PAD), lambda i: (0, 0))],
        out_specs=pl.BlockSpec((tm, ROUTER_PAD), lambda i: (i, 0)),
        out_shape=jax.ShapeDtypeStruct((m, ROUTER_PAD), F32),
        compiler_params=_params(("arbitrary",)),
        name="router_top2",
    )(x, w.reshape(1, d), rpad)


def _row_copy(src_hbm, row, dst, r, sem):
    return pltpu.make_async_copy(src_hbm.at[pl.ds(row, 1)], dst.at[pl.ds(r, 1)], sem)


def _gather_norm_kernel(nused_ref, idx_ref, h_hbm, w_ref, o_ref, buf, sem):
    t = pl.program_id(0)
    tg = buf.shape[1]
    n_used = nused_ref[0]

    def issue(tile, slot):
        def body(r2, carry):
            for u in range(2):
                r = 2 * r2 + u
                _row_copy(h_hbm, idx_ref[tile * tg + r], buf.at[slot], r, sem.at[slot]).start(priority=u)
            return carry

        lax.fori_loop(0, tg // 2, body, 0, unroll=4)

    @pl.when(jnp.logical_and(t == 0, n_used > 0))
    def _():
        issue(0, 0)

    @pl.when(t + 1 < n_used)
    def _():
        issue(t + 1, (t + 1) % 2)

    @pl.when(t < n_used)
    def _():
        slot = t % 2
        pltpu.make_async_copy(h_hbm.at[pl.ds(0, tg)], buf.at[slot], sem.at[slot]).wait()
        x = buf[slot]
        hn = x * lax.rsqrt(jnp.mean(x * x, axis=-1, keepdims=True) + EPS) * w_ref[...]
        o_ref[...] = hn.astype(o_ref.dtype)

    @pl.when(t >= n_used)
    def _():
        o_ref[...] = jnp.zeros_like(o_ref)


def gather_norm(h, w, src, n_used, n_tiles):
    m, d = h.shape
    tg = MOE_TILE
    return pl.pallas_call(
        _gather_norm_kernel,
        grid_spec=pltpu.PrefetchScalarGridSpec(
            num_scalar_prefetch=2,
            grid=(n_tiles,),
            in_specs=[pl.BlockSpec(memory_space=pl.ANY), pl.BlockSpec((1, d), lambda t, *_: (0, 0))],
            out_specs=pl.BlockSpec((tg, d), lambda t, *_: (t, 0)),
            scratch_shapes=[pltpu.VMEM((2, tg, d), F32), pltpu.SemaphoreType.DMA((2,))],
        ),
        out_shape=jax.ShapeDtypeStruct((n_tiles * tg, d), BF16),
        compiler_params=_params(("arbitrary",)),
        name="moe_gather",
    )(n_used, src, h, w.reshape(1, d))


def _new_expert(te_ref, t):
    return jnp.logical_or(t == 0, te_ref[t] != te_ref[jnp.maximum(t - 1, 0)])


def _used_tile(t, nu):
    return jnp.minimum(t, nu[0] - 1)


def _moe_up_kernel(te_ref, nused_ref, a_ref, wg_ref, wu_ref, o_ref, wgb_ref, wub_ref):
    t = pl.program_id(1)

    @pl.when(_new_expert(te_ref, t))
    def _():
        wgb_ref[...] = _b(wg_ref[...])
        wub_ref[...] = _b(wu_ref[...])

    @pl.when(t < nused_ref[0])
    def _():
        a = a_ref[...]
        o_ref[...] = (_silu(_dot(a, wgb_ref[...])) * _dot(a, wub_ref[...])).astype(o_ref.dtype)

    @pl.when(t >= nused_ref[0])
    def _():
        o_ref[...] = jnp.zeros_like(o_ref)


def moe_up(xs, wg, wu, tile_e, n_used):
    p, k = xs.shape
    f = wg.shape[-1]
    tg = MOE_TILE
    tn = _tile(f, (512, 256, 128))
    w_spec = pl.BlockSpec((None, k, tn), lambda j, t, te, nu: (te[t], 0, j))
    return pl.pallas_call(
        _moe_up_kernel,
        grid_spec=pltpu.PrefetchScalarGridSpec(
            num_scalar_prefetch=2,
            grid=(f // tn, p // tg),
            in_specs=[pl.BlockSpec((tg, k), lambda j, t, te, nu: (_used_tile(t, nu), 0)), w_spec, w_spec],
            out_specs=pl.BlockSpec((tg, tn), lambda j, t, te, nu: (t, j)),
            scratch_shapes=[pltpu.VMEM((k, tn), BF16), pltpu.VMEM((k, tn), BF16)],
        ),
        out_shape=jax.ShapeDtypeStruct((p, f), BF16),
        compiler_params=_params(("arbitrary", "arbitrary")),
        name="moe_up",
    )(tile_e, n_used, xs, wg, wu)


def _moe_down_kernel(te_ref, nused_ref, a_ref, w_ref, o_ref, wb_ref):
    t = pl.program_id(1)

    @pl.when(_new_expert(te_ref, t))
    def _():
        wb_ref[...] = _b(w_ref[...])

    @pl.when(t < nused_ref[0])
    def _():
        o_ref[...] = _dot(a_ref[...], wb_ref[...])

    @pl.when(t >= nused_ref[0])
    def _():
        o_ref[...] = jnp.zeros_like(o_ref)


def moe_down(fs, wd, tile_e, n_used):
    p, f = fs.shape
    d = wd.shape[-1]
    tg = MOE_TILE
    tn = _tile(d, (512, 256, 128))
    return pl.pallas_call(
        _moe_down_kernel,
        grid_spec=pltpu.PrefetchScalarGridSpec(
            num_scalar_prefetch=2,
            grid=(d // tn, p // tg),
            in_specs=[pl.BlockSpec((tg, f), lambda j, t, te, nu: (_used_tile(t, nu), 0)),
                      pl.BlockSpec((None, f, tn), lambda j, t, te, nu: (te[t], 0, j))],
            out_specs=pl.BlockSpec((tg, tn), lambda j, t, te, nu: (t, j)),
            scratch_shapes=[pltpu.VMEM((f, tn), BF16)],
        ),
        out_shape=jax.ShapeDtypeStruct((p, d), F32),
        compiler_params=_params(("arbitrary", "arbitrary")),
        name="moe_down",
    )(tile_e, n_used, fs, wd)


def _combine_norm_kernel(pos_ref, h_hbm, route_hbm, y_hbm, w_ref, o_ref, hbuf, rbuf, buf0, buf1, sem,
                         *, seq_t, m, tiles_per_seq):
    s = pl.program_id(0)
    tm = hbuf.shape[1]

    def first_row(step):
        return (step // tiles_per_seq) * seq_t + (N_PAD + N_META) + (step % tiles_per_seq) * tm

    def copies(base, slot):
        return (pltpu.make_async_copy(h_hbm.at[pl.ds(base, tm)], hbuf.at[slot], sem.at[slot, 0]),
                pltpu.make_async_copy(route_hbm.at[pl.ds(base, tm)], rbuf.at[slot], sem.at[slot, 1]))

    def issue(step, slot):
        base = first_row(step)
        for cp in copies(base, slot):
            cp.start()

        def body(r, carry):
            _row_copy(y_hbm, pos_ref[base + r], buf0.at[slot], r, sem.at[slot, 2]).start(priority=0)
            _row_copy(y_hbm, pos_ref[m + base + r], buf1.at[slot], r, sem.at[slot, 3]).start(priority=1)
            return carry

        lax.fori_loop(0, tm, body, 0, unroll=4)

    @pl.when(s == 0)
    def _():
        issue(0, 0)

    @pl.when(s + 1 < pl.num_programs(0))
    def _():
        issue(s + 1, (s + 1) % 2)

    slot = s % 2
    for cp in copies(0, slot):
        cp.wait()
    pltpu.make_async_copy(y_hbm.at[pl.ds(0, tm)], buf0.at[slot], sem.at[slot, 2]).wait()
    pltpu.make_async_copy(y_hbm.at[pl.ds(0, tm)], buf1.at[slot], sem.at[slot, 3]).wait()
    w1 = rbuf[slot, :, ROUTE_W1:ROUTE_W1 + 1]
    w2 = rbuf[slot, :, ROUTE_W2:ROUTE_W2 + 1]
    x = hbuf[slot] + (w1 * buf0[slot] + w2 * buf1[slot])
    o_ref[...] = (x * lax.rsqrt(jnp.mean(x * x, axis=-1, keepdims=True) + EPS) * w_ref[...]).astype(o_ref.dtype)


def combine_norm(h, route, ys, pos, w, batch, seq_t):
    m, d = h.shape
    seq = seq_t - N_PAD - N_META
    tm = _tile(seq, (256, 128, 64))
    any_spec = pl.BlockSpec(memory_space=pl.ANY)
    return pl.pallas_call(
        functools.partial(_combine_norm_kernel, seq_t=seq_t, m=m, tiles_per_seq=seq // tm),
        grid_spec=pltpu.PrefetchScalarGridSpec(
            num_scalar_prefetch=1,
            grid=(batch * seq // tm,),
            in_specs=[any_spec, any_spec, any_spec, pl.BlockSpec((1, d), lambda s, *_: (0, 0))],
            out_specs=pl.BlockSpec((tm, d), lambda s, *_: (s, 0)),
            scratch_shapes=[pltpu.VMEM((2, tm, d), F32), pltpu.VMEM((2, tm, ROUTER_PAD), F32),
                            pltpu.VMEM((2, tm, d), F32), pltpu.VMEM((2, tm, d), F32),
                            pltpu.SemaphoreType.DMA((2, 4))],
        ),
        out_shape=jax.ShapeDtypeStruct((batch * seq, d), F32),
        compiler_params=_params(("arbitrary",)),
        name="moe_combine",
    )(pos, h, route, ys, w.reshape(1, d))


def moe_routing(route, batch, seq_t):
    m = batch * seq_t
    tg = MOE_TILE
    n_tiles = (2 * m + tg - 1) // tg + N_EXPERTS
    e_flat = jnp.concatenate([route[:, ROUTE_I1], route[:, ROUTE_I2]]).astype(jnp.int32)
    tok = jnp.concatenate([jnp.arange(m, dtype=jnp.int32)] * 2)
    real = (tok % seq_t) >= (N_PAD + N_META)
    onehot = ((e_flat[:, None] == jnp.arange(N_EXPERTS, dtype=jnp.int32)[None, :]) & real[:, None]).astype(jnp.int32)
    csum = jnp.cumsum(onehot, axis=0)
    rank = jnp.sum(csum * onehot, axis=1) - 1
    counts = csum[-1]
    padded = ((counts + tg - 1) // tg) * tg
    ends = jnp.cumsum(padded)
    starts = ends - padded
    pos = jnp.sum(starts[None, :] * onehot, axis=1) + rank
    pos = jnp.where(real, pos, n_tiles * tg)
    src = jnp.zeros((n_tiles * tg,), jnp.int32).at[pos].set(tok, mode="drop")
    n_used = (ends[-1] // tg).astype(jnp.int32)
    tile_start = jnp.arange(n_tiles, dtype=jnp.int32) * tg
    tile_e = jnp.sum((tile_start[:, None] >= ends[None, :]).astype(jnp.int32), axis=1)
    last_e = jnp.sum((((n_used - 1) * tg) >= ends).astype(jnp.int32))
    tile_e = jnp.where(jnp.arange(n_tiles) < n_used, tile_e, last_e)
    return src, pos, tile_e, n_used.reshape(1), n_tiles


def kernel(x, meta_tokens, e_norm_mix, e_w_in, ml_i_bias, ml_f_bias, ml_out_norm, rw_mu, rw_w0, rw_w_up, rw_a0, rw_a_up, rw_g_up, rw_k_k, rw_k_a, rw_r_k, rw_ln_w, rw_ln_b, e_w_out, e_norm_ffn, ffn_w_gate, ffn_w_up, ffn_w_down, o_norm_mix, ssd_w_in, ssd_conv_w, ssd_conv_b, ssd_dt_bias, ssd_A_log, ssd_D, ssd_norm, ssd_w_out, o_norm_ffn, moe_router, moe_w_gate, moe_w_up, moe_w_down, final_norm):
    batch, seq, d = x.shape
    seq_t = N_PAD + N_META + seq
    m = batch * seq_t
    h = jnp.concatenate([jnp.zeros((batch, N_PAD, d), x.dtype),
                         jnp.broadcast_to(meta_tokens.astype(x.dtype), (batch, N_META, d)), x], axis=1)
    h = h.reshape(m, d)

    hn = rmsnorm(h, e_norm_mix[0], BF16)
    p = mm_up(hn, e_w_in[0].T, F32, 1024).reshape(batch, seq_t, -1)
    y_ml = mlstm_group(p, ml_i_bias[0], ml_f_bias[0], ml_out_norm[0]).reshape(m, ML_V)
    y_rw = rwkv7_group(p, rw_mu[0], rw_w0[0], rw_w_up[0], rw_a0[0], rw_a_up[0], rw_g_up[0], rw_k_k[0],
                       rw_k_a[0], rw_r_k[0], rw_ln_w[0], rw_ln_b[0]).reshape(m, RW_DIM)
    h = mm_down([y_ml, y_rw], _b(e_w_out[0]), h, seq_t)
    hn = rmsnorm(h, e_norm_ffn[0], BF16)
    f = swiglu_up(hn, ffn_w_gate[0], ffn_w_up[0])
    h = mm_down([f], _b(ffn_w_down[0]), h, seq_t)

    hn = rmsnorm(h, o_norm_mix[0], BF16)
    p = mm_up(hn, ssd_w_in[0].T, F32, 1024).reshape(batch, seq_t, -1)
    y = ssd_mixer(p, ssd_conv_w[0], ssd_conv_b[0], ssd_dt_bias[0], ssd_A_log[0], ssd_D[0], ssd_norm[0])
    h = mm_down([y.reshape(m, SSD_DI)], _b(ssd_w_out[0]), h, seq_t)
    route = router_top2(h, o_norm_ffn[0], moe_router[0])
    src, pos, tile_e, n_used, n_tiles = moe_routing(route, batch, seq_t)
    xs = gather_norm(h, o_norm_ffn[0], src, n_used, n_tiles)
    fs = moe_up(xs, moe_w_gate[0], moe_w_up[0], tile_e, n_used)
    ys = moe_down(fs, moe_w_down[0], tile_e, n_used)
    out = combine_norm(h, route, ys, pos, final_norm, batch, seq_t)
    return out.reshape(batch, seq, d).astype(x.dtype)
```
